```python
import math
import jax, jax.numpy as jnp
from jax import lax
import numpy as np

D_MODEL = 4096
BATCH = 1
SEQ = 8192
DEPTH = 1
DEC_BATCH = 2
DEC_SEQ = 8192
PAST_LEN = 128

D_RG = D_MODEL // 2
D_HY = D_MODEL - D_RG
D_MIX = D_RG + D_HY
RG_HEAD_DIM = 128
N_RG_HEADS = D_RG // RG_HEAD_DIM
RG_CONV = 4
RG_CONV_LEFT = 2
RG_C = 8.0
HY_ORDER = 2
HY_CONV = 3
HY_EMB = 33
HY_BANDS = (HY_EMB - 1) // 2
HY_FILTER_WIDTH = 64
HY_FAST_DECAY = 0.3
HY_SLOW_DECAY = 1.5
HY_TARGET = 1e-2
D_FF = 11008
FFN_CONV = 3
EPS = 1e-6
N_IN = 2 * D_RG + (HY_ORDER + 1) * D_HY

kernel_name = "hymba_rglru_hyena_encoder"

F32 = jnp.float32


def rms_norm(x, g):
    xf = x.astype(F32)
    y = xf * lax.rsqrt(jnp.mean(xf * xf, axis=-1, keepdims=True) + EPS)
    return (y * g.astype(F32)).astype(x.dtype)


def dw_conv(x, w, b, left):
    k_w = w.shape[0]
    L = x.shape[1]
    xp = jnp.pad(x, ((0, 0), (left, k_w - 1 - left), (0, 0)))
    y = xp[:, 0:L, :] * w[0] + b
    for k in range(1, k_w):
        y = y + xp[:, k:k + L, :] * w[k]
    return y


def block_diag(x, w, b):
    B, L, _ = x.shape
    xh = x.reshape(B, L, N_RG_HEADS, RG_HEAD_DIM)
    return jnp.einsum("blhi,hij->blhj", xh, w).reshape(B, L, D_RG) + b


def _linear_combine(left, right):
    a1, b1 = left
    a2, b2 = right
    return a1 * a2, a2 * b1 + b2


def rg_lru(xc, wa, ba, wx, bx, lam, reverse):
    xf = xc.astype(F32)
    r = jax.nn.sigmoid(block_diag(xf, wa.astype(F32), ba.astype(F32)))
    i = jax.nn.sigmoid(block_diag(xf, wx.astype(F32), bx.astype(F32)))
    log_a = -RG_C * r * jax.nn.softplus(-lam.astype(F32))
    a = jnp.exp(log_a)
    b = jnp.sqrt(-jnp.expm1(2.0 * log_a)) * (i * xf)
    _, h = lax.associative_scan(_linear_combine, (a, b), axis=1, reverse=reverse)
    return h


def hyena_filters(L, w1, b1, w2, b2, w3, b3, w4, freq):
    t = jnp.linspace(0.0, 1.0, L, dtype=F32)[:, None]
    n = jnp.arange(L, dtype=F32)[:, None]
    bands = jnp.linspace(1e-4, HY_BANDS - 1, HY_BANDS, dtype=F32)
    ang = (2.0 * math.pi / L) * n * bands
    z = jnp.concatenate([t, jnp.cos(ang), -jnp.sin(ang)], axis=-1)
    fr = freq.astype(F32)
    h = jnp.sin(fr * (z @ w1.astype(F32) + b1.astype(F32)))
    h = jnp.sin(fr * (h @ w2.astype(F32) + b2.astype(F32)))
    h = jnp.sin(fr * (h @ w3.astype(F32) + b3.astype(F32)))
    h = (h @ w4.astype(F32)).reshape(L, HY_ORDER, 2, D_HY)
    max_decay = math.log(HY_TARGET) / HY_FAST_DECAY
    min_decay = math.log(HY_TARGET) / HY_SLOW_DECAY
    deltas = jnp.linspace(min_decay, max_decay, D_HY, dtype=F32)
    decay = jnp.exp(-t * jnp.abs(deltas))
    h = h * decay[:, None, None, :]
    h_fwd = h[:, :, 0, :]
    h_bwd = h[1:, :, 1, :][::-1]
    k = jnp.concatenate([h_fwd, jnp.zeros((1, HY_ORDER, D_HY), F32), h_bwd], axis=0)
    return jnp.fft.rfft(k, axis=0)


def long_conv(u, k_spec, bias):
    L = u.shape[1]
    U = jnp.fft.rfft(u, n=2 * L, axis=1)
    y = jnp.fft.irfft(U * k_spec[None], n=2 * L, axis=1)[:, :L]
    return y + u * bias.astype(F32)


def encoder_layer(x, pre_mix_norm, w_in, rg_conv_w, rg_conv_b, rg_a_w, rg_a_b, rg_x_w, rg_x_b,
                  rg_lambda, hy_conv_w, hy_conv_b, hy_w1, hy_b1, hy_w2, hy_b2, hy_w3, hy_b3,
                  hy_w4, hy_sin_freq, hy_bias, rg_out_norm, hy_out_norm, w_out, post_mix_norm,
                  pre_ffn_norm, w_ffn_in, ffn_conv_w, ffn_conv_b, w_ffn_out, post_ffn_norm):
    dt = x.dtype
    L = x.shape[1]
    hn = rms_norm(x, pre_mix_norm)
    proj = hn @ w_in
    rg_x = proj[..., :D_RG]
    rg_g = proj[..., D_RG:2 * D_RG]
    hy_u = proj[..., 2 * D_RG:]
    xc = dw_conv(rg_x, rg_conv_w, rg_conv_b, RG_CONV_LEFT)
    h_f = rg_lru(xc, rg_a_w[0], rg_a_b[0], rg_x_w[0], rg_x_b[0], rg_lambda[0], False)
    h_b = rg_lru(xc, rg_a_w[1], rg_a_b[1], rg_x_w[1], rg_x_b[1], rg_lambda[1], True)
    rg_y = ((h_f + h_b) * jax.nn.gelu(rg_g.astype(F32))).astype(dt)
    hu = dw_conv(hy_u, hy_conv_w, hy_conv_b, (HY_CONV - 1) // 2).astype(F32)
    streams = jnp.split(hu, HY_ORDER + 1, axis=-1)
    k_spec = hyena_filters(L, hy_w1, hy_b1, hy_w2, hy_b2, hy_w3, hy_b3, hy_w4, hy_sin_freq)
    z = streams[0]
    for o in range(HY_ORDER):
        z = streams[o + 1] * long_conv(z, k_spec[:, o], hy_bias[o])
    hy_y = z.astype(dt)
    mix = jnp.concatenate([rms_norm(rg_y, rg_out_norm), rms_norm(hy_y, hy_out_norm)], axis=-1)
    x = x + rms_norm(mix @ w_out, post_mix_norm)
    hn = rms_norm(x, pre_ffn_norm)
    gu = hn @ w_ffn_in
    g = dw_conv(gu[..., :D_FF], ffn_conv_w, ffn_conv_b, (FFN_CONV - 1) // 2)
    f = (jax.nn.gelu(g) * gu[..., D_FF:]) @ w_ffn_out
    return x + rms_norm(f, post_ffn_norm)


def run_trunk(x, params):
    for l in range(DEPTH):
        x = encoder_layer(x, *[p[l] for p in params])
    return x


def setup_inputs(seed: int = 0) -> dict:
    key = jax.random.key(seed)
    ks = iter(jax.random.split(key, 48))
    nrm = lambda shape, scale: scale * jax.random.normal(next(ks), shape, F32)
    gain = lambda shape: 1.0 + 0.05 * jax.random.normal(next(ks), shape, F32)
    u = jax.random.uniform(next(ks), (DEPTH, 2, D_RG), F32, minval=0.9, maxval=0.999)
    base = u ** (1.0 / RG_C)
    rg_lambda = jnp.log(base) - jnp.log1p(-base)
    return {
        "x_prompt": nrm((BATCH, SEQ, D_MODEL), 1.0),
        "x_sample": nrm((DEC_BATCH, DEC_SEQ, D_MODEL), 1.0),
        "pre_mix_norm": gain((DEPTH, D_MODEL)),
        "w_in": nrm((DEPTH, D_MODEL, N_IN), D_MODEL ** -0.5),
        "rg_conv_w": nrm((DEPTH, RG_CONV, D_RG), RG_CONV ** -0.5),
        "rg_conv_b": nrm((DEPTH, D_RG), 0.02),
        "rg_a_w": nrm((DEPTH, 2, N_RG_HEADS, RG_HEAD_DIM, RG_HEAD_DIM), RG_HEAD_DIM ** -0.5),
        "rg_a_b": nrm((DEPTH, 2, D_RG), 0.02),
        "rg_x_w": nrm((DEPTH, 2, N_RG_HEADS, RG_HEAD_DIM, RG_HEAD_DIM), RG_HEAD_DIM ** -0.5),
        "rg_x_b": nrm((DEPTH, 2, D_RG), 0.02),
        "rg_lambda": rg_lambda,
        "hy_conv_w": nrm((DEPTH, HY_CONV, (HY_ORDER + 1) * D_HY), HY_CONV ** -0.5),
        "hy_conv_b": nrm((DEPTH, (HY_ORDER + 1) * D_HY), 0.02),
        "hy_w1": nrm((DEPTH, HY_EMB, HY_FILTER_WIDTH), HY_EMB ** -0.5),
        "hy_b1": nrm((DEPTH, HY_FILTER_WIDTH), 0.02),
        "hy_w2": nrm((DEPTH, HY_FILTER_WIDTH, HY_FILTER_WIDTH), HY_FILTER_WIDTH ** -0.5),
        "hy_b2": nrm((DEPTH, HY_FILTER_WIDTH), 0.02),
        "hy_w3": nrm((DEPTH, HY_FILTER_WIDTH, HY_FILTER_WIDTH), HY_FILTER_WIDTH ** -0.5),
        "hy_b3": nrm((DEPTH, HY_FILTER_WIDTH), 0.02),
        "hy_w4": nrm((DEPTH, HY_FILTER_WIDTH, HY_ORDER * 2 * D_HY), HY_FILTER_WIDTH ** -0.5),
        "hy_sin_freq": gain((DEPTH, HY_FILTER_WIDTH)),
        "hy_bias": nrm((DEPTH, HY_ORDER, D_HY), 1.0),
        "rg_out_norm": gain((DEPTH, D_RG)),
        "hy_out_norm": gain((DEPTH, D_HY)),
        "w_out": nrm((DEPTH, D_MIX, D_MODEL), D_MIX ** -0.5),
        "post_mix_norm": gain((DEPTH, D_MODEL)),
        "pre_ffn_norm": gain((DEPTH, D_MODEL)),
        "w_ffn_in": nrm((DEPTH, D_MODEL, 2 * D_FF), D_MODEL ** -0.5),
        "ffn_conv_w": nrm((DEPTH, FFN_CONV, D_FF), FFN_CONV ** -0.5),
        "ffn_conv_b": nrm((DEPTH, D_FF), 0.02),
        "w_ffn_out": nrm((DEPTH, D_FF, D_MODEL), D_FF ** -0.5),
        "post_ffn_norm": gain((DEPTH, D_MODEL)),
    }


def reference(x_prompt, x_sample, pre_mix_norm, w_in, rg_conv_w, rg_conv_b, rg_a_w, rg_a_b,
              rg_x_w, rg_x_b, rg_lambda, hy_conv_w, hy_conv_b, hy_w1, hy_b1, hy_w2, hy_b2,
              hy_w3, hy_b3, hy_w4, hy_sin_freq, hy_bias, rg_out_norm, hy_out_norm, w_out,
              post_mix_norm, pre_ffn_norm, w_ffn_in, ffn_conv_w, ffn_conv_b, w_ffn_out,
              post_ffn_norm):
    params = (pre_mix_norm, w_in, rg_conv_w, rg_conv_b, rg_a_w, rg_a_b, rg_x_w, rg_x_b,
              rg_lambda, hy_conv_w, hy_conv_b, hy_w1, hy_b1, hy_w2, hy_b2, hy_w3, hy_b3,
              hy_w4, hy_sin_freq, hy_bias, rg_out_norm, hy_out_norm, w_out, post_mix_norm,
              pre_ffn_norm, w_ffn_in, ffn_conv_w, ffn_conv_b, w_ffn_out, post_ffn_norm)
    y_prompt = run_trunk(x_prompt, params)
    y_sample = run_trunk(x_sample, params)
    return (y_prompt, y_sample)
```

```python
import functools
import math

import numpy as np
import jax
import jax.numpy as jnp
from jax import lax
from jax.experimental import pallas as pl
from jax.experimental.pallas import tpu as pltpu

F32 = jnp.float32
BF16 = jnp.bfloat16

EPS = 1e-6
RG_C = 8.0
RG_CONV_LEFT = 2
HY_FAST_DECAY = 0.3
HY_SLOW_DECAY = 1.5
HY_TARGET = 1e-2

LANES = 128
SUBLANES = 8
BF16_ROWS = 16
VMEM_LIMIT = 56 * 1024 * 1024


def _pick(n, candidates):
    for c in candidates:
        if n % c == 0:
            return c
    raise ValueError(f"no tile in {candidates} divides {n}")


def _params(sem):
    return pltpu.CompilerParams(dimension_semantics=sem, vmem_limit_bytes=VMEM_LIMIT)


def _rms(x, g):
    return x * lax.rsqrt(jnp.mean(x * x, axis=-1, keepdims=True) + EPS) * g


def _norm_in_kernel(x_ref, g_ref, hn_ref, hnt_ref):
    y = _rms(x_ref[...], g_ref[...])
    hn_ref[...] = y.astype(BF16)
    hnt_ref[...] = y.T.astype(BF16)


def _norm_in(x, g):
    T, D = x.shape
    tm = _pick(T, (256, 128))
    return pl.pallas_call(
        _norm_in_kernel,
        grid=(T // tm,),
        in_specs=[pl.BlockSpec((tm, D), lambda i: (i, 0)), pl.BlockSpec((1, D), lambda i: (0, 0))],
        out_specs=[pl.BlockSpec((tm, D), lambda i: (i, 0)), pl.BlockSpec((D, tm), lambda i: (0, i))],
        out_shape=[jax.ShapeDtypeStruct((T, D), BF16), jax.ShapeDtypeStruct((D, T), BF16)],
        compiler_params=_params(("parallel",)),
        name="norm_in",
    )(x, g.reshape(1, D))


def _mm_kernel(a_ref, b_ref, o_ref):
    o_ref[...] = jnp.dot(a_ref[...], b_ref[...], preferred_element_type=F32).astype(o_ref.dtype)


def _mm(a, b, *, tm, tn, name):
    M, K = a.shape
    _, N = b.shape
    return pl.pallas_call(
        _mm_kernel,
        grid=(N // tn, M // tm),
        in_specs=[pl.BlockSpec((tm, K), lambda j, i: (i, 0)), pl.BlockSpec((K, tn), lambda j, i: (0, j))],
        out_specs=pl.BlockSpec((tm, tn), lambda j, i: (i, j)),
        out_shape=jax.ShapeDtypeStruct((M, N), F32),
        compiler_params=_params(("parallel", "parallel")),
        name=name,
    )(a, b)


def _one_minus_exp(y, a):
    series = -y * (1.0 + y * (0.5 + y * (1.0 / 6.0 + y * (1.0 / 24.0))))
    return jnp.where(y > -0.03, series, 1.0 - a * a)


def _scan_chunk(a, b, carry, reverse):
    R = a.shape[0]
    nv = R // SUBLANES
    a3 = a.reshape(nv, SUBLANES, LANES)
    b3 = b.reshape(nv, SUBLANES, LANES)
    row = lax.broadcasted_iota(jnp.int32, (nv, SUBLANES, LANES), 1)
    for d in (1, 2, 4):
        shift = SUBLANES - d if reverse else d
        valid = (row < SUBLANES - d) if reverse else (row >= d)
        sa = pltpu.roll(a3, shift, axis=1)
        sb = pltpu.roll(b3, shift, axis=1)
        b3 = b3 + jnp.where(valid, a3, 0.0) * sb
        a3 = a3 * jnp.where(valid, sa, 1.0)
    edge = 0 if reverse else SUBLANES - 1
    a_edge = jnp.broadcast_to(a3[:, edge:edge + 1, :], a3.shape)
    b_edge = jnp.broadcast_to(b3[:, edge:edge + 1, :], b3.shape)
    order = range(nv - 1, -1, -1) if reverse else range(nv)
    carries = [None] * nv
    for j in order:
        carries[j] = carry
        carry = b_edge[j] + a_edge[j] * carry
    h3 = b3 + a3 * jnp.stack(carries)
    return h3.reshape(R, LANES), carry


def _rglru_kernel(x_ref, g_ref, cw_ref, cb_ref, w_ref, bias_ref, lam_ref, o_ref, xp_ref, xc_ref, hf_ref, *, L, R):
    pad = SUBLANES
    zeros = jnp.zeros((pad, LANES), F32)
    xp_ref[0:pad, :] = zeros
    xp_ref[L + pad:L + 2 * pad, :] = zeros
    xp_ref[pad:L + pad, :] = x_ref[0]
    cw = cw_ref[...]
    cb = cb_ref[...]
    lam = lam_ref[...]
    decay = RG_C * jax.nn.softplus(-lam)
    nc = L // R
    hw = 2 * LANES

    def gates(xc, d):
        z = jnp.dot(xc.astype(BF16), w_ref[0, :, d * hw:(d + 1) * hw], preferred_element_type=F32)
        z = z + bias_ref[0, :, d * hw:(d + 1) * hw]
        r = jax.nn.sigmoid(z[:, :LANES])
        i = jax.nn.sigmoid(z[:, LANES:])
        log_a = -(r * decay[d:d + 1, :])
        a = jnp.exp(log_a)
        b = jnp.sqrt(_one_minus_exp(2.0 * log_a, a)) * (i * xc)
        return a, b

    def fwd(c, carry):
        base = pl.multiple_of(c * R, R)
        big = xp_ref[pl.ds(base, R + 2 * pad), :]
        xc = cb
        for k in range(cw.shape[0]):
            off = pad + k - RG_CONV_LEFT
            xc = xc + big[off:off + R, :] * cw[k:k + 1, :]
        xc_ref[pl.ds(base, R), :] = xc
        a, b = gates(xc, 0)
        h, carry = _scan_chunk(a, b, carry, False)
        hf_ref[pl.ds(base, R), :] = h
        return carry

    lax.fori_loop(0, nc, fwd, jnp.zeros((SUBLANES, LANES), F32))

    def bwd(s, carry):
        base = pl.multiple_of((nc - 1 - s) * R, R)
        xc = xc_ref[pl.ds(base, R), :]
        a, b = gates(xc, 1)
        h, carry = _scan_chunk(a, b, carry, True)
        gate = jax.nn.gelu(g_ref[0, pl.ds(base, R), :])
        o_ref[0, pl.ds(base, R), :] = (hf_ref[pl.ds(base, R), :] + h) * gate
        return carry

    lax.fori_loop(0, nc, bwd, jnp.zeros((SUBLANES, LANES), F32))


def _rglru(proj, cw, cb, w_cat, b_cat, lam, *, d_rg):
    Bt, L, _ = proj.shape
    nh = d_rg // LANES
    R = _pick(L, (256, 128, 64, 32, 16, 8))
    kern = functools.partial(_rglru_kernel, L=L, R=R)
    return pl.pallas_call(
        kern,
        grid=(Bt, nh),
        in_specs=[
            pl.BlockSpec((1, L, LANES), lambda b, h: (b, 0, h)),
            pl.BlockSpec((1, L, LANES), lambda b, h: (b, 0, nh + h)),
            pl.BlockSpec((cw.shape[0], LANES), lambda b, h: (0, h)),
            pl.BlockSpec((1, LANES), lambda b, h: (0, h)),
            pl.BlockSpec((1, LANES, 4 * LANES), lambda b, h: (h, 0, 0)),
            pl.BlockSpec((1, 1, 4 * LANES), lambda b, h: (h, 0, 0)),
            pl.BlockSpec((2, LANES), lambda b, h: (0, h)),
        ],
        out_specs=pl.BlockSpec((1, L, LANES), lambda b, h: (b, 0, h)),
        out_shape=jax.ShapeDtypeStruct((Bt, L, d_rg), F32),
        scratch_shapes=[
            pltpu.VMEM((L + 2 * SUBLANES, LANES), F32),
            pltpu.VMEM((L, LANES), F32),
            pltpu.VMEM((L, LANES), F32),
        ],
        compiler_params=_params(("parallel", "parallel")),
        name="rglru",
    )(proj, proj, cw, cb, w_cat, b_cat, lam)


def _dft_tables(L):
    n = 2 * L
    nb = n // LANES
    kb = -(-(nb // 2 + 1) // BF16_ROWS) * BF16_ROWS
    k = np.arange(kb, dtype=np.float64)[:, None]
    keep = (np.arange(kb) <= nb // 2)[:, None]
    j = np.arange(nb, dtype=np.float64)[None, :]
    ang = 2.0 * np.pi * k * j / nb
    f1 = np.concatenate([np.where(keep, np.cos(ang), 0.0), np.where(keep, -np.sin(ang), 0.0)], 0)
    la = np.arange(LANES, dtype=np.float64)[None, :]
    tw = np.stack([np.cos(2.0 * np.pi * k * la / n), -np.sin(2.0 * np.pi * k * la / n)])
    aa = 2.0 * np.pi * np.outer(np.arange(LANES), np.arange(LANES)) / LANES
    cs = np.concatenate([np.cos(aa), np.sin(aa)], 1)
    wgt = np.where((np.arange(kb) == 0) | (np.arange(kb) == nb // 2), 1.0,
                   np.where(np.arange(kb) < nb // 2, 2.0, 0.0))[None, :]
    jr = np.arange(nb // 2, dtype=np.float64)[:, None]
    kc = np.arange(kb, dtype=np.float64)[None, :]
    ang2 = 2.0 * np.pi * jr * kc / nb
    g = np.concatenate([wgt * np.cos(ang2), -wgt * np.sin(ang2)], 1) / n
    return dict(
        nb=nb, kb=kb,
        f1_full=jnp.asarray(f1, BF16), f1_half=jnp.asarray(f1[:, :nb // 2], BF16),
        tw=jnp.asarray(tw, F32), cs=jnp.asarray(cs, BF16), g=jnp.asarray(g, BF16),
    )


def _stage1_pair(f1, u_pair, tw_ref, dst_ref, c0):
    kb = tw_ref.shape[1]
    y = jnp.dot(f1, u_pair, preferred_element_type=F32)
    twr = tw_ref[0]
    twi = tw_ref[1]
    for q in range(2):
        yr = y[:kb, q * LANES:(q + 1) * LANES]
        yi = y[kb:, q * LANES:(q + 1) * LANES]
        dst_ref[c0 + q, 0] = (yr * twr - yi * twi).astype(BF16)
        dst_ref[c0 + q, 1] = (yr * twi + yi * twr).astype(BF16)


def _stage2_group(src_ref, cs_ref, c0, gc):
    kb = src_ref.shape[2]
    a2 = src_ref[pl.ds(c0, gc)].reshape(gc * 2 * kb, LANES)
    m = jnp.dot(a2, cs_ref[...], preferred_element_type=F32).reshape(gc, 2, kb, 2 * LANES)
    return m[:, 0, :, :LANES], m[:, 0, :, LANES:], m[:, 1, :, :LANES], m[:, 1, :, LANES:]


def _filter_time_kernel(zt_ref, w1_ref, w2_ref, w3_ref, col_ref, w4_ref, delta_ref, o_ref, h3_ref, *, L, lt):
    j = pl.program_id(0)

    @pl.when(pl.program_id(1) == 0)
    def _():
        fr = col_ref[:, 0:1]
        h = jnp.sin(fr * (jnp.dot(w1_ref[...], zt_ref[...].astype(BF16), preferred_element_type=F32) + col_ref[:, 1:2]))
        h = jnp.sin(fr * (jnp.dot(w2_ref[...], h.astype(BF16), preferred_element_type=F32) + col_ref[:, 2:3]))
        h = jnp.sin(fr * (jnp.dot(w3_ref[...], h.astype(BF16), preferred_element_type=F32) + col_ref[:, 3:4]))
        h3_ref[...] = h.astype(BF16)

    k = jnp.dot(w4_ref[...], h3_ref[...], preferred_element_type=F32)
    t = zt_ref[0:1, :]
    k = k * jnp.exp(-t * jnp.abs(delta_ref[:, 0:1]))
    pos = j * lt + lax.broadcasted_iota(jnp.int32, k.shape, 1)
    o_ref[...] = jnp.where(pos == L, 0.0, k)


def _filter_time(zt, w1t, w2t, w3t, cols, w4t, delta, *, L, d_hy, n_order):
    fw = w2t.shape[0]
    lt = _pick(2 * L, (1024, 512, 256, 128))
    nt = 2 * L // lt
    tr = _pick(d_hy, (512, 256, 128))
    nr = d_hy // tr
    kern = functools.partial(_filter_time_kernel, L=L, lt=lt)

    def w4_map(j, r):
        o = r // nr
        direction = (j >= nt // 2).astype(jnp.int32)
        return ((o * 2 + direction) * nr + r % nr, 0)

    return pl.pallas_call(
        kern,
        grid=(nt, n_order * nr),
        in_specs=[
            pl.BlockSpec((zt.shape[0], lt), lambda j, r: (0, j)),
            pl.BlockSpec(w1t.shape, lambda j, r: (0, 0)),
            pl.BlockSpec(w2t.shape, lambda j, r: (0, 0)),
            pl.BlockSpec(w3t.shape, lambda j, r: (0, 0)),
            pl.BlockSpec(cols.shape, lambda j, r: (0, 0)),
            pl.BlockSpec((tr, fw), w4_map),
            pl.BlockSpec((tr, LANES), lambda j, r: (r % nr, 0)),
        ],
        out_specs=pl.BlockSpec((tr, lt), lambda j, r: (r, j)),
        out_shape=jax.ShapeDtypeStruct((n_order * d_hy, 2 * L), F32),
        scratch_shapes=[pltpu.VMEM((fw, lt), BF16)],
        compiler_params=_params(("arbitrary", "arbitrary")),
        name="hy_filter_time",
    )(zt, w1t, w2t, w3t, cols, w4t, delta)


def _filter_spec_kernel(k_ref, f1_ref, tw_ref, cs_ref, o_ref, a_ref, *, cb, gc):
    f1 = f1_ref[...]

    def s1(p, _):
        u = jnp.concatenate([k_ref[2 * p], k_ref[2 * p + 1]], axis=1).astype(BF16)
        _stage1_pair(f1, u, tw_ref, a_ref, 2 * p)
        return 0

    lax.fori_loop(0, cb // 2, s1, 0)

    def s2(gi, _):
        c0 = pl.multiple_of(gi * gc, gc)
        rc, rs, ic, is_ = _stage2_group(a_ref, cs_ref, c0, gc)
        o_ref[pl.ds(c0, gc), 0] = rc + is_
        o_ref[pl.ds(c0, gc), 1] = ic - rs
        return 0

    lax.fori_loop(0, cb // gc, s2, 0)


def _filter_spec(kt, tabs, *, L):
    C = kt.shape[0]
    nb, kb = tabs["nb"], tabs["kb"]
    cb = _pick(C, (32, 16, 8, 4, 2))
    gc = _pick(cb, (8, 4, 2))
    kern = functools.partial(_filter_spec_kernel, cb=cb, gc=gc)
    return pl.pallas_call(
        kern,
        grid=(C // cb,),
        in_specs=[
            pl.BlockSpec((cb, nb, LANES), lambda i: (i, 0, 0)),
            pl.BlockSpec((2 * kb, nb), lambda i: (0, 0)),
            pl.BlockSpec((2, kb, LANES), lambda i: (0, 0, 0)),
            pl.BlockSpec((LANES, 2 * LANES), lambda i: (0, 0)),
        ],
        out_specs=pl.BlockSpec((cb, 2, kb, LANES), lambda i: (i, 0, 0, 0)),
        out_shape=jax.ShapeDtypeStruct((C, 2, kb, LANES), F32),
        scratch_shapes=[pltpu.VMEM((cb, 2, kb, LANES), BF16)],
        compiler_params=_params(("parallel",)),
        name="hy_filter_spec",
    )(kt.reshape(C, nb, LANES), tabs["f1_full"], tabs["tw"], tabs["cs"])


def _shift_time(x, step):
    rows = x.shape[1]
    lane = lax.broadcasted_iota(jnp.int32, x.shape, 2)
    row = lax.broadcasted_iota(jnp.int32, x.shape, 1)
    if step == 1:
        r = pltpu.roll(x, 1, axis=2)
        wrapped = pltpu.roll(r, 1, axis=1)
        y = jnp.where(lane == 0, wrapped, r)
        return jnp.where((lane == 0) & (row == 0), 0.0, y)
    r = pltpu.roll(x, LANES - 1, axis=2)
    wrapped = pltpu.roll(r, rows - 1, axis=1)
    y = jnp.where(lane == LANES - 1, wrapped, r)
    return jnp.where((lane == LANES - 1) & (row == rows - 1), 0.0, y)


def _hyena_kernel(v_ref, x1_ref, x2_ref, par_ref, ks0_ref, ks1_ref, f1_ref, tw_ref, cs_ref, g_ref, o_ref,
                  hv_ref, hx1_ref, hx2_ref, ub_ref, a_ref, b_ref, *, cb, gc):
    par = par_ref[...]

    def pv(k):
        return par[:, k:k + 1, :]

    for s, (src, dst) in enumerate(((v_ref, hv_ref), (x1_ref, hx1_ref), (x2_ref, hx2_ref))):
        x = src[...]
        dst[...] = _shift_time(x, 1) * pv(3 * s) + x * pv(3 * s + 1) + _shift_time(x, -1) * pv(3 * s + 2) + pv(9 + s)

    f1 = f1_ref[...]
    gm = g_ref[...]
    twr = tw_ref[0]
    twi = tw_ref[1]
    kb = tw_ref.shape[1]

    def long_conv(ks_ref, finish):
        def s1(p, _):
            u = jnp.concatenate([ub_ref[2 * p], ub_ref[2 * p + 1]], axis=1)
            _stage1_pair(f1, u, tw_ref, a_ref, 2 * p)
            return 0

        lax.fori_loop(0, cb // 2, s1, 0)

        def s2(gi, _):
            c0 = pl.multiple_of(gi * gc, gc)
            rc, rs, ic, is_ = _stage2_group(a_ref, cs_ref, c0, gc)
            xr = rc + is_
            xi = ic - rs
            kr = ks_ref[pl.ds(c0, gc), 0]
            ki = ks_ref[pl.ds(c0, gc), 1]
            b_ref[pl.ds(c0, gc), 0] = (xr * kr - xi * ki).astype(BF16)
            b_ref[pl.ds(c0, gc), 1] = (xr * ki + xi * kr).astype(BF16)
            return 0

        lax.fori_loop(0, cb // gc, s2, 0)

        def s3(gi, _):
            c0 = pl.multiple_of(gi * gc, gc)
            rc, rs, ic, is_ = _stage2_group(b_ref, cs_ref, c0, gc)
            qr = rc - is_
            qi = rs + ic
            a_ref[pl.ds(c0, gc), 0] = (qr * twr + qi * twi).astype(BF16)
            a_ref[pl.ds(c0, gc), 1] = (qi * twr - qr * twi).astype(BF16)
            return 0

        lax.fori_loop(0, cb // gc, s3, 0)

        def s4(p, _):
            q = jnp.concatenate([a_ref[2 * p].reshape(2 * kb, LANES), a_ref[2 * p + 1].reshape(2 * kb, LANES)], axis=1)
            y = jnp.dot(gm, q, preferred_element_type=F32)
            finish(2 * p, y[:, :LANES])
            finish(2 * p + 1, y[:, LANES:])
            return 0

        lax.fori_loop(0, cb // 2, s4, 0)

    ub_ref[...] = hv_ref[...].astype(BF16)

    def finish0(c, y):
        u = hv_ref[c]
        z = hx1_ref[c] * (y + u * par_ref[c, 12:13, :])
        hx1_ref[c] = z
        ub_ref[c] = z.astype(BF16)

    long_conv(ks0_ref, finish0)

    def finish1(c, y):
        z = hx1_ref[c]
        o_ref[c] = hx2_ref[c] * (y + z * par_ref[c, 13:14, :])

    long_conv(ks1_ref, finish1)


def _hyena(hyt, par, ks, tabs, *, Bt, L, d_hy):
    nb, kb = tabs["nb"], tabs["kb"]
    rows = L // LANES
    cb = _pick(d_hy, (32, 16, 8, 4, 2))
    gc = _pick(cb, (8, 4, 2))
    ncb = d_hy // cb
    x3 = hyt.reshape(3 * d_hy, Bt * rows, LANES)
    kern = functools.partial(_hyena_kernel, cb=cb, gc=gc)
    const2 = lambda j, b: (0, 0)
    out = pl.pallas_call(
        kern,
        grid=(ncb, Bt),
        in_specs=[
            pl.BlockSpec((cb, rows, LANES), lambda j, b: (j, b, 0)),
            pl.BlockSpec((cb, rows, LANES), lambda j, b: (ncb + j, b, 0)),
            pl.BlockSpec((cb, rows, LANES), lambda j, b: (2 * ncb + j, b, 0)),
            pl.BlockSpec((cb, 16, LANES), lambda j, b: (j, 0, 0)),
            pl.BlockSpec((cb, 2, kb, LANES), lambda j, b: (j, 0, 0, 0)),
            pl.BlockSpec((cb, 2, kb, LANES), lambda j, b: (ncb + j, 0, 0, 0)),
            pl.BlockSpec((2 * kb, nb // 2), const2),
            pl.BlockSpec((2, kb, LANES), lambda j, b: (0, 0, 0)),
            pl.BlockSpec((LANES, 2 * LANES), const2),
            pl.BlockSpec((nb // 2, 2 * kb), const2),
        ],
        out_specs=pl.BlockSpec((cb, rows, LANES), lambda j, b: (j, b, 0)),
        out_shape=jax.ShapeDtypeStruct((d_hy, Bt * rows, LANES), F32),
        scratch_shapes=[
            pltpu.VMEM((cb, rows, LANES), F32),
            pltpu.VMEM((cb, rows, LANES), F32),
            pltpu.VMEM((cb, rows, LANES), F32),
            pltpu.VMEM((cb, rows, LANES), BF16),
            pltpu.VMEM((cb, 2, kb, LANES), BF16),
            pltpu.VMEM((cb, 2, kb, LANES), BF16),
        ],
        compiler_params=_params(("parallel", "arbitrary")),
        name="hyena",
    )(x3, x3, x3, par, ks, ks, tabs["f1_half"], tabs["tw"], tabs["cs"], tabs["g"])
    return out.reshape(d_hy, Bt * L)


def _mix_norm_kernel(rg_ref, hy_ref, grg_ref, ghy_ref, o_ref, *, d_rg):
    o_ref[:, :d_rg] = _rms(rg_ref[...], grg_ref[...]).astype(BF16)
    hy = hy_ref[...]
    scale = lax.rsqrt(jnp.mean(hy * hy, axis=0, keepdims=True) + EPS)
    o_ref[:, d_rg:] = (hy * scale * ghy_ref[:, 0:1]).T.astype(BF16)


def _mix_norm(rg, hyt, g_rg, g_hy_col):
    T, d_rg = rg.shape
    d_hy = hyt.shape[0]
    tm = _pick(T, (256, 128))
    kern = functools.partial(_mix_norm_kernel, d_rg=d_rg)
    return pl.pallas_call(
        kern,
        grid=(T // tm,),
        in_specs=[
            pl.BlockSpec((tm, d_rg), lambda i: (i, 0)),
            pl.BlockSpec((d_hy, tm), lambda i: (0, i)),
            pl.BlockSpec((1, d_rg), lambda i: (0, 0)),
            pl.BlockSpec((d_hy, LANES), lambda i: (0, 0)),
        ],
        out_specs=pl.BlockSpec((tm, d_rg + d_hy), lambda i: (i, 0)),
        out_shape=jax.ShapeDtypeStruct((T, d_rg + d_hy), BF16),
        compiler_params=_params(("parallel",)),
        name="mix_norm",
    )(rg, hyt, g_rg.reshape(1, d_rg), g_hy_col)


def _post_mix_kernel(x_ref, f_ref, g1_ref, g2_ref, x1_ref, hn_ref):
    x1 = x_ref[...] + _rms(f_ref[...], g1_ref[...])
    x1_ref[...] = x1
    hn_ref[...] = _rms(x1, g2_ref[...]).astype(BF16)


def _post_mix(x, f, g1, g2):
    T, D = x.shape
    tm = _pick(T, (256, 128))
    row = pl.BlockSpec((tm, D), lambda i: (i, 0))
    vec = pl.BlockSpec((1, D), lambda i: (0, 0))
    return pl.pallas_call(
        _post_mix_kernel,
        grid=(T // tm,),
        in_specs=[row, row, vec, vec],
        out_specs=[row, row],
        out_shape=[jax.ShapeDtypeStruct((T, D), F32), jax.ShapeDtypeStruct((T, D), BF16)],
        compiler_params=_params(("parallel",)),
        name="post_mix",
    )(x, f, g1.reshape(1, D), g2.reshape(1, D))


def _post_ffn_kernel(x_ref, f_ref, g_ref, o_ref):
    o_ref[...] = x_ref[...] + _rms(f_ref[...], g_ref[...])


def _post_ffn(x, f, g):
    T, D = x.shape
    tm = _pick(T, (256, 128))
    row = pl.BlockSpec((tm, D), lambda i: (i, 0))
    return pl.pallas_call(
        _post_ffn_kernel,
        grid=(T // tm,),
        in_specs=[row, row, pl.BlockSpec((1, D), lambda i: (0, 0))],
        out_specs=row,
        out_shape=jax.ShapeDtypeStruct((T, D), F32),
        compiler_params=_params(("parallel",)),
        name="post_ffn",
    )(x, f, g.reshape(1, D))


def _ffn_gate_kernel(g_ref, u_ref, prev_ref, next_ref, cw_ref, cb_ref, o_ref, *, tm, tiles_per_seq):
    i = pl.program_id(0)
    g = g_ref[...]
    row = lax.broadcasted_iota(jnp.int32, g.shape, 0)
    first = (i % tiles_per_seq) == 0
    last = (i % tiles_per_seq) == tiles_per_seq - 1
    prev_row = jnp.where(first, 0.0, prev_ref[SUBLANES - 1:SUBLANES, :])
    next_row = jnp.where(last, 0.0, next_ref[0:1, :])
    g_m1 = jnp.where(row == 0, prev_row, pltpu.roll(g, 1, axis=0))
    g_p1 = jnp.where(row == tm - 1, next_row, pltpu.roll(g, tm - 1, axis=0))
    cw = cw_ref[...]
    y = g_m1 * cw[0:1, :] + g * cw[1:2, :] + g_p1 * cw[2:3, :] + cb_ref[...]
    o_ref[...] = (jax.nn.gelu(y) * u_ref[...]).astype(BF16)


def _ffn_gate(gu, cw, cb, *, L, d_ff):
    T = gu.shape[0]
    tm = _pick(L, (256, 128))
    tc = _pick(d_ff, tuple(LANES * m for m in (44, 43, 32, 16, 8, 6, 4, 3, 2, 1)))
    ncol = d_ff // tc
    rb = tm // SUBLANES
    kern = functools.partial(_ffn_gate_kernel, tm=tm, tiles_per_seq=L // tm)
    return pl.pallas_call(
        kern,
        grid=(T // tm, ncol),
        in_specs=[
            pl.BlockSpec((tm, tc), lambda i, j: (i, j)),
            pl.BlockSpec((tm, tc), lambda i, j: (i, ncol + j)),
            pl.BlockSpec((SUBLANES, tc), lambda i, j: (jnp.maximum(i * rb - 1, 0), j)),
            pl.BlockSpec((SUBLANES, tc), lambda i, j: (jnp.minimum((i + 1) * rb, T // SUBLANES - 1), j)),
            pl.BlockSpec((3, tc), lambda i, j: (0, j)),
            pl.BlockSpec((1, tc), lambda i, j: (0, j)),
        ],
        out_specs=pl.BlockSpec((tm, tc), lambda i, j: (i, j)),
        out_shape=jax.ShapeDtypeStruct((T, d_ff), BF16),
        compiler_params=_params(("parallel", "parallel")),
        name="ffn_gate",
    )(gu, gu, gu, gu, cw, cb.reshape(1, d_ff))


def _lane_bcast(v):
    return jnp.broadcast_to(v[..., None], v.shape + (LANES,))


def _hyena_filter_spectrum(L, d_hy, n_order, w1, b1, w2, b2, w3, b3, w4, freq, tabs):
    emb = w1.shape[0]
    bands = (emb - 1) // 2
    pos = jnp.arange(2 * L)
    m = jnp.where(pos < L, pos, 2 * L - pos).astype(F32)
    t = m / (L - 1)
    band = jnp.linspace(1e-4, bands - 1, bands, dtype=F32)
    ang = (2.0 * math.pi / L) * m[None, :] * band[:, None]
    zt = jnp.concatenate([t[None, :], jnp.cos(ang), -jnp.sin(ang)], axis=0)
    kpad = -(-emb // BF16_ROWS) * BF16_ROWS
    zt = jnp.pad(zt, ((0, kpad - emb), (0, 0)))
    w1t = jnp.pad(w1.T, ((0, 0), (0, kpad - emb))).astype(BF16)
    fw = w2.shape[0]
    cols = jnp.zeros((fw, LANES), F32)
    cols = cols.at[:, 0].set(freq).at[:, 1].set(b1).at[:, 2].set(b2).at[:, 3].set(b3)
    max_decay = math.log(HY_TARGET) / HY_FAST_DECAY
    min_decay = math.log(HY_TARGET) / HY_SLOW_DECAY
    delta = _lane_bcast(jnp.linspace(min_decay, max_decay, d_hy, dtype=F32))
    kt = _filter_time(zt, w1t, w2.T.astype(BF16), w3.T.astype(BF16), cols, w4.T.astype(BF16), delta,
                      L=L, d_hy=d_hy, n_order=n_order)
    return _filter_spec(kt, tabs, L=L)


def _encoder_layer(x3, p):
    (pre_mix_norm, w_in, rg_conv_w, rg_conv_b, rg_a_w, rg_a_b, rg_x_w, rg_x_b, rg_lambda, hy_conv_w, hy_conv_b,
     hy_w1, hy_b1, hy_w2, hy_b2, hy_w3, hy_b3, hy_w4, hy_sin_freq, hy_bias, rg_out_norm, hy_out_norm, w_out,
     post_mix_norm, pre_ffn_norm, w_ffn_in, ffn_conv_w, ffn_conv_b, w_ffn_out, post_ffn_norm) = p
    Bt, L, D = x3.shape
    T = Bt * L
    d_rg = rg_conv_w.shape[-1]
    n_order, d_hy = hy_bias.shape
    d_ff = ffn_conv_w.shape[-1]
    nh = rg_a_w.shape[1]
    assert d_rg == nh * LANES and n_order == 2 and L % LANES == 0
    x = x3.reshape(T, D)

    hn, hnt = _norm_in(x, pre_mix_norm)
    w_rg = w_in[:, :2 * d_rg].astype(BF16)
    w_hyt = w_in[:, 2 * d_rg:].T.astype(BF16)
    tm = _pick(T, (1024, 512, 256, 128))
    proj_rg = _mm(hn, w_rg, tm=tm, tn=_pick(2 * d_rg, (1024, 512, 256, 128)), name="proj_rg")
    hyt = _mm(w_hyt, hnt, tm=_pick(3 * d_hy, (1024, 512, 256, 128)), tn=tm, name="proj_hy")

    w_cat = jnp.concatenate([rg_a_w[0], rg_x_w[0], rg_a_w[1], rg_x_w[1]], axis=-1).astype(BF16)
    b_cat = jnp.concatenate([rg_a_b[0], rg_x_b[0], rg_a_b[1], rg_x_b[1]], axis=0)
    b_cat = b_cat.reshape(4, nh, LANES).transpose(1, 0, 2).reshape(nh, 1, 4 * LANES)
    rg_y = _rglru(proj_rg.reshape(Bt, L, 2 * d_rg), rg_conv_w, rg_conv_b.reshape(1, d_rg), w_cat, b_cat,
                  rg_lambda, d_rg=d_rg)

    tabs = _dft_tables(L)
    ks = _hyena_filter_spectrum(L, d_hy, n_order, hy_w1, hy_b1, hy_w2, hy_b2, hy_w3, hy_b3, hy_w4, hy_sin_freq, tabs)
    cw = hy_conv_w.reshape(3, 3, d_hy)
    par = jnp.concatenate([cw.transpose(1, 0, 2).reshape(9, d_hy), hy_conv_b.reshape(3, d_hy), hy_bias,
                           jnp.zeros((2, d_hy), F32)], axis=0)
    par = _lane_bcast(par.T)
    hy_yt = _hyena(hyt, par, ks, tabs, Bt=Bt, L=L, d_hy=d_hy)

    mix = _mix_norm(rg_y.reshape(T, d_rg), hy_yt, rg_out_norm, _lane_bcast(hy_out_norm))
    f = _mm(mix, w_out.astype(BF16), tm=tm, tn=_pick(D, (1024, 512, 256, 128)), name="out_proj")
    x1, hn2 = _post_mix(x, f, post_mix_norm, pre_ffn_norm)

    gu = _mm(hn2, w_ffn_in.astype(BF16), tm=tm, tn=_pick(2 * d_ff, (1024, 512, 256, 128)), name="ffn_in")
    fg = _ffn_gate(gu, ffn_conv_w, ffn_conv_b, L=L, d_ff=d_ff)
    f2 = _mm(fg, w_ffn_out.astype(BF16), tm=_pick(T, (512, 256, 128)), tn=_pick(D, (512, 256, 128)), name="ffn_out")
    y = _post_ffn(x1, f2, post_ffn_norm)
    return y.reshape(Bt, L, D)


def _run_trunk(x3, params):
    depth = params[0].shape[0]
    for l in range(depth):
        x3 = _encoder_layer(x3, [q[l] for q in params])
    return x3


def kernel(x_prompt, x_sample, pre_mix_norm, w_in, rg_conv_w, rg_conv_b, rg_a_w, rg_a_b, rg_x_w, rg_x_b, rg_lambda, hy_conv_w, hy_conv_b, hy_w1, hy_b1, hy_w2, hy_b2, hy_w3, hy_b3, hy_w4, hy_sin_freq, hy_bias, rg_out_norm, hy_out_norm, w_out, post_mix_norm, pre_ffn_norm, w_ffn_in, ffn_conv_w, ffn_conv_b, w_ffn_out, post_ffn_norm):
    params = (pre_mix_norm, w_in, rg_conv_w, rg_conv_b, rg_a_w, rg_a_b, rg_x_w, rg_x_b, rg_lambda, hy_conv_w,
              hy_conv_b, hy_w1, hy_b1, hy_w2, hy_b2, hy_w3, hy_b3, hy_w4, hy_sin_freq, hy_bias, rg_out_norm,
              hy_out_norm, w_out, post_mix_norm, pre_ffn_norm, w_ffn_in, ffn_conv_w, ffn_conv_b, w_ffn_out,
              post_ffn_norm)
    if x_prompt.shape[1:] == x_sample.shape[1:]:
        nb = x_prompt.shape[0]
        y = _run_trunk(jnp.concatenate([x_prompt, x_sample], axis=0), params)
        return (y[:nb], y[nb:])
    return (_run_trunk(x_prompt, params), _run_trunk(x_sample, params))
```

```python
import functools
import math

import numpy as np
import jax
import jax.numpy as jnp
from jax import lax
from jax.experimental import pallas as pl
from jax.experimental.pallas import tpu as pltpu

F32 = jnp.float32
BF16 = jnp.bfloat16

EPS = 1e-6
RG_C = 8.0
RG_CONV_LEFT = 2
HY_FAST_DECAY = 0.3
HY_SLOW_DECAY = 1.5
HY_TARGET = 1e-2

LANES = 128
SUBLANES = 8
BF16_ROWS = 16
VMEM_LIMIT = 56 * 1024 * 1024


def _pick(n, candidates):
    for c in candidates:
        if n % c == 0:
            return c
    raise ValueError(f"no tile in {candidates} divides {n}")


def _params(sem):
    return pltpu.CompilerParams(dimension_semantics=sem, vmem_limit_bytes=VMEM_LIMIT)


def _rms(x, g):
    return x * lax.rsqrt(jnp.mean(x * x, axis=-1, keepdims=True) + EPS) * g


def _pair_specs(block, n_first, inner_last, make_index):
    def first(o, i):
        return make_index(jnp.minimum(o, n_first - 1), jnp.where(o < n_first, i, inner_last))

    def second(o, i):
        return make_index(jnp.maximum(o - n_first, 0), jnp.where(o >= n_first, i, 0))

    return pl.BlockSpec(block, first), pl.BlockSpec(block, second)


def _norm_in_kernel(xa_ref, xb_ref, g_ref, hn_ref, hnt_ref, *, n_first):
    x = jnp.where(pl.program_id(0) < n_first, xa_ref[0], xb_ref[0])
    y = _rms(x, g_ref[...])
    hn_ref[0] = y.astype(BF16)
    hnt_ref[...] = y.T.astype(BF16)


def _norm_in(xs, g, *, L):
    D = xs[0].shape[-1]
    A = L // LANES
    n_first = xs[0].shape[0]
    Bt = sum(x.shape[0] for x in xs)
    views = [x.reshape(x.shape[0], LANES, A * D) for x in xs]
    block = (1, LANES, D)
    if len(views) == 2:
        spec_a, spec_b = _pair_specs(block, n_first, A - 1, lambda b, q: (b, 0, q))
    else:
        spec_a = spec_b = pl.BlockSpec(block, lambda b, q: (b, 0, q))
        views = views * 2
    hn, hnt = pl.pallas_call(
        functools.partial(_norm_in_kernel, n_first=n_first),
        grid=(Bt, A),
        in_specs=[spec_a, spec_b, pl.BlockSpec((1, D), lambda b, q: (0, 0))],
        out_specs=[pl.BlockSpec(block, lambda b, q: (b, 0, q)), pl.BlockSpec((D, LANES), lambda b, q: (0, b * A + q))],
        out_shape=[jax.ShapeDtypeStruct((Bt, LANES, A * D), BF16), jax.ShapeDtypeStruct((D, Bt * L), BF16)],
        compiler_params=_params(("parallel", "parallel")),
        name="norm_in",
    )(views[0], views[1], g.reshape(1, D))
    return hn.reshape(Bt * L, D), hnt


def _mm_kernel(a_ref, b_ref, o_ref):
    o_ref[...] = jnp.dot(a_ref[...], b_ref[...], preferred_element_type=F32).astype(o_ref.dtype)


def _mm(a, b, *, tm, tn, name):
    M, K = a.shape
    _, N = b.shape
    return pl.pallas_call(
        _mm_kernel,
        grid=(N // tn, M // tm),
        in_specs=[pl.BlockSpec((tm, K), lambda j, i: (i, 0)), pl.BlockSpec((K, tn), lambda j, i: (0, j))],
        out_specs=pl.BlockSpec((tm, tn), lambda j, i: (i, j)),
        out_shape=jax.ShapeDtypeStruct((M, N), F32),
        compiler_params=_params(("parallel", "parallel")),
        name=name,
    )(a, b)


def _mm2_kernel(a1_ref, a2_ref, b_ref, o_ref):
    k1 = a1_ref.shape[1]
    acc = jnp.dot(a1_ref[...], b_ref[:k1, :], preferred_element_type=F32)
    o_ref[...] = acc + jnp.dot(a2_ref[...], b_ref[k1:, :], preferred_element_type=F32)


def _mm2(a1, a2, b, *, tm, tn, name):
    M, K1 = a1.shape
    K2 = a2.shape[1]
    N = b.shape[1]
    return pl.pallas_call(
        _mm2_kernel,
        grid=(N // tn, M // tm),
        in_specs=[pl.BlockSpec((tm, K1), lambda j, i: (i, 0)), pl.BlockSpec((tm, K2), lambda j, i: (i, 0)),
                  pl.BlockSpec((K1 + K2, tn), lambda j, i: (0, j))],
        out_specs=pl.BlockSpec((tm, tn), lambda j, i: (i, j)),
        out_shape=jax.ShapeDtypeStruct((M, N), F32),
        compiler_params=_params(("parallel", "parallel")),
        name=name,
    )(a1, a2, b)


def _scan_chunk(a, b, carry, reverse):
    R = a.shape[0]
    nv = R // SUBLANES
    a3 = a.reshape(nv, SUBLANES, LANES)
    b3 = b.reshape(nv, SUBLANES, LANES)
    row = lax.broadcasted_iota(jnp.int32, (nv, SUBLANES, LANES), 1)
    for d in (1, 2, 4):
        shift = SUBLANES - d if reverse else d
        valid = (row < SUBLANES - d) if reverse else (row >= d)
        sa = pltpu.roll(a3, shift, axis=1)
        sb = pltpu.roll(b3, shift, axis=1)
        b3 = b3 + jnp.where(valid, a3, 0.0) * sb
        a3 = a3 * jnp.where(valid, sa, 1.0)
    edge = 0 if reverse else SUBLANES - 1
    a_edge = jnp.broadcast_to(a3[:, edge:edge + 1, :], a3.shape)
    b_edge = jnp.broadcast_to(b3[:, edge:edge + 1, :], b3.shape)
    order = range(nv - 1, -1, -1) if reverse else range(nv)
    carries = [None] * nv
    for j in order:
        carries[j] = carry
        carry = b_edge[j] + a_edge[j] * carry
    h3 = b3 + a3 * jnp.stack(carries)
    return h3.reshape(R, LANES), carry


def _rglru_kernel(x_ref, g_ref, cw_ref, cb_ref, w_ref, bias_ref, lam_ref, o_ref, xp_ref, xc_ref, hf_ref, *, L, R):
    pad = SUBLANES
    zeros = jnp.zeros((pad, LANES), F32)
    xp_ref[0:pad, :] = zeros
    xp_ref[L + pad:L + 2 * pad, :] = zeros
    xp_ref[pad:L + pad, :] = x_ref[0]
    cw = cw_ref[...]
    cb = cb_ref[...]
    lam = lam_ref[...]
    decay = RG_C * jax.nn.softplus(-lam)
    nc = L // R
    hw = 2 * LANES

    def gates(xc, d):
        z = jnp.dot(xc.astype(BF16), w_ref[0, :, d * hw:(d + 1) * hw], preferred_element_type=F32)
        z = z + bias_ref[0, :, d * hw:(d + 1) * hw]
        r = jax.nn.sigmoid(z[:, :LANES])
        i = jax.nn.sigmoid(z[:, LANES:])
        s = r * decay[d:d + 1, :]
        a = jnp.exp(-s)
        b = jnp.sqrt(jnp.tanh(s) * (1.0 + a * a)) * (i * xc)
        return a, b

    def fwd(c, carry):
        base = pl.multiple_of(c * R, R)
        big = xp_ref[pl.ds(base, R + 2 * pad), :]
        xc = cb
        for k in range(cw.shape[0]):
            off = pad + k - RG_CONV_LEFT
            xc = xc + big[off:off + R, :] * cw[k:k + 1, :]
        xc_ref[pl.ds(base, R), :] = xc
        a, b = gates(xc, 0)
        h, carry = _scan_chunk(a, b, carry, False)
        hf_ref[pl.ds(base, R), :] = h
        return carry

    lax.fori_loop(0, nc, fwd, jnp.zeros((SUBLANES, LANES), F32))

    def bwd(s, carry):
        base = pl.multiple_of((nc - 1 - s) * R, R)
        xc = xc_ref[pl.ds(base, R), :]
        a, b = gates(xc, 1)
        h, carry = _scan_chunk(a, b, carry, True)
        gate = jax.nn.gelu(g_ref[0, pl.ds(base, R), :])
        o_ref[0, pl.ds(base, R), :] = (hf_ref[pl.ds(base, R), :] + h) * gate
        return carry

    lax.fori_loop(0, nc, bwd, jnp.zeros((SUBLANES, LANES), F32))


def _rglru(proj, cw, cb, w_cat, b_cat, lam, *, d_rg):
    Bt, L, _ = proj.shape
    nh = d_rg // LANES
    R = _pick(L, (512, 256, 128, 64, 32, 16, 8))
    kern = functools.partial(_rglru_kernel, L=L, R=R)
    return pl.pallas_call(
        kern,
        grid=(Bt, nh),
        in_specs=[
            pl.BlockSpec((1, L, LANES), lambda b, h: (b, 0, h)),
            pl.BlockSpec((1, L, LANES), lambda b, h: (b, 0, nh + h)),
            pl.BlockSpec((cw.shape[0], LANES), lambda b, h: (0, h)),
            pl.BlockSpec((1, LANES), lambda b, h: (0, h)),
            pl.BlockSpec((1, LANES, 4 * LANES), lambda b, h: (h, 0, 0)),
            pl.BlockSpec((1, 1, 4 * LANES), lambda b, h: (h, 0, 0)),
            pl.BlockSpec((2, LANES), lambda b, h: (0, h)),
        ],
        out_specs=pl.BlockSpec((1, L, LANES), lambda b, h: (b, 0, h)),
        out_shape=jax.ShapeDtypeStruct((Bt, L, d_rg), F32),
        scratch_shapes=[
            pltpu.VMEM((L + 2 * SUBLANES, LANES), F32),
            pltpu.VMEM((L, LANES), F32),
            pltpu.VMEM((L, LANES), F32),
        ],
        compiler_params=_params(("parallel", "parallel")),
        name="rglru",
    )(proj, proj, cw, cb, w_cat, b_cat, lam)


def _dft_tables(L):
    n = 2 * L
    A = L // LANES
    B = 2 * LANES
    l = np.arange(B, dtype=np.float64)[:, None]
    kl = np.arange(LANES, dtype=np.float64)[None, :] + 0.5
    phi = 2.0 * np.pi * l * kl / B
    cs_full = np.concatenate([np.cos(phi), -np.sin(phi)], 1)
    r = np.arange(A, dtype=np.float64)[:, None]
    psi = 2.0 * np.pi * r * kl / n
    tw = np.stack([np.cos(psi), -np.sin(psi)])
    rk = 2.0 * np.pi * np.outer(np.arange(A), np.arange(A)) / A
    fr, fi = np.cos(rk), -np.sin(rk)
    ff = np.block([[fr, -fi], [fi, fr]])
    fb = np.block([[fr, fi], [-fi, fr]])
    theta = 2.0 * np.pi * kl.T * np.arange(LANES, dtype=np.float64)[None, :] / B
    ci = np.concatenate([np.cos(theta), -np.sin(theta)], 0) * (2.0 / n)
    return dict(
        A=A,
        cs_full=jnp.asarray(cs_full, BF16), cs_half=jnp.asarray(cs_full[:LANES], BF16),
        tw=jnp.asarray(tw, F32), ff=jnp.asarray(ff, BF16), fb=jnp.asarray(fb, BF16), ci=jnp.asarray(ci, BF16),
    )


def _lane_stage(u2, cs_ref, tw_ref, dst_ref, c0, gc):
    A = tw_ref.shape[1]
    y = jnp.dot(u2, cs_ref[...], preferred_element_type=F32)
    yr = y[:, :LANES].reshape(gc, A, LANES)
    yi = y[:, LANES:].reshape(gc, A, LANES)
    twr = tw_ref[0]
    twi = tw_ref[1]
    dst_ref[pl.ds(c0, gc), 0] = (yr * twr - yi * twi).astype(BF16)
    dst_ref[pl.ds(c0, gc), 1] = (yr * twi + yi * twr).astype(BF16)


def _pair_rows(src_ref, p):
    a2 = src_ref.shape[1] * src_ref.shape[2]
    return jnp.concatenate([src_ref[2 * p].reshape(a2, LANES), src_ref[2 * p + 1].reshape(a2, LANES)], axis=1)


def _filter_time_kernel(zt_ref, aux_ref, w1_ref, w2_ref, w3_ref, col_ref, w4f_ref, w4b_ref, delta_ref, o_ref, h3_ref):
    @pl.when(pl.program_id(1) == 0)
    def _():
        fr = col_ref[:, 0:1]
        h = jnp.sin(fr * (jnp.dot(w1_ref[...], zt_ref[...].astype(BF16), preferred_element_type=F32) + col_ref[:, 1:2]))
        h = jnp.sin(fr * (jnp.dot(w2_ref[...], h.astype(BF16), preferred_element_type=F32) + col_ref[:, 2:3]))
        h = jnp.sin(fr * (jnp.dot(w3_ref[...], h.astype(BF16), preferred_element_type=F32) + col_ref[:, 3:4]))
        h3_ref[...] = h.astype(BF16)

    sign = aux_ref[0:1, :]
    t = aux_ref[1:2, :]
    h3 = h3_ref[...]
    kf = jnp.dot(w4f_ref[...], h3, preferred_element_type=F32)
    kb = jnp.dot(w4b_ref[...], h3, preferred_element_type=F32)
    k = jnp.where(sign > 0.0, kf, kb)
    o_ref[...] = k * jnp.exp(-t * jnp.abs(delta_ref[:, 0:1])) * sign


def _filter_time(zt, aux, w1t, w2t, w3t, cols, w4t, delta, *, L, d_hy, n_order):
    fw = w2t.shape[0]
    lt = _pick(2 * L, (1024, 512, 256))
    nt = 2 * L // lt
    tr = _pick(d_hy, (512, 256, 128))
    nr = d_hy // tr
    const = lambda j, r: (0, 0)
    return pl.pallas_call(
        _filter_time_kernel,
        grid=(nt, n_order * nr),
        in_specs=[
            pl.BlockSpec((zt.shape[0], lt), lambda j, r: (0, j)),
            pl.BlockSpec((SUBLANES, lt), lambda j, r: (0, j)),
            pl.BlockSpec(w1t.shape, const),
            pl.BlockSpec(w2t.shape, const),
            pl.BlockSpec(w3t.shape, const),
            pl.BlockSpec(cols.shape, const),
            pl.BlockSpec((tr, fw), lambda j, r: ((r // nr) * 2 * nr + r % nr, 0)),
            pl.BlockSpec((tr, fw), lambda j, r: (((r // nr) * 2 + 1) * nr + r % nr, 0)),
            pl.BlockSpec((tr, LANES), lambda j, r: (r % nr, 0)),
        ],
        out_specs=pl.BlockSpec((tr, lt), lambda j, r: (r, j)),
        out_shape=jax.ShapeDtypeStruct((n_order * d_hy, 2 * L), F32),
        scratch_shapes=[pltpu.VMEM((fw, lt), BF16)],
        compiler_params=_params(("arbitrary", "arbitrary")),
        name="hy_filter_time",
    )(zt, aux, w1t, w2t, w3t, cols, w4t, w4t, delta)


def _filter_spec_kernel(k_ref, cs_ref, tw_ref, ff_ref, o_ref, a_ref, *, cb, gc):
    A = tw_ref.shape[1]

    def lane(gi, _):
        c0 = pl.multiple_of(gi * gc, gc)
        u2 = k_ref[pl.ds(c0, gc)].reshape(gc * A, 2 * LANES).astype(BF16)
        _lane_stage(u2, cs_ref, tw_ref, a_ref, c0, gc)
        return 0

    lax.fori_loop(0, cb // gc, lane, 0)

    def rows(p, _):
        x = jnp.dot(ff_ref[...], _pair_rows(a_ref, p), preferred_element_type=F32)
        for q in range(2):
            o_ref[2 * p + q, 0] = x[:A, q * LANES:(q + 1) * LANES]
            o_ref[2 * p + q, 1] = x[A:, q * LANES:(q + 1) * LANES]
        return 0

    lax.fori_loop(0, cb // 2, rows, 0)


def _filter_spec(kt, tabs):
    C = kt.shape[0]
    A = tabs["A"]
    cb = _pick(C, (32, 16, 8, 4, 2))
    gc = _pick(cb, (8, 4, 2))
    kern = functools.partial(_filter_spec_kernel, cb=cb, gc=gc)
    return pl.pallas_call(
        kern,
        grid=(C // cb,),
        in_specs=[
            pl.BlockSpec((cb, A, 2 * LANES), lambda i: (i, 0, 0)),
            pl.BlockSpec((2 * LANES, 2 * LANES), lambda i: (0, 0)),
            pl.BlockSpec((2, A, LANES), lambda i: (0, 0, 0)),
            pl.BlockSpec((2 * A, 2 * A), lambda i: (0, 0)),
        ],
        out_specs=pl.BlockSpec((cb, 2, A, LANES), lambda i: (i, 0, 0, 0)),
        out_shape=jax.ShapeDtypeStruct((C, 2, A, LANES), F32),
        scratch_shapes=[pltpu.VMEM((cb, 2, A, LANES), BF16)],
        compiler_params=_params(("parallel",)),
        name="hy_filter_spec",
    )(kt.reshape(C, A, 2 * LANES), tabs["cs_full"], tabs["tw"], tabs["ff"])


def _shift_time(x, step):
    A = x.shape[1]
    row = lax.broadcasted_iota(jnp.int32, x.shape, 1)
    lane = lax.broadcasted_iota(jnp.int32, (x.shape[0], 1, LANES), 2)
    if step == 1:
        rolled = pltpu.roll(x, 1, axis=1)
        edge = jnp.where(lane == 0, 0.0, pltpu.roll(x[:, A - 1:A, :], 1, axis=2))
        return jnp.where(row == 0, edge, rolled)
    rolled = pltpu.roll(x, A - 1, axis=1)
    edge = jnp.where(lane == LANES - 1, 0.0, pltpu.roll(x[:, 0:1, :], LANES - 1, axis=2))
    return jnp.where(row == A - 1, edge, rolled)


def _hyena_kernel(v_ref, x1_ref, x2_ref, par_ref, ks0_ref, ks1_ref, cs_ref, tw_ref, ff_ref, fb_ref, ci_ref, o_ref,
                  hv_ref, hx1_ref, hx2_ref, ub_ref, a_ref, b_ref, *, cb, gc):
    A = tw_ref.shape[1]
    ngroups = cb // gc

    def short_conv(gi, _):
        c0 = pl.multiple_of(gi * gc, gc)
        par = par_ref[pl.ds(c0, gc)]
        for s, (src, dst) in enumerate(((v_ref, hv_ref), (x1_ref, hx1_ref), (x2_ref, hx2_ref))):
            x = src[pl.ds(c0, gc)]
            y = (_shift_time(x, 1) * par[:, 3 * s:3 * s + 1, :] + x * par[:, 3 * s + 1:3 * s + 2, :]
                 + _shift_time(x, -1) * par[:, 3 * s + 2:3 * s + 3, :] + par[:, 9 + s:10 + s, :])
            dst[pl.ds(c0, gc)] = y
            if s == 0:
                ub_ref[pl.ds(c0, gc)] = y.astype(BF16)
        return 0

    lax.fori_loop(0, ngroups, short_conv, 0)

    twr = tw_ref[0]
    twi = tw_ref[1]

    def long_conv(ks_ref, finish):
        def lane(gi, _):
            c0 = pl.multiple_of(gi * gc, gc)
            _lane_stage(ub_ref[pl.ds(c0, gc)].reshape(gc * A, LANES), cs_ref, tw_ref, a_ref, c0, gc)
            return 0

        lax.fori_loop(0, ngroups, lane, 0)

        def one_pair(p):
            x = jnp.dot(ff_ref[...], _pair_rows(a_ref, p), preferred_element_type=F32)
            prod = []
            for q in range(2):
                xr = x[:A, q * LANES:(q + 1) * LANES]
                xi = x[A:, q * LANES:(q + 1) * LANES]
                kr = ks_ref[2 * p + q, 0]
                ki = ks_ref[2 * p + q, 1]
                prod.append(jnp.concatenate([xr * kr - xi * ki, xr * ki + xi * kr], axis=0))
            pcat = jnp.concatenate(prod, axis=1).astype(BF16)
            qv = jnp.dot(fb_ref[...], pcat, preferred_element_type=F32)
            for q in range(2):
                qr = qv[:A, q * LANES:(q + 1) * LANES]
                qi = qv[A:, q * LANES:(q + 1) * LANES]
                b_ref[2 * p + q, 0] = (qr * twr + qi * twi).astype(BF16)
                b_ref[2 * p + q, 1] = (qi * twr - qr * twi).astype(BF16)

        def rows(i, _):
            one_pair(2 * i)
            one_pair(2 * i + 1)
            return 0

        lax.fori_loop(0, cb // 4, rows, 0)

        def back(gi, _):
            c0 = pl.multiple_of(gi * gc, gc)
            qcat = jnp.concatenate([b_ref[pl.ds(c0, gc), 0].reshape(gc * A, LANES),
                                    b_ref[pl.ds(c0, gc), 1].reshape(gc * A, LANES)], axis=1)
            y = jnp.dot(qcat, ci_ref[...], preferred_element_type=F32).reshape(gc, A, LANES)
            finish(c0, y)
            return 0

        lax.fori_loop(0, ngroups, back, 0)

    def finish0(c0, y):
        g = pl.ds(c0, gc)
        z = hx1_ref[g] * (y + hv_ref[g] * par_ref[g, 12:13, :])
        hx1_ref[g] = z
        ub_ref[g] = z.astype(BF16)

    long_conv(ks0_ref, finish0)

    def finish1(c0, y):
        g = pl.ds(c0, gc)
        o_ref[g] = hx2_ref[g] * (y + hx1_ref[g] * par_ref[g, 13:14, :])

    long_conv(ks1_ref, finish1)


def _hyena(hyt, par, ks, tabs, *, Bt, L, d_hy):
    A = tabs["A"]
    cb = _pick(d_hy, (32, 16, 8, 4))
    gc = _pick(cb, (8, 4))
    ncb = d_hy // cb
    x3 = hyt.reshape(3 * d_hy, Bt * A, LANES)
    kern = functools.partial(_hyena_kernel, cb=cb, gc=gc)
    const2 = lambda j, b: (0, 0)
    tile = (cb, A, LANES)
    return pl.pallas_call(
        kern,
        grid=(ncb, Bt),
        in_specs=[
            pl.BlockSpec(tile, lambda j, b: (j, b, 0)),
            pl.BlockSpec(tile, lambda j, b: (ncb + j, b, 0)),
            pl.BlockSpec(tile, lambda j, b: (2 * ncb + j, b, 0)),
            pl.BlockSpec((cb, 16, LANES), lambda j, b: (j, 0, 0)),
            pl.BlockSpec((cb, 2, A, LANES), lambda j, b: (j, 0, 0, 0)),
            pl.BlockSpec((cb, 2, A, LANES), lambda j, b: (ncb + j, 0, 0, 0)),
            pl.BlockSpec((LANES, 2 * LANES), const2),
            pl.BlockSpec((2, A, LANES), lambda j, b: (0, 0, 0)),
            pl.BlockSpec((2 * A, 2 * A), const2),
            pl.BlockSpec((2 * A, 2 * A), const2),
            pl.BlockSpec((2 * LANES, LANES), const2),
        ],
        out_specs=pl.BlockSpec(tile, lambda j, b: (j, b, 0)),
        out_shape=jax.ShapeDtypeStruct((d_hy, Bt * A, LANES), F32),
        scratch_shapes=[
            pltpu.VMEM(tile, F32),
            pltpu.VMEM(tile, F32),
            pltpu.VMEM(tile, F32),
            pltpu.VMEM(tile, BF16),
            pltpu.VMEM((cb, 2, A, LANES), BF16),
            pltpu.VMEM((cb, 2, A, LANES), BF16),
        ],
        compiler_params=_params(("parallel", "arbitrary")),
        name="hyena",
    )(x3, x3, x3, par, ks, ks, tabs["cs_half"], tabs["tw"], tabs["ff"], tabs["fb"], tabs["ci"])


def _rg_norm_kernel(x_ref, g_ref, o_ref):
    o_ref[...] = _rms(x_ref[...], g_ref[...]).astype(BF16)


def _rg_norm(x, g):
    T, C = x.shape
    tm = _pick(T, (512, 256, 128))
    return pl.pallas_call(
        _rg_norm_kernel,
        grid=(T // tm,),
        in_specs=[pl.BlockSpec((tm, C), lambda i: (i, 0)), pl.BlockSpec((1, C), lambda i: (0, 0))],
        out_specs=pl.BlockSpec((tm, C), lambda i: (i, 0)),
        out_shape=jax.ShapeDtypeStruct((T, C), BF16),
        compiler_params=_params(("parallel",)),
        name="rg_norm",
    )(x, g.reshape(1, C))


def _hy_norm_kernel(hy_ref, g_ref, o_ref, *, d_hy):
    gain = g_ref[:, 0:1]
    for r in range(SUBLANES):
        h = hy_ref[:, r, :]
        scale = lax.rsqrt(jnp.mean(h * h, axis=0, keepdims=True) + EPS)
        o_ref[0, :, r * d_hy:(r + 1) * d_hy] = (h * scale * gain).T.astype(BF16)


def _hy_norm(hy3, g_col, *, Bt, L):
    d_hy = hy3.shape[0]
    A = L // LANES
    ng = A // SUBLANES
    out = pl.pallas_call(
        functools.partial(_hy_norm_kernel, d_hy=d_hy),
        grid=(Bt, ng),
        in_specs=[pl.BlockSpec((d_hy, SUBLANES, LANES), lambda b, g: (0, b * ng + g, 0)),
                  pl.BlockSpec((d_hy, LANES), lambda b, g: (0, 0))],
        out_specs=pl.BlockSpec((1, LANES, SUBLANES * d_hy), lambda b, g: (b, 0, g)),
        out_shape=jax.ShapeDtypeStruct((Bt, LANES, A * d_hy), BF16),
        compiler_params=_params(("parallel", "parallel")),
        name="hy_norm",
    )(hy3, g_col)
    return out.reshape(Bt * L, d_hy)


def _post_mix_kernel(xa_ref, xb_ref, f_ref, g1_ref, g2_ref, x1_ref, hn_ref, *, n_first):
    x = jnp.where(pl.program_id(0) < n_first, xa_ref[0], xb_ref[0])
    x1 = x + _rms(f_ref[0], g1_ref[...])
    x1_ref[0] = x1
    hn_ref[0] = _rms(x1, g2_ref[...]).astype(BF16)


def _post_mix(xs, f, g1, g2, *, L):
    D = xs[0].shape[-1]
    n_first = xs[0].shape[0]
    Bt = sum(x.shape[0] for x in xs)
    tm = _pick(L, (256, 128))
    nt = L // tm
    block = (1, tm, D)
    index = lambda b, i: (b, i, 0)
    if len(xs) == 2:
        spec_a, spec_b = _pair_specs(block, n_first, nt - 1, index)
    else:
        spec_a = spec_b = pl.BlockSpec(block, index)
        xs = list(xs) * 2
    row = pl.BlockSpec(block, index)
    vec = pl.BlockSpec((1, D), lambda b, i: (0, 0))
    x1, hn = pl.pallas_call(
        functools.partial(_post_mix_kernel, n_first=n_first),
        grid=(Bt, nt),
        in_specs=[spec_a, spec_b, row, vec, vec],
        out_specs=[row, row],
        out_shape=[jax.ShapeDtypeStruct((Bt, L, D), F32), jax.ShapeDtypeStruct((Bt, L, D), BF16)],
        compiler_params=_params(("parallel", "parallel")),
        name="post_mix",
    )(xs[0], xs[1], f.reshape(Bt, L, D), g1.reshape(1, D), g2.reshape(1, D))
    return x1.reshape(Bt * L, D), hn.reshape(Bt * L, D)


def _post_ffn_kernel(x_ref, f_ref, g_ref, ya_ref, yb_ref, *, n_first):
    y = x_ref[0] + _rms(f_ref[0], g_ref[...])

    @pl.when(pl.program_id(0) < n_first)
    def _():
        ya_ref[0] = y

    @pl.when(pl.program_id(0) >= n_first)
    def _():
        yb_ref[0] = y


def _post_ffn(x, f, g, *, L, splits):
    T, D = x.shape
    Bt = T // L
    tm = _pick(L, (256, 128))
    nt = L // tm
    block = (1, tm, D)
    index = lambda b, i: (b, i, 0)
    row = pl.BlockSpec(block, index)
    if len(splits) == 1:
        out = pl.pallas_call(
            _post_ffn_single_kernel,
            grid=(Bt, nt),
            in_specs=[row, row, pl.BlockSpec((1, D), lambda b, i: (0, 0))],
            out_specs=row,
            out_shape=jax.ShapeDtypeStruct((Bt, L, D), F32),
            compiler_params=_params(("parallel", "parallel")),
            name="post_ffn",
        )(x.reshape(Bt, L, D), f.reshape(Bt, L, D), g.reshape(1, D))
        return (out,)
    n_first = splits[0]
    spec_a, spec_b = _pair_specs(block, n_first, nt - 1, index)
    return tuple(pl.pallas_call(
        functools.partial(_post_ffn_kernel, n_first=n_first),
        grid=(Bt, nt),
        in_specs=[row, row, pl.BlockSpec((1, D), lambda b, i: (0, 0))],
        out_specs=[spec_a, spec_b],
        out_shape=[jax.ShapeDtypeStruct((n, L, D), F32) for n in splits],
        compiler_params=_params(("arbitrary", "arbitrary")),
        name="post_ffn",
    )(x.reshape(Bt, L, D), f.reshape(Bt, L, D), g.reshape(1, D)))


def _post_ffn_single_kernel(x_ref, f_ref, g_ref, y_ref):
    y_ref[0] = x_ref[0] + _rms(f_ref[0], g_ref[...])


def _ffn_in_kernel(a_ref, prev_ref, next_ref, wg_ref, wu_ref, cw_ref, cb_ref, o_ref, *, tm, tiles_per_seq):
    i = pl.program_id(1)
    a = a_ref[...]
    wg = wg_ref[...]
    g = jnp.dot(a, wg, preferred_element_type=F32)
    u = jnp.dot(a, wu_ref[...], preferred_element_type=F32)
    g_prev = jnp.dot(prev_ref[...], wg, preferred_element_type=F32)[SUBLANES - 1:SUBLANES, :]
    g_next = jnp.dot(next_ref[...], wg, preferred_element_type=F32)[0:1, :]
    first = (i % tiles_per_seq) == 0
    last = (i % tiles_per_seq) == tiles_per_seq - 1
    g_prev = jnp.where(first, 0.0, g_prev)
    g_next = jnp.where(last, 0.0, g_next)
    row = lax.broadcasted_iota(jnp.int32, g.shape, 0)
    g_m1 = jnp.where(row == 0, g_prev, pltpu.roll(g, 1, axis=0))
    g_p1 = jnp.where(row == tm - 1, g_next, pltpu.roll(g, tm - 1, axis=0))
    cw = cw_ref[...]
    y = g_m1 * cw[0:1, :] + g * cw[1:2, :] + g_p1 * cw[2:3, :] + cb_ref[...]
    o_ref[...] = (jax.nn.gelu(y) * u).astype(BF16)


def _ffn_in(a, w, cw, cb, *, L, d_ff):
    T, D = a.shape
    tm = _pick(L, (1024, 512, 256, 128))
    tn = _pick(d_ff, (256, 128))
    ncol = d_ff // tn
    rb = tm // SUBLANES
    kern = functools.partial(_ffn_in_kernel, tm=tm, tiles_per_seq=L // tm)
    return pl.pallas_call(
        kern,
        grid=(ncol, T // tm),
        in_specs=[
            pl.BlockSpec((tm, D), lambda j, i: (i, 0)),
            pl.BlockSpec((SUBLANES, D), lambda j, i: (jnp.maximum(i * rb - 1, 0), 0)),
            pl.BlockSpec((SUBLANES, D), lambda j, i: (jnp.minimum((i + 1) * rb, T // SUBLANES - 1), 0)),
            pl.BlockSpec((D, tn), lambda j, i: (0, j)),
            pl.BlockSpec((D, tn), lambda j, i: (0, ncol + j)),
            pl.BlockSpec((3, tn), lambda j, i: (0, j)),
            pl.BlockSpec((1, tn), lambda j, i: (0, j)),
        ],
        out_specs=pl.BlockSpec((tm, tn), lambda j, i: (i, j)),
        out_shape=jax.ShapeDtypeStruct((T, d_ff), BF16),
        compiler_params=_params(("parallel", "parallel")),
        name="ffn_in",
    )(a, a, a, w, w, cw, cb.reshape(1, d_ff))


def _lane_bcast(v):
    return jnp.broadcast_to(v[..., None], v.shape + (LANES,))


def _hyena_filter_spectrum(L, d_hy, n_order, w1, b1, w2, b2, w3, b3, w4, freq, tabs):
    emb = w1.shape[0]
    bands = (emb - 1) // 2
    A = tabs["A"]
    slot = jnp.arange(2 * L)
    n = A * (slot % (2 * LANES)) + slot // (2 * LANES)
    m = jnp.where(n < L, n, 2 * L - n).astype(F32)
    sign = jnp.where(n < L, 1.0, jnp.where(n == L, 0.0, -1.0)).astype(F32)
    t = m / (L - 1)
    band = jnp.linspace(1e-4, bands - 1, bands, dtype=F32)
    ang = (2.0 * math.pi / L) * m[None, :] * band[:, None]
    zt = jnp.concatenate([t[None, :], jnp.cos(ang), -jnp.sin(ang)], axis=0)
    kpad = -(-emb // BF16_ROWS) * BF16_ROWS
    zt = jnp.pad(zt, ((0, kpad - emb), (0, 0)))
    aux = jnp.zeros((SUBLANES, 2 * L), F32).at[0].set(sign).at[1].set(t)
    w1t = jnp.pad(w1.T, ((0, 0), (0, kpad - emb))).astype(BF16)
    fw = w2.shape[0]
    cols = jnp.zeros((fw, LANES), F32)
    cols = cols.at[:, 0].set(freq).at[:, 1].set(b1).at[:, 2].set(b2).at[:, 3].set(b3)
    max_decay = math.log(HY_TARGET) / HY_FAST_DECAY
    min_decay = math.log(HY_TARGET) / HY_SLOW_DECAY
    delta = _lane_bcast(jnp.linspace(min_decay, max_decay, d_hy, dtype=F32))
    kt = _filter_time(zt, aux, w1t, w2.T.astype(BF16), w3.T.astype(BF16), cols, w4.T.astype(BF16), delta,
                      L=L, d_hy=d_hy, n_order=n_order)
    return _filter_spec(kt, tabs)


def _encoder_layer(xs, p, splits):
    (pre_mix_norm, w_in, rg_conv_w, rg_conv_b, rg_a_w, rg_a_b, rg_x_w, rg_x_b, rg_lambda, hy_conv_w, hy_conv_b,
     hy_w1, hy_b1, hy_w2, hy_b2, hy_w3, hy_b3, hy_w4, hy_sin_freq, hy_bias, rg_out_norm, hy_out_norm, w_out,
     post_mix_norm, pre_ffn_norm, w_ffn_in, ffn_conv_w, ffn_conv_b, w_ffn_out, post_ffn_norm) = p
    _, L, D = xs[0].shape
    Bt = sum(x.shape[0] for x in xs)
    T = Bt * L
    d_rg = rg_conv_w.shape[-1]
    n_order, d_hy = hy_bias.shape
    d_ff = ffn_conv_w.shape[-1]
    nh = rg_a_w.shape[1]
    assert d_rg == nh * LANES and n_order == 2 and L % (LANES * BF16_ROWS) == 0

    hn, hnt = _norm_in(xs, pre_mix_norm, L=L)
    w_rg = w_in[:, :2 * d_rg].astype(BF16)
    w_hyt = w_in[:, 2 * d_rg:].T.astype(BF16)
    tm = _pick(T, (1024, 512, 256, 128))
    proj_rg = _mm(hn, w_rg, tm=tm, tn=_pick(2 * d_rg, (1024, 512, 256, 128)), name="proj_rg")
    hyt = _mm(w_hyt, hnt, tm=_pick(3 * d_hy, (1024, 512, 256, 128)), tn=tm, name="proj_hy")

    w_cat = jnp.concatenate([rg_a_w[0], rg_x_w[0], rg_a_w[1], rg_x_w[1]], axis=-1).astype(BF16)
    b_cat = jnp.concatenate([rg_a_b[0], rg_x_b[0], rg_a_b[1], rg_x_b[1]], axis=0)
    b_cat = b_cat.reshape(4, nh, LANES).transpose(1, 0, 2).reshape(nh, 1, 4 * LANES)
    rg_y = _rglru(proj_rg.reshape(Bt, L, 2 * d_rg), rg_conv_w, rg_conv_b.reshape(1, d_rg), w_cat, b_cat,
                  rg_lambda, d_rg=d_rg)

    tabs = _dft_tables(L)
    ks = _hyena_filter_spectrum(L, d_hy, n_order, hy_w1, hy_b1, hy_w2, hy_b2, hy_w3, hy_b3, hy_w4, hy_sin_freq, tabs)
    cw = hy_conv_w.reshape(3, 3, d_hy)
    par = jnp.concatenate([cw.transpose(1, 0, 2).reshape(9, d_hy), hy_conv_b.reshape(3, d_hy), hy_bias,
                           jnp.zeros((2, d_hy), F32)], axis=0)
    par = _lane_bcast(par.T)
    hy_y = _hyena(hyt, par, ks, tabs, Bt=Bt, L=L, d_hy=d_hy)

    mix_rg = _rg_norm(rg_y.reshape(T, d_rg), rg_out_norm)
    mix_hy = _hy_norm(hy_y, _lane_bcast(hy_out_norm), Bt=Bt, L=L)
    f = _mm2(mix_rg, mix_hy, w_out.astype(BF16), tm=tm, tn=_pick(D, (1024, 512, 256, 128)), name="out_proj")
    x1, hn2 = _post_mix(xs, f, post_mix_norm, pre_ffn_norm, L=L)

    fg = _ffn_in(hn2, w_ffn_in.astype(BF16), ffn_conv_w, ffn_conv_b, L=L, d_ff=d_ff)
    f2 = _mm(fg, w_ffn_out.astype(BF16), tm=_pick(T, (512, 256, 128)), tn=_pick(D, (512, 256, 128)), name="ffn_out")
    return _post_ffn(x1, f2, post_ffn_norm, L=L, splits=splits)


def _run_trunk(xs, params):
    splits = tuple(x.shape[0] for x in xs)
    depth = params[0].shape[0]
    for l in range(depth):
        last = l == depth - 1
        xs = _encoder_layer(xs, [q[l] for q in params], splits if last else (sum(splits),))
    return xs


def kernel(x_prompt, x_sample, pre_mix_norm, w_in, rg_conv_w, rg_conv_b, rg_a_w, rg_a_b, rg_x_w, rg_x_b, rg_lambda, hy_conv_w, hy_conv_b, hy_w1, hy_b1, hy_w2, hy_b2, hy_w3, hy_b3, hy_w4, hy_sin_freq, hy_bias, rg_out_norm, hy_out_norm, w_out, post_mix_norm, pre_ffn_norm, w_ffn_in, ffn_conv_w, ffn_conv_b, w_ffn_out, post_ffn_norm):
    params = (pre_mix_norm, w_in, rg_conv_w, rg_conv_b, rg_a_w, rg_a_b, rg_x_w, rg_x_b, rg_lambda, hy_conv_w,
              hy_conv_b, hy_w1, hy_b1, hy_w2, hy_b2, hy_w3, hy_b3, hy_w4, hy_sin_freq, hy_bias, rg_out_norm,
              hy_out_norm, w_out, post_mix_norm, pre_ffn_norm, w_ffn_in, ffn_conv_w, ffn_conv_b, w_ffn_out,
              post_ffn_norm)
    if x_prompt.shape[1:] == x_sample.shape[1:]:
        return _run_trunk((x_prompt, x_sample), params)
    return (_run_trunk((x_prompt,), params)[0], _run_trunk((x_sample,), params)[0])
```

```python
import functools
import math

import numpy as np
import jax
import jax.numpy as jnp
from jax import lax
from jax.experimental import pallas as pl
from jax.experimental.pallas import tpu as pltpu

F32 = jnp.float32
BF16 = jnp.bfloat16

EPS = 1e-6
RG_C = 8.0
RG_CONV_LEFT = 2
HY_FAST_DECAY = 0.3
HY_SLOW_DECAY = 1.5
HY_TARGET = 1e-2

LANES = 128
SUBLANES = 8
BF16_ROWS = 16
VMEM_LIMIT = 56 * 1024 * 1024


def _pick(n, candidates):
    for c in candidates:
        if n % c == 0:
            return c
    raise ValueError(f"no tile in {candidates} divides {n}")


def _params(sem):
    return pltpu.CompilerParams(dimension_semantics=sem, vmem_limit_bytes=VMEM_LIMIT)


def _rms(x, g):
    return x * lax.rsqrt(jnp.mean(x * x, axis=-1, keepdims=True) + EPS) * g


def _pair_specs(block, n_first, inner_last, make_index):
    def first(o, i):
        return make_index(jnp.minimum(o, n_first - 1), jnp.where(o < n_first, i, inner_last))

    def second(o, i):
        return make_index(jnp.maximum(o - n_first, 0), jnp.where(o >= n_first, i, 0))

    return pl.BlockSpec(block, first), pl.BlockSpec(block, second)


def _norm_in_kernel(xa_ref, xb_ref, g_ref, hn_ref, hnt_ref, *, n_first):
    x = jnp.where(pl.program_id(0) < n_first, xa_ref[0], xb_ref[0])
    y = _rms(x, g_ref[...])
    hn_ref[0] = y.astype(BF16)
    hnt_ref[...] = y.T.astype(BF16)


def _norm_in(xs, g, *, L):
    D = xs[0].shape[-1]
    n_first = xs[0].shape[0]
    Bt = sum(x.shape[0] for x in xs)
    tm = _pick(L, (256, 128))
    nt = L // tm
    block = (1, tm, D)
    index = lambda b, i: (b, i, 0)
    if len(xs) == 2:
        spec_a, spec_b = _pair_specs(block, n_first, nt - 1, index)
    else:
        spec_a = spec_b = pl.BlockSpec(block, index)
        xs = list(xs) * 2
    hn, hnt = pl.pallas_call(
        functools.partial(_norm_in_kernel, n_first=n_first),
        grid=(Bt, nt),
        in_specs=[spec_a, spec_b, pl.BlockSpec((1, D), lambda b, i: (0, 0))],
        out_specs=[pl.BlockSpec(block, index), pl.BlockSpec((D, tm), lambda b, i: (0, b * nt + i))],
        out_shape=[jax.ShapeDtypeStruct((Bt, L, D), BF16), jax.ShapeDtypeStruct((D, Bt * L), BF16)],
        compiler_params=_params(("parallel", "parallel")),
        name="norm_in",
    )(xs[0], xs[1], g.reshape(1, D))
    return hn.reshape(Bt * L, D), hnt


def _mm_kernel(a_ref, b_ref, o_ref):
    o_ref[...] = jnp.dot(a_ref[...], b_ref[...], preferred_element_type=F32).astype(o_ref.dtype)


def _mm(a, b, *, tm, tn, name):
    M, K = a.shape
    _, N = b.shape
    return pl.pallas_call(
        _mm_kernel,
        grid=(N // tn, M // tm),
        in_specs=[pl.BlockSpec((tm, K), lambda j, i: (i, 0)), pl.BlockSpec((K, tn), lambda j, i: (0, j))],
        out_specs=pl.BlockSpec((tm, tn), lambda j, i: (i, j)),
        out_shape=jax.ShapeDtypeStruct((M, N), F32),
        compiler_params=_params(("parallel", "parallel")),
        name=name,
    )(a, b)


def _scan_chunk(a, b, carry, reverse):
    R = a.shape[0]
    nv = R // SUBLANES
    a3 = a.reshape(nv, SUBLANES, LANES)
    b3 = b.reshape(nv, SUBLANES, LANES)
    row = lax.broadcasted_iota(jnp.int32, (nv, SUBLANES, LANES), 1)
    for d in (1, 2, 4):
        shift = SUBLANES - d if reverse else d
        valid = (row < SUBLANES - d) if reverse else (row >= d)
        sa = pltpu.roll(a3, shift, axis=1)
        sb = pltpu.roll(b3, shift, axis=1)
        b3 = b3 + jnp.where(valid, a3, 0.0) * sb
        a3 = a3 * jnp.where(valid, sa, 1.0)
    edge = 0 if reverse else SUBLANES - 1
    a_edge = jnp.broadcast_to(a3[:, edge:edge + 1, :], a3.shape)
    b_edge = jnp.broadcast_to(b3[:, edge:edge + 1, :], b3.shape)
    order = range(nv - 1, -1, -1) if reverse else range(nv)
    carries = [None] * nv
    for j in order:
        carries[j] = carry
        carry = b_edge[j] + a_edge[j] * carry
    h3 = b3 + a3 * jnp.stack(carries)
    return h3.reshape(R, LANES), carry


def _rglru_kernel(x_ref, g_ref, cw_ref, cb_ref, w_ref, bias_ref, lam_ref, o_ref, xp_ref, xc_ref, hf_ref, *, L, R):
    pad = SUBLANES
    zeros = jnp.zeros((pad, LANES), F32)
    xp_ref[0:pad, :] = zeros
    xp_ref[L + pad:L + 2 * pad, :] = zeros
    xp_ref[pad:L + pad, :] = x_ref[0]
    cw = cw_ref[...]
    cb = cb_ref[...]
    lam = lam_ref[...]
    decay = RG_C * jax.nn.softplus(-lam)
    nc = L // R
    hw = 2 * LANES

    def gates(xc, d):
        z = jnp.dot(xc.astype(BF16), w_ref[0, :, d * hw:(d + 1) * hw], preferred_element_type=F32)
        z = z + bias_ref[0, :, d * hw:(d + 1) * hw]
        r = jax.nn.sigmoid(z[:, :LANES])
        i = jax.nn.sigmoid(z[:, LANES:])
        s = r * decay[d:d + 1, :]
        a = jnp.exp(-s)
        b = jnp.sqrt(jnp.tanh(s) * (1.0 + a * a)) * (i * xc)
        return a, b

    def fwd(c, carry):
        base = pl.multiple_of(c * R, R)
        big = xp_ref[pl.ds(base, R + 2 * pad), :]
        xc = cb
        for k in range(cw.shape[0]):
            off = pad + k - RG_CONV_LEFT
            xc = xc + big[off:off + R, :] * cw[k:k + 1, :]
        xc_ref[pl.ds(base, R), :] = xc
        a, b = gates(xc, 0)
        h, carry = _scan_chunk(a, b, carry, False)
        hf_ref[pl.ds(base, R), :] = h
        return carry

    lax.fori_loop(0, nc, fwd, jnp.zeros((SUBLANES, LANES), F32))

    def bwd(s, carry):
        base = pl.multiple_of((nc - 1 - s) * R, R)
        xc = xc_ref[pl.ds(base, R), :]
        a, b = gates(xc, 1)
        h, carry = _scan_chunk(a, b, carry, True)
        gate = jax.nn.gelu(g_ref[0, pl.ds(base, R), :])
        o_ref[0, pl.ds(base, R), :] = (hf_ref[pl.ds(base, R), :] + h) * gate
        return carry

    lax.fori_loop(0, nc, bwd, jnp.zeros((SUBLANES, LANES), F32))


def _rglru(proj, cw, cb, w_cat, b_cat, lam, *, d_rg):
    Bt, L, _ = proj.shape
    nh = d_rg // LANES
    R = _pick(L, (512, 256, 128, 64, 32, 16, 8))
    kern = functools.partial(_rglru_kernel, L=L, R=R)
    return pl.pallas_call(
        kern,
        grid=(Bt, nh),
        in_specs=[
            pl.BlockSpec((1, L, LANES), lambda b, h: (b, 0, h)),
            pl.BlockSpec((1, L, LANES), lambda b, h: (b, 0, nh + h)),
            pl.BlockSpec((cw.shape[0], LANES), lambda b, h: (0, h)),
            pl.BlockSpec((1, LANES), lambda b, h: (0, h)),
            pl.BlockSpec((1, LANES, 4 * LANES), lambda b, h: (h, 0, 0)),
            pl.BlockSpec((1, 1, 4 * LANES), lambda b, h: (h, 0, 0)),
            pl.BlockSpec((2, LANES), lambda b, h: (0, h)),
        ],
        out_specs=pl.BlockSpec((1, L, LANES), lambda b, h: (b, 0, h)),
        out_shape=jax.ShapeDtypeStruct((Bt, L, d_rg), F32),
        scratch_shapes=[
            pltpu.VMEM((L + 2 * SUBLANES, LANES), F32),
            pltpu.VMEM((L, LANES), F32),
            pltpu.VMEM((L, LANES), F32),
        ],
        compiler_params=_params(("parallel", "parallel")),
        name="rglru",
    )(proj, proj, cw, cb, w_cat, b_cat, lam)


def _dft_tables(L):
    n = 2 * L
    A = L // LANES
    nb = np.arange(2 * A, dtype=np.float64)[None, :]
    kb = np.arange(A, dtype=np.float64)[:, None] + 0.5
    phi = 2.0 * np.pi * kb * nb / (2 * A)
    fr_full = np.concatenate([np.cos(phi), -np.sin(phi)], 0)
    na = np.arange(LANES, dtype=np.float64)[None, :]
    psi = 2.0 * np.pi * kb * na / n
    tw = np.stack([np.cos(psi), -np.sin(psi)])
    aa = 2.0 * np.pi * np.outer(np.arange(LANES), np.arange(LANES)) / LANES
    cs = np.concatenate([np.cos(aa), np.sin(aa)], 1)
    theta = phi[:, :A].T
    gi = np.concatenate([np.cos(theta), -np.sin(theta)], 1) * (2.0 / n)
    return dict(
        A=A,
        fr_full=jnp.asarray(fr_full, BF16), fr_half=jnp.asarray(fr_full[:, :A], BF16),
        tw=jnp.asarray(tw, F32), cs=jnp.asarray(cs, BF16), gi=jnp.asarray(gi, BF16),
    )


def _row_stage(fr_ref, tiles, tw_ref, dst_ref, c0):
    A = tw_ref.shape[1]
    y = jnp.dot(fr_ref[...], jnp.concatenate(tiles, axis=1), preferred_element_type=F32)
    twr = tw_ref[0]
    twi = tw_ref[1]
    for q in range(len(tiles)):
        yr = y[:A, q * LANES:(q + 1) * LANES]
        yi = y[A:, q * LANES:(q + 1) * LANES]
        dst_ref[c0 + q, 0] = (yr * twr - yi * twi).astype(BF16)
        dst_ref[c0 + q, 1] = (yr * twi + yi * twr).astype(BF16)


def _lane_stage(src_ref, cs_ref, c0, gc):
    A = src_ref.shape[2]
    a2 = src_ref[pl.ds(c0, gc)].reshape(gc * 2 * A, LANES)
    m = jnp.dot(a2, cs_ref[...], preferred_element_type=F32).reshape(gc, 2, A, 2 * LANES)
    return m[:, 0, :, :LANES], m[:, 0, :, LANES:], m[:, 1, :, :LANES], m[:, 1, :, LANES:]


def _filter_time_kernel(zt_ref, aux_ref, w1_ref, w2_ref, w3_ref, col_ref, w4_ref, delta_ref, o_ref, h3_ref):
    @pl.when(pl.program_id(1) == 0)
    def _():
        fr = col_ref[:, 0:1]
        h = jnp.sin(fr * (jnp.dot(w1_ref[...], zt_ref[...].astype(BF16), preferred_element_type=F32) + col_ref[:, 1:2]))
        h = jnp.sin(fr * (jnp.dot(w2_ref[...], h.astype(BF16), preferred_element_type=F32) + col_ref[:, 2:3]))
        h = jnp.sin(fr * (jnp.dot(w3_ref[...], h.astype(BF16), preferred_element_type=F32) + col_ref[:, 3:4]))
        h3_ref[...] = h.astype(BF16)

    sign = aux_ref[0:1, :]
    t = aux_ref[1:2, :]
    k = jnp.dot(w4_ref[...], h3_ref[...], preferred_element_type=F32)
    o_ref[...] = k * jnp.exp(-t * jnp.abs(delta_ref[:, 0:1])) * sign


def _filter_time(zt, aux, w1t, w2t, w3t, cols, w4t, delta, *, L, d_hy, n_order):
    fw = w2t.shape[0]
    lt = _pick(L, (1024, 512, 256, 128))
    nt = 2 * L // lt
    tr = _pick(d_hy, (512, 256, 128))
    nr = d_hy // tr
    const = lambda j, r: (0, 0)

    def w4_map(j, r):
        direction = (j >= nt // 2).astype(jnp.int32)
        return (((r // nr) * 2 + direction) * nr + r % nr, 0)

    return pl.pallas_call(
        _filter_time_kernel,
        grid=(nt, n_order * nr),
        in_specs=[
            pl.BlockSpec((zt.shape[0], lt), lambda j, r: (0, j)),
            pl.BlockSpec((SUBLANES, lt), lambda j, r: (0, j)),
            pl.BlockSpec(w1t.shape, const),
            pl.BlockSpec(w2t.shape, const),
            pl.BlockSpec(w3t.shape, const),
            pl.BlockSpec(cols.shape, const),
            pl.BlockSpec((tr, fw), w4_map),
            pl.BlockSpec((tr, LANES), lambda j, r: (r % nr, 0)),
        ],
        out_specs=pl.BlockSpec((tr, lt), lambda j, r: (r, j)),
        out_shape=jax.ShapeDtypeStruct((n_order * d_hy, 2 * L), F32),
        scratch_shapes=[pltpu.VMEM((fw, lt), BF16)],
        compiler_params=_params(("arbitrary", "arbitrary")),
        name="hy_filter_time",
    )(zt, aux, w1t, w2t, w3t, cols, w4t, delta)


def _filter_spec_kernel(k_ref, fr_ref, tw_ref, cs_ref, o_ref, a_ref, *, cb, gc):
    def rows(gi, _):
        c0 = pl.multiple_of(gi * gc, gc)
        _row_stage(fr_ref, [k_ref[c0 + q].astype(BF16) for q in range(gc)], tw_ref, a_ref, c0)
        return 0

    lax.fori_loop(0, cb // gc, rows, 0)

    def lanes(gi, _):
        c0 = pl.multiple_of(gi * gc, gc)
        rc, rs, ic, is_ = _lane_stage(a_ref, cs_ref, c0, gc)
        o_ref[pl.ds(c0, gc), 0] = rc + is_
        o_ref[pl.ds(c0, gc), 1] = ic - rs
        return 0

    lax.fori_loop(0, cb // gc, lanes, 0)


def _filter_spec(kt, tabs):
    C = kt.shape[0]
    A = tabs["A"]
    cb = _pick(C, (32, 16, 8))
    gc = 8
    kern = functools.partial(_filter_spec_kernel, cb=cb, gc=gc)
    return pl.pallas_call(
        kern,
        grid=(C // cb,),
        in_specs=[
            pl.BlockSpec((cb, 2 * A, LANES), lambda i: (i, 0, 0)),
            pl.BlockSpec((2 * A, 2 * A), lambda i: (0, 0)),
            pl.BlockSpec((2, A, LANES), lambda i: (0, 0, 0)),
            pl.BlockSpec((LANES, 2 * LANES), lambda i: (0, 0)),
        ],
        out_specs=pl.BlockSpec((cb, 2, A, LANES), lambda i: (i, 0, 0, 0)),
        out_shape=jax.ShapeDtypeStruct((C, 2, A, LANES), F32),
        scratch_shapes=[pltpu.VMEM((cb, 2, A, LANES), BF16)],
        compiler_params=_params(("parallel",)),
        name="hy_filter_spec",
    )(kt.reshape(C, 2 * A, LANES), tabs["fr_full"], tabs["tw"], tabs["cs"])


def _shift_time(x, step):
    A = x.shape[1]
    lane = lax.broadcasted_iota(jnp.int32, x.shape, 2)
    row = lax.broadcasted_iota(jnp.int32, x.shape, 1)
    if step == 1:
        r = pltpu.roll(x, 1, axis=2)
        y = jnp.where(lane == 0, pltpu.roll(r, 1, axis=1), r)
        return jnp.where((lane == 0) & (row == 0), 0.0, y)
    r = pltpu.roll(x, LANES - 1, axis=2)
    y = jnp.where(lane == LANES - 1, pltpu.roll(r, A - 1, axis=1), r)
    return jnp.where((lane == LANES - 1) & (row == A - 1), 0.0, y)


def _hyena_kernel(v_ref, x1_ref, x2_ref, par_ref, ks0_ref, ks1_ref, fr_ref, tw_ref, cs_ref, gi_ref, o_ref,
                  hv_ref, hx1_ref, hx2_ref, ub_ref, a_ref, b_ref, *, cb, gc, cc):
    A = tw_ref.shape[1]
    groups = [g * gc for g in range(cb // gc)]

    def short_conv(s, src, dst, c0):
        par = par_ref[pl.ds(c0, cc)]
        x = src[pl.ds(c0, cc)]
        y = (_shift_time(x, 1) * par[:, 3 * s:3 * s + 1, :] + x * par[:, 3 * s + 1:3 * s + 2, :]
             + _shift_time(x, -1) * par[:, 3 * s + 2:3 * s + 3, :] + par[:, 9 + s:10 + s, :])
        dst[pl.ds(c0, cc)] = y
        return y

    for c0 in range(0, cb, cc):
        ub_ref[pl.ds(c0, cc)] = short_conv(0, v_ref, hv_ref, c0).astype(BF16)

    twr = tw_ref[0]
    twi = tw_ref[1]

    def long_conv(ks_ref, finish, side_work):
        for c0 in groups:
            _row_stage(fr_ref, [ub_ref[c0 + q] for q in range(gc)], tw_ref, a_ref, c0)
        for c0 in groups:
            rc, rs, ic, is_ = _lane_stage(a_ref, cs_ref, c0, gc)
            xr = rc + is_
            xi = ic - rs
            kr = ks_ref[pl.ds(c0, gc), 0]
            ki = ks_ref[pl.ds(c0, gc), 1]
            b_ref[pl.ds(c0, gc), 0] = (xr * kr - xi * ki).astype(BF16)
            b_ref[pl.ds(c0, gc), 1] = (xr * ki + xi * kr).astype(BF16)
        for work in side_work:
            work()
        for c0 in groups:
            rc, rs, ic, is_ = _lane_stage(b_ref, cs_ref, c0, gc)
            qr = rc - is_
            qi = rs + ic
            a_ref[pl.ds(c0, gc), 0] = (qr * twr + qi * twi).astype(BF16)
            a_ref[pl.ds(c0, gc), 1] = (qi * twr - qr * twi).astype(BF16)
        for c0 in groups:
            rhs = jnp.concatenate([a_ref[c0 + q].reshape(2 * A, LANES) for q in range(gc)], axis=1)
            y = jnp.dot(gi_ref[...], rhs, preferred_element_type=F32)
            for q in range(gc):
                finish(c0 + q, y[:, q * LANES:(q + 1) * LANES])

    def finish0(c, y):
        z = hx1_ref[c] * (y + hv_ref[c] * par_ref[c, 12:13, :])
        hx1_ref[c] = z
        ub_ref[c] = z.astype(BF16)

    long_conv(ks0_ref, finish0,
              [functools.partial(short_conv, 1, x1_ref, hx1_ref, c0) for c0 in range(0, cb, cc)])

    def finish1(c, y):
        o_ref[c] = hx2_ref[c] * (y + hx1_ref[c] * par_ref[c, 13:14, :])

    long_conv(ks1_ref, finish1,
              [functools.partial(short_conv, 2, x2_ref, hx2_ref, c0) for c0 in range(0, cb, cc)])


def _hyena(hyt, par, ks, tabs, *, Bt, L, d_hy):
    A = tabs["A"]
    cb = _pick(d_hy, (32, 16, 8))
    gc = 8
    ncb = d_hy // cb
    x3 = hyt.reshape(3 * d_hy, Bt * A, LANES)
    kern = functools.partial(_hyena_kernel, cb=cb, gc=gc, cc=4)
    const2 = lambda j, b: (0, 0)
    tile = (cb, A, LANES)
    return pl.pallas_call(
        kern,
        grid=(ncb, Bt),
        in_specs=[
            pl.BlockSpec(tile, lambda j, b: (j, b, 0)),
            pl.BlockSpec(tile, lambda j, b: (ncb + j, b, 0)),
            pl.BlockSpec(tile, lambda j, b: (2 * ncb + j, b, 0)),
            pl.BlockSpec((cb, 16, LANES), lambda j, b: (j, 0, 0)),
            pl.BlockSpec((cb, 2, A, LANES), lambda j, b: (j, 0, 0, 0)),
            pl.BlockSpec((cb, 2, A, LANES), lambda j, b: (ncb + j, 0, 0, 0)),
            pl.BlockSpec((2 * A, A), const2),
            pl.BlockSpec((2, A, LANES), lambda j, b: (0, 0, 0)),
            pl.BlockSpec((LANES, 2 * LANES), const2),
            pl.BlockSpec((A, 2 * A), const2),
        ],
        out_specs=pl.BlockSpec(tile, lambda j, b: (j, b, 0)),
        out_shape=jax.ShapeDtypeStruct((d_hy, Bt * A, LANES), F32),
        scratch_shapes=[
            pltpu.VMEM(tile, F32),
            pltpu.VMEM(tile, F32),
            pltpu.VMEM(tile, F32),
            pltpu.VMEM(tile, BF16),
            pltpu.VMEM((cb, 2, A, LANES), BF16),
            pltpu.VMEM((cb, 2, A, LANES), BF16),
        ],
        compiler_params=_params(("parallel", "arbitrary")),
        name="hyena",
    )(x3, x3, x3, par, ks, ks, tabs["fr_half"], tabs["tw"], tabs["cs"], tabs["gi"])


def _mix_norm_kernel(rg_ref, hy_ref, grg_ref, ghy_ref, o_ref, *, d_rg):
    o_ref[:, :d_rg] = _rms(rg_ref[...], grg_ref[...]).astype(BF16)
    hy = hy_ref[...]
    scale = lax.rsqrt(jnp.mean(hy * hy, axis=0, keepdims=True) + EPS)
    o_ref[:, d_rg:] = (hy * scale * ghy_ref[:, 0:1]).T.astype(BF16)


def _mix_norm(rg, hyt, g_rg, g_hy_col):
    T, d_rg = rg.shape
    d_hy = hyt.shape[0]
    tm = _pick(T, (256, 128))
    kern = functools.partial(_mix_norm_kernel, d_rg=d_rg)
    return pl.pallas_call(
        kern,
        grid=(T // tm,),
        in_specs=[
            pl.BlockSpec((tm, d_rg), lambda i: (i, 0)),
            pl.BlockSpec((d_hy, tm), lambda i: (0, i)),
            pl.BlockSpec((1, d_rg), lambda i: (0, 0)),
            pl.BlockSpec((d_hy, LANES), lambda i: (0, 0)),
        ],
        out_specs=pl.BlockSpec((tm, d_rg + d_hy), lambda i: (i, 0)),
        out_shape=jax.ShapeDtypeStruct((T, d_rg + d_hy), BF16),
        compiler_params=_params(("parallel",)),
        name="mix_norm",
    )(rg, hyt, g_rg.reshape(1, d_rg), g_hy_col)


def _post_mix_kernel(xa_ref, xb_ref, f_ref, g1_ref, g2_ref, x1_ref, hn_ref, *, n_first):
    x = jnp.where(pl.program_id(0) < n_first, xa_ref[0], xb_ref[0])
    x1 = x + _rms(f_ref[0], g1_ref[...])
    x1_ref[0] = x1
    hn_ref[0] = _rms(x1, g2_ref[...]).astype(BF16)


def _post_mix(xs, f, g1, g2, *, L):
    D = xs[0].shape[-1]
    n_first = xs[0].shape[0]
    Bt = sum(x.shape[0] for x in xs)
    tm = _pick(L, (256, 128))
    nt = L // tm
    block = (1, tm, D)
    index = lambda b, i: (b, i, 0)
    if len(xs) == 2:
        spec_a, spec_b = _pair_specs(block, n_first, nt - 1, index)
    else:
        spec_a = spec_b = pl.BlockSpec(block, index)
        xs = list(xs) * 2
    row = pl.BlockSpec(block, index)
    vec = pl.BlockSpec((1, D), lambda b, i: (0, 0))
    x1, hn = pl.pallas_call(
        functools.partial(_post_mix_kernel, n_first=n_first),
        grid=(Bt, nt),
        in_specs=[spec_a, spec_b, row, vec, vec],
        out_specs=[row, row],
        out_shape=[jax.ShapeDtypeStruct((Bt, L, D), F32), jax.ShapeDtypeStruct((Bt, L, D), BF16)],
        compiler_params=_params(("parallel", "parallel")),
        name="post_mix",
    )(xs[0], xs[1], f.reshape(Bt, L, D), g1.reshape(1, D), g2.reshape(1, D))
    return x1.reshape(Bt * L, D), hn.reshape(Bt * L, D)


def _post_ffn_pair_kernel(x_ref, f_ref, g_ref, ya_ref, yb_ref, *, n_first):
    y = x_ref[0] + _rms(f_ref[0], g_ref[...])

    @pl.when(pl.program_id(0) < n_first)
    def _():
        ya_ref[0] = y

    @pl.when(pl.program_id(0) >= n_first)
    def _():
        yb_ref[0] = y


def _post_ffn_kernel(x_ref, f_ref, g_ref, y_ref):
    y_ref[0] = x_ref[0] + _rms(f_ref[0], g_ref[...])


def _post_ffn(x, f, g, *, L, splits):
    T, D = x.shape
    Bt = T // L
    tm = _pick(L, (256, 128))
    nt = L // tm
    block = (1, tm, D)
    index = lambda b, i: (b, i, 0)
    row = pl.BlockSpec(block, index)
    args = (x.reshape(Bt, L, D), f.reshape(Bt, L, D), g.reshape(1, D))
    in_specs = [row, row, pl.BlockSpec((1, D), lambda b, i: (0, 0))]
    if len(splits) == 1:
        return (pl.pallas_call(
            _post_ffn_kernel,
            grid=(Bt, nt),
            in_specs=in_specs,
            out_specs=row,
            out_shape=jax.ShapeDtypeStruct((Bt, L, D), F32),
            compiler_params=_params(("parallel", "parallel")),
            name="post_ffn",
        )(*args),)
    n_first = splits[0]
    spec_a, spec_b = _pair_specs(block, n_first, nt - 1, index)
    return tuple(pl.pallas_call(
        functools.partial(_post_ffn_pair_kernel, n_first=n_first),
        grid=(Bt, nt),
        in_specs=in_specs,
        out_specs=[spec_a, spec_b],
        out_shape=[jax.ShapeDtypeStruct((n, L, D), F32) for n in splits],
        compiler_params=_params(("arbitrary", "arbitrary")),
        name="post_ffn",
    )(*args))


FFN_LAG = 2


def _ffn_in_kernel(a_ref, wg_ref, wu_ref, cw_ref, cb_ref, o_ref, g_ref, u_ref, *, tm, rc, n_tiles, tiles_per_seq):
    s = pl.program_id(0)
    ring = FFN_LAG + 1

    @pl.when(s == 0)
    def _():
        g_ref[...] = jnp.zeros_like(g_ref)
        u_ref[...] = jnp.zeros_like(u_ref)

    new = s % ring
    mid = (s + 1) % ring
    nxt = (s + 2) % ring

    e = jnp.maximum(s - FFN_LAG, 0) % n_tiles
    first = (e % tiles_per_seq) == 0
    last = (e % tiles_per_seq) == tiles_per_seq - 1
    cw = cw_ref[...]
    cb = cb_ref[...]
    pad = SUBLANES
    for r0 in range(0, tm, rc):
        if r0 == 0:
            top = jnp.where(first, 0.0, g_ref[new, tm - pad:tm, :])
        else:
            top = g_ref[mid, r0 - pad:r0, :]
        if r0 + rc == tm:
            bottom = jnp.where(last, 0.0, g_ref[nxt, 0:pad, :])
        else:
            bottom = g_ref[mid, r0 + rc:r0 + rc + pad, :]
        big = jnp.concatenate([top, g_ref[mid, r0:r0 + rc, :], bottom], axis=0)
        y = (big[pad - 1:pad - 1 + rc] * cw[0:1, :] + big[pad:pad + rc] * cw[1:2, :]
             + big[pad + 1:pad + 1 + rc] * cw[2:3, :] + cb)
        o_ref[r0:r0 + rc, :] = (jax.nn.gelu(y) * u_ref[mid, r0:r0 + rc, :]).astype(BF16)

        a = a_ref[r0:r0 + rc, :]
        g_ref[new, r0:r0 + rc, :] = jnp.dot(a, wg_ref[...], preferred_element_type=F32)
        u_ref[new, r0:r0 + rc, :] = jnp.dot(a, wu_ref[...], preferred_element_type=F32)


def _ffn_in(a, w, cw, cb, *, L, d_ff):
    T, D = a.shape
    tm = _pick(L, (1024, 512, 256, 128))
    tn = _pick(d_ff, (256, 128))
    ncol = d_ff // tn
    n_tiles = T // tm
    steps = ncol * n_tiles

    def mm_tile(s):
        return jnp.minimum(s, steps - 1)

    def ew_tile(s):
        return jnp.maximum(s - FFN_LAG, 0)

    kern = functools.partial(_ffn_in_kernel, tm=tm, rc=_pick(tm, (256, 128)), n_tiles=n_tiles, tiles_per_seq=L // tm)
    return pl.pallas_call(
        kern,
        grid=(steps + FFN_LAG,),
        in_specs=[
            pl.BlockSpec((tm, D), lambda s: (mm_tile(s) % n_tiles, 0)),
            pl.BlockSpec((D, tn), lambda s: (0, mm_tile(s) // n_tiles)),
            pl.BlockSpec((D, tn), lambda s: (0, ncol + mm_tile(s) // n_tiles)),
            pl.BlockSpec((3, tn), lambda s: (0, ew_tile(s) // n_tiles)),
            pl.BlockSpec((1, tn), lambda s: (0, ew_tile(s) // n_tiles)),
        ],
        out_specs=pl.BlockSpec((tm, tn), lambda s: (ew_tile(s) % n_tiles, ew_tile(s) // n_tiles)),
        out_shape=jax.ShapeDtypeStruct((T, d_ff), BF16),
        scratch_shapes=[pltpu.VMEM((FFN_LAG + 1, tm, tn), F32), pltpu.VMEM((FFN_LAG + 1, tm, tn), F32)],
        compiler_params=_params(("arbitrary",)),
        name="ffn_in",
    )(a, w, w, cw, cb.reshape(1, d_ff))


def _lane_bcast(v):
    return jnp.broadcast_to(v[..., None], v.shape + (LANES,))


def _hyena_filter_spectrum(L, d_hy, n_order, w1, b1, w2, b2, w3, b3, w4, freq, tabs):
    emb = w1.shape[0]
    bands = (emb - 1) // 2
    n = jnp.arange(2 * L)
    m = jnp.where(n < L, n, 2 * L - n).astype(F32)
    sign = jnp.where(n < L, 1.0, jnp.where(n == L, 0.0, -1.0)).astype(F32)
    t = m / (L - 1)
    band = jnp.linspace(1e-4, bands - 1, bands, dtype=F32)
    ang = (2.0 * math.pi / L) * m[None, :] * band[:, None]
    zt = jnp.concatenate([t[None, :], jnp.cos(ang), -jnp.sin(ang)], axis=0)
    kpad = -(-emb // BF16_ROWS) * BF16_ROWS
    zt = jnp.pad(zt, ((0, kpad - emb), (0, 0)))
    aux = jnp.zeros((SUBLANES, 2 * L), F32).at[0].set(sign).at[1].set(t)
    w1t = jnp.pad(w1.T, ((0, 0), (0, kpad - emb))).astype(BF16)
    fw = w2.shape[0]
    cols = jnp.zeros((fw, LANES), F32)
    cols = cols.at[:, 0].set(freq).at[:, 1].set(b1).at[:, 2].set(b2).at[:, 3].set(b3)
    max_decay = math.log(HY_TARGET) / HY_FAST_DECAY
    min_decay = math.log(HY_TARGET) / HY_SLOW_DECAY
    delta = _lane_bcast(jnp.linspace(min_decay, max_decay, d_hy, dtype=F32))
    kt = _filter_time(zt, aux, w1t, w2.T.astype(BF16), w3.T.astype(BF16), cols, w4.T.astype(BF16), delta,
                      L=L, d_hy=d_hy, n_order=n_order)
    return _filter_spec(kt, tabs)


def _encoder_layer(xs, p, splits):
    (pre_mix_norm, w_in, rg_conv_w, rg_conv_b, rg_a_w, rg_a_b, rg_x_w, rg_x_b, rg_lambda, hy_conv_w, hy_conv_b,
     hy_w1, hy_b1, hy_w2, hy_b2, hy_w3, hy_b3, hy_w4, hy_sin_freq, hy_bias, rg_out_norm, hy_out_norm, w_out,
     post_mix_norm, pre_ffn_norm, w_ffn_in, ffn_conv_w, ffn_conv_b, w_ffn_out, post_ffn_norm) = p
    _, L, D = xs[0].shape
    Bt = sum(x.shape[0] for x in xs)
    T = Bt * L
    d_rg = rg_conv_w.shape[-1]
    n_order, d_hy = hy_bias.shape
    d_ff = ffn_conv_w.shape[-1]
    nh = rg_a_w.shape[1]
    assert d_rg == nh * LANES and n_order == 2 and L % (LANES * BF16_ROWS) == 0

    hn, hnt = _norm_in(xs, pre_mix_norm, L=L)
    w_rg = w_in[:, :2 * d_rg].astype(BF16)
    w_hyt = w_in[:, 2 * d_rg:].T.astype(BF16)
    tm = _pick(T, (1024, 512, 256, 128))
    proj_rg = _mm(hn, w_rg, tm=tm, tn=_pick(2 * d_rg, (1024, 512, 256, 128)), name="proj_rg")
    hyt = _mm(w_hyt, hnt, tm=_pick(3 * d_hy, (1024, 512, 256, 128)), tn=tm, name="proj_hy")

    w_cat = jnp.concatenate([rg_a_w[0], rg_x_w[0], rg_a_w[1], rg_x_w[1]], axis=-1).astype(BF16)
    b_cat = jnp.concatenate([rg_a_b[0], rg_x_b[0], rg_a_b[1], rg_x_b[1]], axis=0)
    b_cat = b_cat.reshape(4, nh, LANES).transpose(1, 0, 2).reshape(nh, 1, 4 * LANES)
    rg_y = _rglru(proj_rg.reshape(Bt, L, 2 * d_rg), rg_conv_w, rg_conv_b.reshape(1, d_rg), w_cat, b_cat,
                  rg_lambda, d_rg=d_rg)

    tabs = _dft_tables(L)
    ks = _hyena_filter_spectrum(L, d_hy, n_order, hy_w1, hy_b1, hy_w2, hy_b2, hy_w3, hy_b3, hy_w4, hy_sin_freq, tabs)
    cw = hy_conv_w.reshape(3, 3, d_hy)
    par = jnp.concatenate([cw.transpose(1, 0, 2).reshape(9, d_hy), hy_conv_b.reshape(3, d_hy), hy_bias,
                           jnp.zeros((2, d_hy), F32)], axis=0)
    par = _lane_bcast(par.T)
    hy_y = _hyena(hyt, par, ks, tabs, Bt=Bt, L=L, d_hy=d_hy)

    mix = _mix_norm(rg_y.reshape(T, d_rg), hy_y.reshape(d_hy, T), rg_out_norm, _lane_bcast(hy_out_norm))
    f = _mm(mix, w_out.astype(BF16), tm=tm, tn=_pick(D, (1024, 512, 256, 128)), name="out_proj")
    x1, hn2 = _post_mix(xs, f, post_mix_norm, pre_ffn_norm, L=L)

    fg = _ffn_in(hn2, w_ffn_in.astype(BF16), ffn_conv_w, ffn_conv_b, L=L, d_ff=d_ff)
    f2 = _mm(fg, w_ffn_out.astype(BF16), tm=_pick(T, (512, 256, 128)), tn=_pick(D, (512, 256, 128)), name="ffn_out")
    return _post_ffn(x1, f2, post_ffn_norm, L=L, splits=splits)


def _run_trunk(xs, params):
    splits = tuple(x.shape[0] for x in xs)
    depth = params[0].shape[0]
    for l in range(depth):
        last = l == depth - 1
        xs = _encoder_layer(xs, [q[l] for q in params], splits if last else (sum(splits),))
    return xs


def kernel(x_prompt, x_sample, pre_mix_norm, w_in, rg_conv_w, rg_conv_b, rg_a_w, rg_a_b, rg_x_w, rg_x_b, rg_lambda, hy_conv_w, hy_conv_b, hy_w1, hy_b1, hy_w2, hy_b2, hy_w3, hy_b3, hy_w4, hy_sin_freq, hy_bias, rg_out_norm, hy_out_norm, w_out, post_mix_norm, pre_ffn_norm, w_ffn_in, ffn_conv_w, ffn_conv_b, w_ffn_out, post_ffn_norm):
    params = (pre_mix_norm, w_in, rg_conv_w, rg_conv_b, rg_a_w, rg_a_b, rg_x_w, rg_x_b, rg_lambda, hy_conv_w,
              hy_conv_b, hy_w1, hy_b1, hy_w2, hy_b2, hy_w3, hy_b3, hy_w4, hy_sin_freq, hy_bias, rg_out_norm,
              hy_out_norm, w_out, post_mix_norm, pre_ffn_norm, w_ffn_in, ffn_conv_w, ffn_conv_b, w_ffn_out,
              post_ffn_norm)
    if x_prompt.shape[1:] == x_sample.shape[1:]:
        return _run_trunk((x_prompt, x_sample), params)
    return (_run_trunk((x_prompt,), params)[0], _run_trunk((x_sample,), params)[0])
```

```python
import functools
import math

import numpy as np
import jax
import jax.numpy as jnp
from jax import lax
from jax.experimental import pallas as pl
from jax.experimental.pallas import tpu as pltpu

F32 = jnp.float32
BF16 = jnp.bfloat16

EPS = 1e-6
TINY = 1e-30
RG_C = 8.0
RG_CONV_LEFT = 2
HY_FAST_DECAY = 0.3
HY_SLOW_DECAY = 1.5
HY_TARGET = 1e-2

LANES = 128
SUBLANES = 8
BF16_ROWS = 16
VMEM_LIMIT = 56 * 1024 * 1024


def _pick(n, candidates):
    for c in candidates:
        if n % c == 0:
            return c
    raise ValueError(f"no tile in {candidates} divides {n}")


def _params(sem):
    return pltpu.CompilerParams(dimension_semantics=sem, vmem_limit_bytes=VMEM_LIMIT)


def _rms(x, g):
    return x * lax.rsqrt(jnp.mean(x * x, axis=-1, keepdims=True) + EPS) * g


def _pair_specs(block, n_first, inner_last, make_index):
    def first(o, i):
        return make_index(jnp.minimum(o, n_first - 1), jnp.where(o < n_first, i, inner_last))

    def second(o, i):
        return make_index(jnp.maximum(o - n_first, 0), jnp.where(o >= n_first, i, 0))

    return pl.BlockSpec(block, first), pl.BlockSpec(block, second)


def _norm_in_kernel(xa_ref, xb_ref, g_ref, hn_ref, hnt_ref, *, n_first):
    x = jnp.where(pl.program_id(0) < n_first, xa_ref[0], xb_ref[0])
    y = _rms(x, g_ref[...])
    hn_ref[0] = y.astype(BF16)
    hnt_ref[...] = y.T.astype(BF16)


def _norm_in(xs, g, *, L):
    D = xs[0].shape[-1]
    n_first = xs[0].shape[0]
    Bt = sum(x.shape[0] for x in xs)
    tm = _pick(L, (256, 128))
    nt = L // tm
    block = (1, tm, D)
    index = lambda b, i: (b, i, 0)
    if len(xs) == 2:
        spec_a, spec_b = _pair_specs(block, n_first, nt - 1, index)
    else:
        spec_a = spec_b = pl.BlockSpec(block, index)
        xs = list(xs) * 2
    hn, hnt = pl.pallas_call(
        functools.partial(_norm_in_kernel, n_first=n_first),
        grid=(Bt, nt),
        in_specs=[spec_a, spec_b, pl.BlockSpec((1, D), lambda b, i: (0, 0))],
        out_specs=[pl.BlockSpec(block, index), pl.BlockSpec((D, tm), lambda b, i: (0, b * nt + i))],
        out_shape=[jax.ShapeDtypeStruct((Bt, L, D), BF16), jax.ShapeDtypeStruct((D, Bt * L), BF16)],
        compiler_params=_params(("parallel", "parallel")),
        name="norm_in",
    )(xs[0], xs[1], g.reshape(1, D))
    return hn.reshape(Bt * L, D), hnt


def _mm_kernel(a_ref, b_ref, o_ref):
    o_ref[...] = jnp.dot(a_ref[...], b_ref[...], preferred_element_type=F32).astype(o_ref.dtype)


def _mm(a, b, *, tm, tn, name):
    M, K = a.shape
    _, N = b.shape
    return pl.pallas_call(
        _mm_kernel,
        grid=(N // tn, M // tm),
        in_specs=[pl.BlockSpec((tm, K), lambda j, i: (i, 0)), pl.BlockSpec((K, tn), lambda j, i: (0, j))],
        out_specs=pl.BlockSpec((tm, tn), lambda j, i: (i, j)),
        out_shape=jax.ShapeDtypeStruct((M, N), F32),
        compiler_params=_params(("parallel", "parallel")),
        name=name,
    )(a, b)


def _mm_wcast_kernel(a_ref, b32_ref, o_ref, b_ref):
    @pl.when(pl.program_id(1) == 0)
    def _():
        b_ref[...] = b32_ref[...].astype(BF16)

    o_ref[...] = jnp.dot(a_ref[...], b_ref[...], preferred_element_type=F32)


def _mm_wcast(a, b32, *, n_cols, tm, tn, name):
    M, K = a.shape
    return pl.pallas_call(
        _mm_wcast_kernel,
        grid=(n_cols // tn, M // tm),
        in_specs=[pl.BlockSpec((tm, K), lambda j, i: (i, 0)), pl.BlockSpec((K, tn), lambda j, i: (0, j))],
        out_specs=pl.BlockSpec((tm, tn), lambda j, i: (i, j)),
        out_shape=jax.ShapeDtypeStruct((M, n_cols), F32),
        scratch_shapes=[pltpu.VMEM((K, tn), BF16)],
        compiler_params=_params(("parallel", "arbitrary")),
        name=name,
    )(a, b32)


def _scan_chunk(a, b, carry, reverse):
    R = a.shape[0]
    nv = R // SUBLANES
    a3 = a.reshape(nv, SUBLANES, LANES)
    b3 = b.reshape(nv, SUBLANES, LANES)
    row = lax.broadcasted_iota(jnp.int32, (nv, SUBLANES, LANES), 1)
    for d in (1, 2, 4):
        shift = SUBLANES - d if reverse else d
        valid = (row < SUBLANES - d) if reverse else (row >= d)
        sa = pltpu.roll(a3, shift, axis=1)
        sb = pltpu.roll(b3, shift, axis=1)
        b3 = b3 + jnp.where(valid, a3, 0.0) * sb
        a3 = a3 * jnp.where(valid, sa, 1.0)
    edge = 0 if reverse else SUBLANES - 1
    a_edge = jnp.broadcast_to(a3[:, edge:edge + 1, :], a3.shape)
    b_edge = jnp.broadcast_to(b3[:, edge:edge + 1, :], b3.shape)
    order = range(nv - 1, -1, -1) if reverse else range(nv)
    carries = [None] * nv
    for j in order:
        carries[j] = carry
        carry = b_edge[j] + a_edge[j] * carry
    h3 = b3 + a3 * jnp.stack(carries)
    return h3.reshape(R, LANES), carry


def _rglru_kernel(x_ref, g_ref, cw_ref, cb_ref, w_ref, bias_ref, lam_ref, o_ref, xp_ref, xc_ref, hf_ref, *, L, R):
    pad = SUBLANES
    zeros = jnp.zeros((pad, LANES), F32)
    xp_ref[0:pad, :] = zeros
    xp_ref[L + pad:L + 2 * pad, :] = zeros
    xp_ref[pad:L + pad, :] = x_ref[0]
    cw = cw_ref[...]
    cb = cb_ref[...]
    lam = lam_ref[...]
    decay = RG_C * jax.nn.softplus(-lam)
    nc = L // R
    hw = 2 * LANES

    def gates(xc, d):
        z = jnp.dot(xc.astype(BF16), w_ref[0, :, d * hw:(d + 1) * hw], preferred_element_type=F32)
        z = z + bias_ref[0, :, d * hw:(d + 1) * hw]
        r = jax.nn.sigmoid(z[:, :LANES])
        i = jax.nn.sigmoid(z[:, LANES:])
        s = r * decay[d:d + 1, :]
        a = jnp.exp(-s)
        q = jnp.tanh(s) * (1.0 + a * a)
        b = (q * lax.rsqrt(jnp.maximum(q, TINY))) * (i * xc)
        return a, b

    def fwd(c, carry):
        base = pl.multiple_of(c * R, R)
        big = xp_ref[pl.ds(base, R + 2 * pad), :]
        xc = cb
        for k in range(cw.shape[0]):
            off = pad + k - RG_CONV_LEFT
            xc = xc + big[off:off + R, :] * cw[k:k + 1, :]
        xc_ref[pl.ds(base, R), :] = xc
        a, b = gates(xc, 0)
        h, carry = _scan_chunk(a, b, carry, False)
        hf_ref[pl.ds(base, R), :] = h
        return carry

    lax.fori_loop(0, nc, fwd, jnp.zeros((SUBLANES, LANES), F32))

    def bwd(s, carry):
        base = pl.multiple_of((nc - 1 - s) * R, R)
        xc = xc_ref[pl.ds(base, R), :]
        a, b = gates(xc, 1)
        h, carry = _scan_chunk(a, b, carry, True)
        gate = jax.nn.gelu(g_ref[0, pl.ds(base, R), :])
        o_ref[0, pl.ds(base, R), :] = (hf_ref[pl.ds(base, R), :] + h) * gate
        return carry

    lax.fori_loop(0, nc, bwd, jnp.zeros((SUBLANES, LANES), F32))


def _rglru(proj, cw, cb, w_cat, b_cat, lam, *, d_rg):
    Bt, L, _ = proj.shape
    nh = d_rg // LANES
    R = _pick(L, (512, 256, 128, 64, 32, 16, 8))
    kern = functools.partial(_rglru_kernel, L=L, R=R)
    return pl.pallas_call(
        kern,
        grid=(Bt, nh),
        in_specs=[
            pl.BlockSpec((1, L, LANES), lambda b, h: (b, 0, h)),
            pl.BlockSpec((1, L, LANES), lambda b, h: (b, 0, nh + h)),
            pl.BlockSpec((cw.shape[0], LANES), lambda b, h: (0, h)),
            pl.BlockSpec((1, LANES), lambda b, h: (0, h)),
            pl.BlockSpec((1, LANES, 4 * LANES), lambda b, h: (h, 0, 0)),
            pl.BlockSpec((1, 1, 4 * LANES), lambda b, h: (h, 0, 0)),
            pl.BlockSpec((2, LANES), lambda b, h: (0, h)),
        ],
        out_specs=pl.BlockSpec((1, L, LANES), lambda b, h: (b, 0, h)),
        out_shape=jax.ShapeDtypeStruct((Bt, L, d_rg), F32),
        scratch_shapes=[
            pltpu.VMEM((L + 2 * SUBLANES, LANES), F32),
            pltpu.VMEM((L, LANES), F32),
            pltpu.VMEM((L, LANES), F32),
        ],
        compiler_params=_params(("parallel", "parallel")),
        name="rglru",
    )(proj, proj, cw, cb, w_cat, b_cat, lam)


def _dft_tables(L):
    n = 2 * L
    A = L // LANES
    nb = np.arange(2 * A, dtype=np.float64)[None, :]
    kb = np.arange(A, dtype=np.float64)[:, None] + 0.5
    phi = 2.0 * np.pi * kb * nb / (2 * A)
    fr_full = np.concatenate([np.cos(phi), -np.sin(phi)], 0)
    na = np.arange(LANES, dtype=np.float64)[None, :]
    psi = 2.0 * np.pi * kb * na / n
    tw = np.stack([np.cos(psi), -np.sin(psi)])
    aa = 2.0 * np.pi * np.outer(np.arange(LANES), np.arange(LANES)) / LANES
    c, s = np.cos(aa), np.sin(aa)
    lf = np.block([[c, -s], [s, c]])
    lb = np.block([[c, s], [-s, c]])
    theta = phi[:, :A].T
    gi = np.concatenate([np.cos(theta), -np.sin(theta)], 1) * (2.0 / n)
    return dict(
        A=A,
        fr_full=jnp.asarray(fr_full, BF16), fr_half=jnp.asarray(fr_full[:, :A], BF16),
        tw=jnp.asarray(tw, F32), lf=jnp.asarray(lf, BF16), lb=jnp.asarray(lb, BF16), gi=jnp.asarray(gi, BF16),
    )


def _row_stage(fr_ref, tiles, tw_ref, dst_ref, c0):
    A = tw_ref.shape[1]
    y = jnp.dot(fr_ref[...], jnp.concatenate(tiles, axis=1), preferred_element_type=F32)
    twr = tw_ref[0]
    twi = tw_ref[1]
    for q in range(len(tiles)):
        yr = y[:A, q * LANES:(q + 1) * LANES]
        yi = y[A:, q * LANES:(q + 1) * LANES]
        dst_ref[c0 + q, :, :LANES] = (yr * twr - yi * twi).astype(BF16)
        dst_ref[c0 + q, :, LANES:] = (yr * twi + yi * twr).astype(BF16)


def _lane_stage(src_ref, m_ref, c0, gc):
    A = src_ref.shape[1]
    a2 = src_ref[pl.ds(c0, gc)].reshape(gc * A, 2 * LANES)
    return jnp.dot(a2, m_ref[...], preferred_element_type=F32).reshape(gc, A, 2 * LANES)


def _filter_time_kernel(zt_ref, aux_ref, w1_ref, w2_ref, w3_ref, col_ref, w4_ref, delta_ref, o_ref, h3_ref):
    @pl.when(pl.program_id(1) == 0)
    def _():
        fr = col_ref[:, 0:1]
        h = jnp.sin(fr * (jnp.dot(w1_ref[...], zt_ref[...].astype(BF16), preferred_element_type=F32) + col_ref[:, 1:2]))
        h = jnp.sin(fr * (jnp.dot(w2_ref[...], h.astype(BF16), preferred_element_type=F32) + col_ref[:, 2:3]))
        h = jnp.sin(fr * (jnp.dot(w3_ref[...], h.astype(BF16), preferred_element_type=F32) + col_ref[:, 3:4]))
        h3_ref[...] = h.astype(BF16)

    sign = aux_ref[0:1, :]
    t = aux_ref[1:2, :]
    k = jnp.dot(w4_ref[...], h3_ref[...], preferred_element_type=F32)
    o_ref[...] = k * jnp.exp(-t * jnp.abs(delta_ref[:, 0:1])) * sign


def _filter_time(zt, aux, w1t, w2t, w3t, cols, w4t, delta, *, L, d_hy, n_order):
    fw = w2t.shape[0]
    lt = _pick(L, (1024, 512, 256, 128))
    nt = 2 * L // lt
    tr = _pick(d_hy, (512, 256, 128))
    nr = d_hy // tr
    const = lambda j, r: (0, 0)

    def w4_map(j, r):
        direction = (j >= nt // 2).astype(jnp.int32)
        return (((r // nr) * 2 + direction) * nr + r % nr, 0)

    return pl.pallas_call(
        _filter_time_kernel,
        grid=(nt, n_order * nr),
        in_specs=[
            pl.BlockSpec((zt.shape[0], lt), lambda j, r: (0, j)),
            pl.BlockSpec((SUBLANES, lt), lambda j, r: (0, j)),
            pl.BlockSpec(w1t.shape, const),
            pl.BlockSpec(w2t.shape, const),
            pl.BlockSpec(w3t.shape, const),
            pl.BlockSpec(cols.shape, const),
            pl.BlockSpec((tr, fw), w4_map),
            pl.BlockSpec((tr, LANES), lambda j, r: (r % nr, 0)),
        ],
        out_specs=pl.BlockSpec((tr, lt), lambda j, r: (r, j)),
        out_shape=jax.ShapeDtypeStruct((n_order * d_hy, 2 * L), F32),
        scratch_shapes=[pltpu.VMEM((fw, lt), BF16)],
        compiler_params=_params(("arbitrary", "arbitrary")),
        name="hy_filter_time",
    )(zt, aux, w1t, w2t, w3t, cols, w4t, delta)


def _filter_spec_kernel(k_ref, fr_ref, tw_ref, lf_ref, o_ref, a_ref, *, cb, gc):
    for c0 in range(0, cb, gc):
        _row_stage(fr_ref, [k_ref[c0 + q].astype(BF16) for q in range(gc)], tw_ref, a_ref, c0)
    for c0 in range(0, cb, gc):
        o_ref[pl.ds(c0, gc)] = _lane_stage(a_ref, lf_ref, c0, gc)


def _filter_spec(kt, tabs):
    C = kt.shape[0]
    A = tabs["A"]
    cb = _pick(C, (32, 16, 8))
    gc = 8
    kern = functools.partial(_filter_spec_kernel, cb=cb, gc=gc)
    return pl.pallas_call(
        kern,
        grid=(C // cb,),
        in_specs=[
            pl.BlockSpec((cb, 2 * A, LANES), lambda i: (i, 0, 0)),
            pl.BlockSpec((2 * A, 2 * A), lambda i: (0, 0)),
            pl.BlockSpec((2, A, LANES), lambda i: (0, 0, 0)),
            pl.BlockSpec((2 * LANES, 2 * LANES), lambda i: (0, 0)),
        ],
        out_specs=pl.BlockSpec((cb, A, 2 * LANES), lambda i: (i, 0, 0)),
        out_shape=jax.ShapeDtypeStruct((C, A, 2 * LANES), F32),
        scratch_shapes=[pltpu.VMEM((cb, A, 2 * LANES), BF16)],
        compiler_params=_params(("parallel",)),
        name="hy_filter_spec",
    )(kt.reshape(C, 2 * A, LANES), tabs["fr_full"], tabs["tw"], tabs["lf"])


def _shift_time(x, step):
    A = x.shape[1]
    lane = lax.broadcasted_iota(jnp.int32, x.shape, 2)
    row = lax.broadcasted_iota(jnp.int32, x.shape, 1)
    if step == 1:
        r = pltpu.roll(x, 1, axis=2)
        y = jnp.where(lane == 0, pltpu.roll(r, 1, axis=1), r)
        return jnp.where((lane == 0) & (row == 0), 0.0, y)
    r = pltpu.roll(x, LANES - 1, axis=2)
    y = jnp.where(lane == LANES - 1, pltpu.roll(r, A - 1, axis=1), r)
    return jnp.where((lane == LANES - 1) & (row == A - 1), 0.0, y)


def _hyena_kernel(v_ref, x1_ref, x2_ref, par_ref, ks0_ref, ks1_ref, fr_ref, tw_ref, lf_ref, lb_ref, gi_ref, o_ref,
                  hv_ref, hx1_ref, hx2_ref, ub_ref, a_ref, b_ref, *, cb, gc, cc):
    A = tw_ref.shape[1]
    groups = [g * gc for g in range(cb // gc)]

    def short_conv(s, src, dst, c0):
        par = par_ref[pl.ds(c0, cc)]
        x = src[pl.ds(c0, cc)]
        y = (_shift_time(x, 1) * par[:, 3 * s:3 * s + 1, :] + x * par[:, 3 * s + 1:3 * s + 2, :]
             + _shift_time(x, -1) * par[:, 3 * s + 2:3 * s + 3, :] + par[:, 9 + s:10 + s, :])
        dst[pl.ds(c0, cc)] = y
        return y

    for c0 in range(0, cb, cc):
        ub_ref[pl.ds(c0, cc)] = short_conv(0, v_ref, hv_ref, c0).astype(BF16)

    twr = tw_ref[0]
    twi = tw_ref[1]

    def long_conv(ks_ref, finish, side_work):
        for c0 in groups:
            _row_stage(fr_ref, [ub_ref[c0 + q] for q in range(gc)], tw_ref, a_ref, c0)
        for c0 in groups:
            x = _lane_stage(a_ref, lf_ref, c0, gc)
            xr = x[:, :, :LANES]
            xi = x[:, :, LANES:]
            kr = ks_ref[pl.ds(c0, gc), :, :LANES]
            ki = ks_ref[pl.ds(c0, gc), :, LANES:]
            b_ref[pl.ds(c0, gc), :, :LANES] = (xr * kr - xi * ki).astype(BF16)
            b_ref[pl.ds(c0, gc), :, LANES:] = (xr * ki + xi * kr).astype(BF16)
        for work in side_work:
            work()
        for c0 in groups:
            qv = _lane_stage(b_ref, lb_ref, c0, gc)
            qr = qv[:, :, :LANES]
            qi = qv[:, :, LANES:]
            a_ref[pl.ds(c0, gc), :, :LANES] = (qr * twr + qi * twi).astype(BF16)
            a_ref[pl.ds(c0, gc), :, LANES:] = (qi * twr - qr * twi).astype(BF16)
        for c0 in groups:
            rhs = jnp.concatenate(
                [jnp.concatenate([a_ref[c0 + q, :, :LANES], a_ref[c0 + q, :, LANES:]], axis=0) for q in range(gc)], axis=1)
            y = jnp.dot(gi_ref[...], rhs, preferred_element_type=F32)
            for q in range(gc):
                finish(c0 + q, y[:, q * LANES:(q + 1) * LANES])

    def finish0(c, y):
        z = hx1_ref[c] * (y + hv_ref[c] * par_ref[c, 12:13, :])
        hx1_ref[c] = z
        ub_ref[c] = z.astype(BF16)

    long_conv(ks0_ref, finish0,
              [functools.partial(short_conv, 1, x1_ref, hx1_ref, c0) for c0 in range(0, cb, cc)])

    def finish1(c, y):
        o_ref[c] = hx2_ref[c] * (y + hx1_ref[c] * par_ref[c, 13:14, :])

    long_conv(ks1_ref, finish1,
              [functools.partial(short_conv, 2, x2_ref, hx2_ref, c0) for c0 in range(0, cb, cc)])


def _hyena(hyt, par, ks, tabs, *, Bt, L, d_hy):
    A = tabs["A"]
    cb = _pick(d_hy, (32, 16, 8))
    gc = 8
    ncb = d_hy // cb
    x3 = hyt.reshape(3 * d_hy, Bt * A, LANES)
    kern = functools.partial(_hyena_kernel, cb=cb, gc=gc, cc=4)
    const2 = lambda j, b: (0, 0)
    tile = (cb, A, LANES)
    return pl.pallas_call(
        kern,
        grid=(ncb, Bt),
        in_specs=[
            pl.BlockSpec(tile, lambda j, b: (j, b, 0)),
            pl.BlockSpec(tile, lambda j, b: (ncb + j, b, 0)),
            pl.BlockSpec(tile, lambda j, b: (2 * ncb + j, b, 0)),
            pl.BlockSpec((cb, 16, LANES), lambda j, b: (j, 0, 0)),
            pl.BlockSpec((cb, A, 2 * LANES), lambda j, b: (j, 0, 0)),
            pl.BlockSpec((cb, A, 2 * LANES), lambda j, b: (ncb + j, 0, 0)),
            pl.BlockSpec((2 * A, A), const2),
            pl.BlockSpec((2, A, LANES), lambda j, b: (0, 0, 0)),
            pl.BlockSpec((2 * LANES, 2 * LANES), const2),
            pl.BlockSpec((2 * LANES, 2 * LANES), const2),
            pl.BlockSpec((A, 2 * A), const2),
        ],
        out_specs=pl.BlockSpec(tile, lambda j, b: (j, b, 0)),
        out_shape=jax.ShapeDtypeStruct((d_hy, Bt * A, LANES), F32),
        scratch_shapes=[
            pltpu.VMEM(tile, F32),
            pltpu.VMEM(tile, F32),
            pltpu.VMEM(tile, F32),
            pltpu.VMEM(tile, BF16),
            pltpu.VMEM((cb, A, 2 * LANES), BF16),
            pltpu.VMEM((cb, A, 2 * LANES), BF16),
        ],
        compiler_params=_params(("parallel", "arbitrary")),
        name="hyena",
    )(x3, x3, x3, par, ks, ks, tabs["fr_half"], tabs["tw"], tabs["lf"], tabs["lb"], tabs["gi"])


def _mix_norm_kernel(rg_ref, hy_ref, grg_ref, ghy_ref, o_ref, *, d_rg):
    o_ref[:, :d_rg] = _rms(rg_ref[...], grg_ref[...]).astype(BF16)
    hy = hy_ref[...]
    scale = lax.rsqrt(jnp.mean(hy * hy, axis=0, keepdims=True) + EPS)
    o_ref[:, d_rg:] = (hy * scale * ghy_ref[:, 0:1]).T.astype(BF16)


def _mix_norm(rg, hyt, g_rg, g_hy_col):
    T, d_rg = rg.shape
    d_hy = hyt.shape[0]
    tm = _pick(T, (256, 128))
    kern = functools.partial(_mix_norm_kernel, d_rg=d_rg)
    return pl.pallas_call(
        kern,
        grid=(T // tm,),
        in_specs=[
            pl.BlockSpec((tm, d_rg), lambda i: (i, 0)),
            pl.BlockSpec((d_hy, tm), lambda i: (0, i)),
            pl.BlockSpec((1, d_rg), lambda i: (0, 0)),
            pl.BlockSpec((d_hy, LANES), lambda i: (0, 0)),
        ],
        out_specs=pl.BlockSpec((tm, d_rg + d_hy), lambda i: (i, 0)),
        out_shape=jax.ShapeDtypeStruct((T, d_rg + d_hy), BF16),
        compiler_params=_params(("parallel",)),
        name="mix_norm",
    )(rg, hyt, g_rg.reshape(1, d_rg), g_hy_col)


def _post_mix_kernel(xa_ref, xb_ref, f_ref, g1_ref, g2_ref, x1_ref, hn_ref, *, n_first):
    x = jnp.where(pl.program_id(0) < n_first, xa_ref[0], xb_ref[0])
    x1 = x + _rms(f_ref[0], g1_ref[...])
    x1_ref[0] = x1
    hn_ref[0] = _rms(x1, g2_ref[...]).astype(BF16)


def _post_mix(xs, f, g1, g2, *, L):
    D = xs[0].shape[-1]
    n_first = xs[0].shape[0]
    Bt = sum(x.shape[0] for x in xs)
    tm = _pick(L, (256, 128))
    nt = L // tm
    block = (1, tm, D)
    index = lambda b, i: (b, i, 0)
    if len(xs) == 2:
        spec_a, spec_b = _pair_specs(block, n_first, nt - 1, index)
    else:
        spec_a = spec_b = pl.BlockSpec(block, index)
        xs = list(xs) * 2
    row = pl.BlockSpec(block, index)
    vec = pl.BlockSpec((1, D), lambda b, i: (0, 0))
    x1, hn = pl.pallas_call(
        functools.partial(_post_mix_kernel, n_first=n_first),
        grid=(Bt, nt),
        in_specs=[spec_a, spec_b, row, vec, vec],
        out_specs=[row, row],
        out_shape=[jax.ShapeDtypeStruct((Bt, L, D), F32), jax.ShapeDtypeStruct((Bt, L, D), BF16)],
        compiler_params=_params(("parallel", "parallel")),
        name="post_mix",
    )(xs[0], xs[1], f.reshape(Bt, L, D), g1.reshape(1, D), g2.reshape(1, D))
    return x1.reshape(Bt * L, D), hn.reshape(Bt * L, D)


def _post_ffn_pair_kernel(x_ref, f_ref, g_ref, ya_ref, yb_ref, *, n_first):
    y = x_ref[0] + _rms(f_ref[0], g_ref[...])

    @pl.when(pl.program_id(0) < n_first)
    def _():
        ya_ref[0] = y

    @pl.when(pl.program_id(0) >= n_first)
    def _():
        yb_ref[0] = y


def _post_ffn_kernel(x_ref, f_ref, g_ref, y_ref):
    y_ref[0] = x_ref[0] + _rms(f_ref[0], g_ref[...])


def _post_ffn(x, f, g, *, L, splits):
    T, D = x.shape
    Bt = T // L
    tm = _pick(L, (256, 128))
    nt = L // tm
    block = (1, tm, D)
    index = lambda b, i: (b, i, 0)
    row = pl.BlockSpec(block, index)
    args = (x.reshape(Bt, L, D), f.reshape(Bt, L, D), g.reshape(1, D))
    in_specs = [row, row, pl.BlockSpec((1, D), lambda b, i: (0, 0))]
    if len(splits) == 1:
        return (pl.pallas_call(
            _post_ffn_kernel,
            grid=(Bt, nt),
            in_specs=in_specs,
            out_specs=row,
            out_shape=jax.ShapeDtypeStruct((Bt, L, D), F32),
            compiler_params=_params(("parallel", "parallel")),
            name="post_ffn",
        )(*args),)
    n_first = splits[0]
    spec_a, spec_b = _pair_specs(block, n_first, nt - 1, index)
    return tuple(pl.pallas_call(
        functools.partial(_post_ffn_pair_kernel, n_first=n_first),
        grid=(Bt, nt),
        in_specs=in_specs,
        out_specs=[spec_a, spec_b],
        out_shape=[jax.ShapeDtypeStruct((n, L, D), F32) for n in splits],
        compiler_params=_params(("arbitrary", "arbitrary")),
        name="post_ffn",
    )(*args))


FFN_LAG = 2


def _ffn_in_kernel(a_ref, wg_ref, wu_ref, cw_ref, cb_ref, o_ref, g_ref, u_ref, *, tm, rc, n_tiles, tiles_per_seq):
    s = pl.program_id(0)
    ring = FFN_LAG + 1

    @pl.when(s == 0)
    def _():
        g_ref[...] = jnp.zeros_like(g_ref)
        u_ref[...] = jnp.zeros_like(u_ref)

    new = s % ring
    mid = (s + 1) % ring
    nxt = (s + 2) % ring

    e = jnp.maximum(s - FFN_LAG, 0) % n_tiles
    first = (e % tiles_per_seq) == 0
    last = (e % tiles_per_seq) == tiles_per_seq - 1
    cw = cw_ref[...]
    cb = cb_ref[...]
    pad = SUBLANES
    for r0 in range(0, tm, rc):
        if r0 == 0:
            top = jnp.where(first, 0.0, g_ref[new, tm - pad:tm, :])
        else:
            top = g_ref[mid, r0 - pad:r0, :]
        if r0 + rc == tm:
            bottom = jnp.where(last, 0.0, g_ref[nxt, 0:pad, :])
        else:
            bottom = g_ref[mid, r0 + rc:r0 + rc + pad, :]
        big = jnp.concatenate([top, g_ref[mid, r0:r0 + rc, :], bottom], axis=0)
        y = (big[pad - 1:pad - 1 + rc] * cw[0:1, :] + big[pad:pad + rc] * cw[1:2, :]
             + big[pad + 1:pad + 1 + rc] * cw[2:3, :] + cb)
        o_ref[r0:r0 + rc, :] = (jax.nn.gelu(y) * u_ref[mid, r0:r0 + rc, :]).astype(BF16)

        a = a_ref[r0:r0 + rc, :]
        g_ref[new, r0:r0 + rc, :] = jnp.dot(a, wg_ref[...], preferred_element_type=F32)
        u_ref[new, r0:r0 + rc, :] = jnp.dot(a, wu_ref[...], preferred_element_type=F32)


def _ffn_in(a, w, cw, cb, *, L, d_ff):
    T, D = a.shape
    tm = _pick(L, (1024, 512, 256, 128))
    tn = _pick(d_ff, (256, 128))
    ncol = d_ff // tn
    n_tiles = T // tm
    steps = ncol * n_tiles

    def mm_tile(s):
        return jnp.minimum(s, steps - 1)

    def ew_tile(s):
        return jnp.maximum(s - FFN_LAG, 0)

    kern = functools.partial(_ffn_in_kernel, tm=tm, rc=_pick(tm, (256, 128)), n_tiles=n_tiles, tiles_per_seq=L // tm)
    return pl.pallas_call(
        kern,
        grid=(steps + FFN_LAG,),
        in_specs=[
            pl.BlockSpec((tm, D), lambda s: (mm_tile(s) % n_tiles, 0)),
            pl.BlockSpec((D, tn), lambda s: (0, mm_tile(s) // n_tiles)),
            pl.BlockSpec((D, tn), lambda s: (0, ncol + mm_tile(s) // n_tiles)),
            pl.BlockSpec((3, tn), lambda s: (0, ew_tile(s) // n_tiles)),
            pl.BlockSpec((1, tn), lambda s: (0, ew_tile(s) // n_tiles)),
        ],
        out_specs=pl.BlockSpec((tm, tn), lambda s: (ew_tile(s) % n_tiles, ew_tile(s) // n_tiles)),
        out_shape=jax.ShapeDtypeStruct((T, d_ff), BF16),
        scratch_shapes=[pltpu.VMEM((FFN_LAG + 1, tm, tn), F32), pltpu.VMEM((FFN_LAG + 1, tm, tn), F32)],
        compiler_params=_params(("arbitrary",)),
        name="ffn_in",
    )(a, w, w, cw, cb.reshape(1, d_ff))


def _lane_bcast(v):
    return jnp.broadcast_to(v[..., None], v.shape + (LANES,))


def _hyena_filter_spectrum(L, d_hy, n_order, w1, b1, w2, b2, w3, b3, w4, freq, tabs):
    emb = w1.shape[0]
    bands = (emb - 1) // 2
    n = jnp.arange(2 * L)
    m = jnp.where(n < L, n, 2 * L - n).astype(F32)
    sign = jnp.where(n < L, 1.0, jnp.where(n == L, 0.0, -1.0)).astype(F32)
    t = m / (L - 1)
    band = jnp.linspace(1e-4, bands - 1, bands, dtype=F32)
    ang = (2.0 * math.pi / L) * m[None, :] * band[:, None]
    zt = jnp.concatenate([t[None, :], jnp.cos(ang), -jnp.sin(ang)], axis=0)
    kpad = -(-emb // BF16_ROWS) * BF16_ROWS
    zt = jnp.pad(zt, ((0, kpad - emb), (0, 0)))
    aux = jnp.zeros((SUBLANES, 2 * L), F32).at[0].set(sign).at[1].set(t)
    w1t = jnp.pad(w1.T, ((0, 0), (0, kpad - emb))).astype(BF16)
    fw = w2.shape[0]
    cols = jnp.zeros((fw, LANES), F32)
    cols = cols.at[:, 0].set(freq).at[:, 1].set(b1).at[:, 2].set(b2).at[:, 3].set(b3)
    max_decay = math.log(HY_TARGET) / HY_FAST_DECAY
    min_decay = math.log(HY_TARGET) / HY_SLOW_DECAY
    delta = _lane_bcast(jnp.linspace(min_decay, max_decay, d_hy, dtype=F32))
    kt = _filter_time(zt, aux, w1t, w2.T.astype(BF16), w3.T.astype(BF16), cols, w4.T.astype(BF16), delta,
                      L=L, d_hy=d_hy, n_order=n_order)
    return _filter_spec(kt, tabs)


def _encoder_layer(xs, p, splits):
    (pre_mix_norm, w_in, rg_conv_w, rg_conv_b, rg_a_w, rg_a_b, rg_x_w, rg_x_b, rg_lambda, hy_conv_w, hy_conv_b,
     hy_w1, hy_b1, hy_w2, hy_b2, hy_w3, hy_b3, hy_w4, hy_sin_freq, hy_bias, rg_out_norm, hy_out_norm, w_out,
     post_mix_norm, pre_ffn_norm, w_ffn_in, ffn_conv_w, ffn_conv_b, w_ffn_out, post_ffn_norm) = p
    _, L, D = xs[0].shape
    Bt = sum(x.shape[0] for x in xs)
    T = Bt * L
    d_rg = rg_conv_w.shape[-1]
    n_order, d_hy = hy_bias.shape
    d_ff = ffn_conv_w.shape[-1]
    nh = rg_a_w.shape[1]
    assert d_rg == nh * LANES and n_order == 2 and L % (LANES * BF16_ROWS) == 0

    hn, hnt = _norm_in(xs, pre_mix_norm, L=L)
    w_hyt = w_in[:, 2 * d_rg:].T.astype(BF16)
    tm = _pick(T, (1024, 512, 256, 128))
    proj_rg = _mm_wcast(hn, w_in, n_cols=2 * d_rg, tm=tm, tn=_pick(2 * d_rg, (512, 256, 128)), name="proj_rg")
    hyt = _mm(w_hyt, hnt, tm=_pick(3 * d_hy, (1024, 512, 256, 128)), tn=tm, name="proj_hy")

    w_cat = jnp.concatenate([rg_a_w[0], rg_x_w[0], rg_a_w[1], rg_x_w[1]], axis=-1).astype(BF16)
    b_cat = jnp.concatenate([rg_a_b[0], rg_x_b[0], rg_a_b[1], rg_x_b[1]], axis=0)
    b_cat = b_cat.reshape(4, nh, LANES).transpose(1, 0, 2).reshape(nh, 1, 4 * LANES)
    rg_y = _rglru(proj_rg.reshape(Bt, L, 2 * d_rg), rg_conv_w, rg_conv_b.reshape(1, d_rg), w_cat, b_cat,
                  rg_lambda, d_rg=d_rg)

    tabs = _dft_tables(L)
    ks = _hyena_filter_spectrum(L, d_hy, n_order, hy_w1, hy_b1, hy_w2, hy_b2, hy_w3, hy_b3, hy_w4, hy_sin_freq, tabs)
    cw = hy_conv_w.reshape(3, 3, d_hy)
    par = jnp.concatenate([cw.transpose(1, 0, 2).reshape(9, d_hy), hy_conv_b.reshape(3, d_hy), hy_bias,
                           jnp.zeros((2, d_hy), F32)], axis=0)
    par = _lane_bcast(par.T)
    hy_y = _hyena(hyt, par, ks, tabs, Bt=Bt, L=L, d_hy=d_hy)

    mix = _mix_norm(rg_y.reshape(T, d_rg), hy_y.reshape(d_hy, T), rg_out_norm, _lane_bcast(hy_out_norm))
    f = _mm_wcast(mix, w_out, n_cols=D, tm=tm, tn=_pick(D, (512, 256, 128)), name="out_proj")
    x1, hn2 = _post_mix(xs, f, post_mix_norm, pre_ffn_norm, L=L)

    fg = _ffn_in(hn2, w_ffn_in.astype(BF16), ffn_conv_w, ffn_conv_b, L=L, d_ff=d_ff)
    f2 = _mm(fg, w_ffn_out.astype(BF16), tm=_pick(T, (512, 256, 128)), tn=_pick(D, (512, 256, 128)), name="ffn_out")
    return _post_ffn(x1, f2, post_ffn_norm, L=L, splits=splits)


def _run_trunk(xs, params):
    splits = tuple(x.shape[0] for x in xs)
    depth = params[0].shape[0]
    for l in range(depth):
        last = l == depth - 1
        xs = _encoder_layer(xs, [q[l] for q in params], splits if last else (sum(splits),))
    return xs


def kernel(x_prompt, x_sample, pre_mix_norm, w_in, rg_conv_w, rg_conv_b, rg_a_w, rg_a_b, rg_x_w, rg_x_b, rg_lambda, hy_conv_w, hy_conv_b, hy_w1, hy_b1, hy_w2, hy_b2, hy_w3, hy_b3, hy_w4, hy_sin_freq, hy_bias, rg_out_norm, hy_out_norm, w_out, post_mix_norm, pre_ffn_norm, w_ffn_in, ffn_conv_w, ffn_conv_b, w_ffn_out, post_ffn_norm):
    params = (pre_mix_norm, w_in, rg_conv_w, rg_conv_b, rg_a_w, rg_a_b, rg_x_w, rg_x_b, rg_lambda, hy_conv_w,
              hy_conv_b, hy_w1, hy_b1, hy_w2, hy_b2, hy_w3, hy_b3, hy_w4, hy_sin_freq, hy_bias, rg_out_norm,
              hy_out_norm, w_out, post_mix_norm, pre_ffn_norm, w_ffn_in, ffn_conv_w, ffn_conv_b, w_ffn_out,
              post_ffn_norm)
    if x_prompt.shape[1:] == x_sample.shape[1:]:
        return _run_trunk((x_prompt, x_sample), params)
    return (_run_trunk((x_prompt,), params)[0], _run_trunk((x_sample,), params)[0])
```

```python
import functools
import math

import numpy as np
import jax
import jax.numpy as jnp
from jax import lax
from jax.experimental import pallas as pl
from jax.experimental.pallas import tpu as pltpu

F32 = jnp.float32
BF16 = jnp.bfloat16

EPS = 1e-6
TINY = 1e-30
RG_C = 8.0
RG_CONV_LEFT = 2
HY_FAST_DECAY = 0.3
HY_SLOW_DECAY = 1.5
HY_TARGET = 1e-2

LANES = 128
SUBLANES = 8
BF16_ROWS = 16
VMEM_LIMIT = 56 * 1024 * 1024


def _pick(n, candidates):
    for c in candidates:
        if n % c == 0:
            return c
    raise ValueError(f"no tile in {candidates} divides {n}")


def _params(sem):
    return pltpu.CompilerParams(dimension_semantics=sem, vmem_limit_bytes=VMEM_LIMIT)


def _rms(x, g):
    return x * lax.rsqrt(jnp.mean(x * x, axis=-1, keepdims=True) + EPS) * g


def _pair_specs(block, n_first, inner_last, make_index):
    def first(o, i):
        return make_index(jnp.minimum(o, n_first - 1), jnp.where(o < n_first, i, inner_last))

    def second(o, i):
        return make_index(jnp.maximum(o - n_first, 0), jnp.where(o >= n_first, i, 0))

    return pl.BlockSpec(block, first), pl.BlockSpec(block, second)


def _norm_in_kernel(xa_ref, xb_ref, g_ref, hn_ref, hnt_ref, *, n_first):
    x = jnp.where(pl.program_id(0) < n_first, xa_ref[0], xb_ref[0])
    y = _rms(x, g_ref[...])
    hn_ref[0] = y.astype(BF16)
    hnt_ref[...] = y.T.astype(BF16)


def _norm_in(xs, g, *, L):
    D = xs[0].shape[-1]
    n_first = xs[0].shape[0]
    Bt = sum(x.shape[0] for x in xs)
    tm = _pick(L, (256, 128))
    nt = L // tm
    block = (1, tm, D)
    index = lambda b, i: (b, i, 0)
    if len(xs) == 2:
        spec_a, spec_b = _pair_specs(block, n_first, nt - 1, index)
    else:
        spec_a = spec_b = pl.BlockSpec(block, index)
        xs = list(xs) * 2
    hn, hnt = pl.pallas_call(
        functools.partial(_norm_in_kernel, n_first=n_first),
        grid=(Bt, nt),
        in_specs=[spec_a, spec_b, pl.BlockSpec((1, D), lambda b, i: (0, 0))],
        out_specs=[pl.BlockSpec(block, index), pl.BlockSpec((D, tm), lambda b, i: (0, b * nt + i))],
        out_shape=[jax.ShapeDtypeStruct((Bt, L, D), BF16), jax.ShapeDtypeStruct((D, Bt * L), BF16)],
        compiler_params=_params(("parallel", "parallel")),
        name="norm_in",
    )(xs[0], xs[1], g.reshape(1, D))
    return hn.reshape(Bt * L, D), hnt


def _mm_kernel(a_ref, b_ref, o_ref):
    o_ref[...] = jnp.dot(a_ref[...], b_ref[...], preferred_element_type=F32).astype(o_ref.dtype)


def _mm(a, b, *, tm, tn, name):
    M, K = a.shape
    _, N = b.shape
    return pl.pallas_call(
        _mm_kernel,
        grid=(N // tn, M // tm),
        in_specs=[pl.BlockSpec((tm, K), lambda j, i: (i, 0)), pl.BlockSpec((K, tn), lambda j, i: (0, j))],
        out_specs=pl.BlockSpec((tm, tn), lambda j, i: (i, j)),
        out_shape=jax.ShapeDtypeStruct((M, N), F32),
        compiler_params=_params(("parallel", "parallel")),
        name=name,
    )(a, b)


def _scan_chunk(a, b, carry, reverse):
    R = a.shape[0]
    nv = R // SUBLANES
    a3 = a.reshape(nv, SUBLANES, LANES)
    b3 = b.reshape(nv, SUBLANES, LANES)
    row = lax.broadcasted_iota(jnp.int32, (nv, SUBLANES, LANES), 1)
    for d in (1, 2, 4):
        shift = SUBLANES - d if reverse else d
        valid = (row < SUBLANES - d) if reverse else (row >= d)
        sa = pltpu.roll(a3, shift, axis=1)
        sb = pltpu.roll(b3, shift, axis=1)
        b3 = b3 + jnp.where(valid, a3, 0.0) * sb
        a3 = a3 * jnp.where(valid, sa, 1.0)
    edge = 0 if reverse else SUBLANES - 1
    a_edge = jnp.broadcast_to(a3[:, edge:edge + 1, :], a3.shape)
    b_edge = jnp.broadcast_to(b3[:, edge:edge + 1, :], b3.shape)
    order = range(nv - 1, -1, -1) if reverse else range(nv)
    carries = [None] * nv
    for j in order:
        carries[j] = carry
        carry = b_edge[j] + a_edge[j] * carry
    h3 = b3 + a3 * jnp.stack(carries)
    return h3.reshape(R, LANES), carry


def _rglru_kernel(x_ref, g_ref, cw_ref, cb_ref, w_ref, bias_ref, lam_ref, o_ref, xp_ref, xc_ref, hf_ref, *, L, R):
    pad = SUBLANES
    zeros = jnp.zeros((pad, LANES), F32)
    xp_ref[0:pad, :] = zeros
    xp_ref[L + pad:L + 2 * pad, :] = zeros
    xp_ref[pad:L + pad, :] = x_ref[0]
    cw = cw_ref[...]
    cb = cb_ref[...]
    lam = lam_ref[...]
    decay = RG_C * jax.nn.softplus(-lam)
    nc = L // R
    hw = 2 * LANES

    def gates(xc, d):
        z = jnp.dot(xc.astype(BF16), w_ref[0, :, d * hw:(d + 1) * hw], preferred_element_type=F32)
        z = z + bias_ref[0, :, d * hw:(d + 1) * hw]
        r = jax.nn.sigmoid(z[:, :LANES])
        i = jax.nn.sigmoid(z[:, LANES:])
        s = r * decay[d:d + 1, :]
        a = jnp.exp(-s)
        q = jnp.tanh(s) * (1.0 + a * a)
        b = (q * lax.rsqrt(jnp.maximum(q, TINY))) * (i * xc)
        return a, b

    def fwd(c, carry):
        base = pl.multiple_of(c * R, R)
        big = xp_ref[pl.ds(base, R + 2 * pad), :]
        xc = cb
        for k in range(cw.shape[0]):
            off = pad + k - RG_CONV_LEFT
            xc = xc + big[off:off + R, :] * cw[k:k + 1, :]
        xc_ref[pl.ds(base, R), :] = xc
        a, b = gates(xc, 0)
        h, carry = _scan_chunk(a, b, carry, False)
        hf_ref[pl.ds(base, R), :] = h
        return carry

    lax.fori_loop(0, nc, fwd, jnp.zeros((SUBLANES, LANES), F32))

    def bwd(s, carry):
        base = pl.multiple_of((nc - 1 - s) * R, R)
        xc = xc_ref[pl.ds(base, R), :]
        a, b = gates(xc, 1)
        h, carry = _scan_chunk(a, b, carry, True)
        gate = jax.nn.gelu(g_ref[0, pl.ds(base, R), :])
        o_ref[0, pl.ds(base, R), :] = (hf_ref[pl.ds(base, R), :] + h) * gate
        return carry

    lax.fori_loop(0, nc, bwd, jnp.zeros((SUBLANES, LANES), F32))


def _rglru(proj, cw, cb, w_cat, b_cat, lam, *, d_rg):
    Bt, L, _ = proj.shape
    nh = d_rg // LANES
    R = _pick(L, (512, 256, 128, 64, 32, 16, 8))
    kern = functools.partial(_rglru_kernel, L=L, R=R)
    return pl.pallas_call(
        kern,
        grid=(Bt, nh),
        in_specs=[
            pl.BlockSpec((1, L, LANES), lambda b, h: (b, 0, h)),
            pl.BlockSpec((1, L, LANES), lambda b, h: (b, 0, nh + h)),
            pl.BlockSpec((cw.shape[0], LANES), lambda b, h: (0, h)),
            pl.BlockSpec((1, LANES), lambda b, h: (0, h)),
            pl.BlockSpec((1, LANES, 4 * LANES), lambda b, h: (h, 0, 0)),
            pl.BlockSpec((1, 1, 4 * LANES), lambda b, h: (h, 0, 0)),
            pl.BlockSpec((2, LANES), lambda b, h: (0, h)),
        ],
        out_specs=pl.BlockSpec((1, L, LANES), lambda b, h: (b, 0, h)),
        out_shape=jax.ShapeDtypeStruct((Bt, L, d_rg), F32),
        scratch_shapes=[
            pltpu.VMEM((L + 2 * SUBLANES, LANES), F32),
            pltpu.VMEM((L, LANES), F32),
            pltpu.VMEM((L, LANES), F32),
        ],
        compiler_params=_params(("parallel", "parallel")),
        name="rglru",
    )(proj, proj, cw, cb, w_cat, b_cat, lam)


def _dft_tables(L):
    n = 2 * L
    A = L // LANES
    nb = np.arange(2 * A, dtype=np.float64)[None, :]
    kb = np.arange(A, dtype=np.float64)[:, None] + 0.5
    phi = 2.0 * np.pi * kb * nb / (2 * A)
    fr_full = np.concatenate([np.cos(phi), -np.sin(phi)], 0)
    na = np.arange(LANES, dtype=np.float64)[None, :]
    psi = 2.0 * np.pi * kb * na / n
    tw = np.stack([np.cos(psi), -np.sin(psi)])
    aa = 2.0 * np.pi * np.outer(np.arange(LANES), np.arange(LANES)) / LANES
    c, s = np.cos(aa), np.sin(aa)
    lf = np.block([[c, -s], [s, c]])
    lb = np.block([[c, s], [-s, c]])
    theta = phi[:, :A].T
    gi = np.concatenate([np.cos(theta), -np.sin(theta)], 1) * (2.0 / n)
    return dict(
        A=A,
        fr_full=jnp.asarray(fr_full, BF16), fr_half=jnp.asarray(fr_full[:, :A], BF16),
        tw=jnp.asarray(tw, F32), lf=jnp.asarray(lf, BF16), lb=jnp.asarray(lb, BF16), gi=jnp.asarray(gi, BF16),
    )


def _row_stage(fr_ref, tiles, tw_ref, dst_ref, c0):
    A = tw_ref.shape[1]
    y = jnp.dot(fr_ref[...], jnp.concatenate(tiles, axis=1), preferred_element_type=F32)
    twr = tw_ref[0]
    twi = tw_ref[1]
    for q in range(len(tiles)):
        yr = y[:A, q * LANES:(q + 1) * LANES]
        yi = y[A:, q * LANES:(q + 1) * LANES]
        dst_ref[c0 + q, :, :LANES] = (yr * twr - yi * twi).astype(BF16)
        dst_ref[c0 + q, :, LANES:] = (yr * twi + yi * twr).astype(BF16)


def _lane_stage(src_ref, m_ref, c0, gc):
    A = src_ref.shape[1]
    a2 = src_ref[pl.ds(c0, gc)].reshape(gc * A, 2 * LANES)
    return jnp.dot(a2, m_ref[...], preferred_element_type=F32).reshape(gc, A, 2 * LANES)


def _filter_time_kernel(zt_ref, aux_ref, w1_ref, w2_ref, w3_ref, col_ref, w4_ref, delta_ref, o_ref, h3_ref):
    @pl.when(pl.program_id(1) == 0)
    def _():
        fr = col_ref[:, 0:1]
        h = jnp.sin(fr * (jnp.dot(w1_ref[...], zt_ref[...].astype(BF16), preferred_element_type=F32) + col_ref[:, 1:2]))
        h = jnp.sin(fr * (jnp.dot(w2_ref[...], h.astype(BF16), preferred_element_type=F32) + col_ref[:, 2:3]))
        h = jnp.sin(fr * (jnp.dot(w3_ref[...], h.astype(BF16), preferred_element_type=F32) + col_ref[:, 3:4]))
        h3_ref[...] = h.astype(BF16)

    sign = aux_ref[0:1, :]
    t = aux_ref[1:2, :]
    k = jnp.dot(w4_ref[...], h3_ref[...], preferred_element_type=F32)
    o_ref[...] = k * jnp.exp(-t * jnp.abs(delta_ref[:, 0:1])) * sign


def _filter_time(zt, aux, w1t, w2t, w3t, cols, w4t, delta, *, L, d_hy, n_order):
    fw = w2t.shape[0]
    lt = _pick(L, (1024, 512, 256, 128))
    nt = 2 * L // lt
    tr = _pick(d_hy, (512, 256, 128))
    nr = d_hy // tr
    const = lambda j, r: (0, 0)

    def w4_map(j, r):
        direction = (j >= nt // 2).astype(jnp.int32)
        return (((r // nr) * 2 + direction) * nr + r % nr, 0)

    return pl.pallas_call(
        _filter_time_kernel,
        grid=(nt, n_order * nr),
        in_specs=[
            pl.BlockSpec((zt.shape[0], lt), lambda j, r: (0, j)),
            pl.BlockSpec((SUBLANES, lt), lambda j, r: (0, j)),
            pl.BlockSpec(w1t.shape, const),
            pl.BlockSpec(w2t.shape, const),
            pl.BlockSpec(w3t.shape, const),
            pl.BlockSpec(cols.shape, const),
            pl.BlockSpec((tr, fw), w4_map),
            pl.BlockSpec((tr, LANES), lambda j, r: (r % nr, 0)),
        ],
        out_specs=pl.BlockSpec((tr, lt), lambda j, r: (r, j)),
        out_shape=jax.ShapeDtypeStruct((n_order * d_hy, 2 * L), F32),
        scratch_shapes=[pltpu.VMEM((fw, lt), BF16)],
        compiler_params=_params(("arbitrary", "arbitrary")),
        name="hy_filter_time",
    )(zt, aux, w1t, w2t, w3t, cols, w4t, delta)


def _filter_spec_kernel(k_ref, fr_ref, tw_ref, lf_ref, o_ref, a_ref, *, cb, gc):
    for c0 in range(0, cb, gc):
        _row_stage(fr_ref, [k_ref[c0 + q].astype(BF16) for q in range(gc)], tw_ref, a_ref, c0)
    for c0 in range(0, cb, gc):
        o_ref[pl.ds(c0, gc)] = _lane_stage(a_ref, lf_ref, c0, gc)


def _filter_spec(kt, tabs):
    C = kt.shape[0]
    A = tabs["A"]
    cb = _pick(C, (32, 16, 8))
    gc = 8
    kern = functools.partial(_filter_spec_kernel, cb=cb, gc=gc)
    return pl.pallas_call(
        kern,
        grid=(C // cb,),
        in_specs=[
            pl.BlockSpec((cb, 2 * A, LANES), lambda i: (i, 0, 0)),
            pl.BlockSpec((2 * A, 2 * A), lambda i: (0, 0)),
            pl.BlockSpec((2, A, LANES), lambda i: (0, 0, 0)),
            pl.BlockSpec((2 * LANES, 2 * LANES), lambda i: (0, 0)),
        ],
        out_specs=pl.BlockSpec((cb, A, 2 * LANES), lambda i: (i, 0, 0)),
        out_shape=jax.ShapeDtypeStruct((C, A, 2 * LANES), F32),
        scratch_shapes=[pltpu.VMEM((cb, A, 2 * LANES), BF16)],
        compiler_params=_params(("parallel",)),
        name="hy_filter_spec",
    )(kt.reshape(C, 2 * A, LANES), tabs["fr_full"], tabs["tw"], tabs["lf"])


def _shift_time(x, step):
    A = x.shape[1]
    lane = lax.broadcasted_iota(jnp.int32, x.shape, 2)
    row = lax.broadcasted_iota(jnp.int32, x.shape, 1)
    if step == 1:
        r = pltpu.roll(x, 1, axis=2)
        y = jnp.where(lane == 0, pltpu.roll(r, 1, axis=1), r)
        return jnp.where((lane == 0) & (row == 0), 0.0, y)
    r = pltpu.roll(x, LANES - 1, axis=2)
    y = jnp.where(lane == LANES - 1, pltpu.roll(r, A - 1, axis=1), r)
    return jnp.where((lane == LANES - 1) & (row == A - 1), 0.0, y)


def _hyena_kernel(v_ref, x1_ref, x2_ref, par_ref, ks0_ref, ks1_ref, fr_ref, tw_ref, lf_ref, lb_ref, gi_ref, o_ref,
                  hv_ref, hx1_ref, hx2_ref, ub_ref, a_ref, b_ref, *, cb, gc, cc):
    A = tw_ref.shape[1]
    groups = [g * gc for g in range(cb // gc)]

    def short_conv(s, src, dst, c0):
        par = par_ref[pl.ds(c0, cc)]
        x = src[pl.ds(c0, cc)]
        y = (_shift_time(x, 1) * par[:, 3 * s:3 * s + 1, :] + x * par[:, 3 * s + 1:3 * s + 2, :]
             + _shift_time(x, -1) * par[:, 3 * s + 2:3 * s + 3, :] + par[:, 9 + s:10 + s, :])
        dst[pl.ds(c0, cc)] = y
        return y

    for c0 in range(0, cb, cc):
        ub_ref[pl.ds(c0, cc)] = short_conv(0, v_ref, hv_ref, c0).astype(BF16)

    twr = tw_ref[0]
    twi = tw_ref[1]

    def long_conv(ks_ref, finish, side_work):
        for c0 in groups:
            _row_stage(fr_ref, [ub_ref[c0 + q] for q in range(gc)], tw_ref, a_ref, c0)
        for c0 in groups:
            x = _lane_stage(a_ref, lf_ref, c0, gc)
            xr = x[:, :, :LANES]
            xi = x[:, :, LANES:]
            kr = ks_ref[pl.ds(c0, gc), :, :LANES]
            ki = ks_ref[pl.ds(c0, gc), :, LANES:]
            b_ref[pl.ds(c0, gc), :, :LANES] = (xr * kr - xi * ki).astype(BF16)
            b_ref[pl.ds(c0, gc), :, LANES:] = (xr * ki + xi * kr).astype(BF16)
        for work in side_work:
            work()
        for c0 in groups:
            qv = _lane_stage(b_ref, lb_ref, c0, gc)
            qr = qv[:, :, :LANES]
            qi = qv[:, :, LANES:]
            a_ref[pl.ds(c0, gc), :, :LANES] = (qr * twr + qi * twi).astype(BF16)
            a_ref[pl.ds(c0, gc), :, LANES:] = (qi * twr - qr * twi).astype(BF16)
        for c0 in groups:
            rhs = jnp.concatenate(
                [jnp.concatenate([a_ref[c0 + q, :, :LANES], a_ref[c0 + q, :, LANES:]], axis=0) for q in range(gc)], axis=1)
            y = jnp.dot(gi_ref[...], rhs, preferred_element_type=F32)
            for q in range(gc):
                finish(c0 + q, y[:, q * LANES:(q + 1) * LANES])

    def finish0(c, y):
        z = hx1_ref[c] * (y + hv_ref[c] * par_ref[c, 12:13, :])
        hx1_ref[c] = z
        ub_ref[c] = z.astype(BF16)

    long_conv(ks0_ref, finish0,
              [functools.partial(short_conv, 1, x1_ref, hx1_ref, c0) for c0 in range(0, cb, cc)])

    def finish1(c, y):
        o_ref[c] = hx2_ref[c] * (y + hx1_ref[c] * par_ref[c, 13:14, :])

    long_conv(ks1_ref, finish1,
              [functools.partial(short_conv, 2, x2_ref, hx2_ref, c0) for c0 in range(0, cb, cc)])


def _hyena(hyt, par, ks, tabs, *, Bt, L, d_hy):
    A = tabs["A"]
    cb = _pick(d_hy, (32, 16, 8))
    gc = 8
    ncb = d_hy // cb
    x3 = hyt.reshape(3 * d_hy, Bt * A, LANES)
    kern = functools.partial(_hyena_kernel, cb=cb, gc=gc, cc=4)
    const2 = lambda j, b: (0, 0)
    tile = (cb, A, LANES)
    return pl.pallas_call(
        kern,
        grid=(ncb, Bt),
        in_specs=[
            pl.BlockSpec(tile, lambda j, b: (j, b, 0)),
            pl.BlockSpec(tile, lambda j, b: (ncb + j, b, 0)),
            pl.BlockSpec(tile, lambda j, b: (2 * ncb + j, b, 0)),
            pl.BlockSpec((cb, 16, LANES), lambda j, b: (j, 0, 0)),
            pl.BlockSpec((cb, A, 2 * LANES), lambda j, b: (j, 0, 0)),
            pl.BlockSpec((cb, A, 2 * LANES), lambda j, b: (ncb + j, 0, 0)),
            pl.BlockSpec((2 * A, A), const2),
            pl.BlockSpec((2, A, LANES), lambda j, b: (0, 0, 0)),
            pl.BlockSpec((2 * LANES, 2 * LANES), const2),
            pl.BlockSpec((2 * LANES, 2 * LANES), const2),
            pl.BlockSpec((A, 2 * A), const2),
        ],
        out_specs=pl.BlockSpec(tile, lambda j, b: (j, b, 0)),
        out_shape=jax.ShapeDtypeStruct((d_hy, Bt * A, LANES), F32),
        scratch_shapes=[
            pltpu.VMEM(tile, F32),
            pltpu.VMEM(tile, F32),
            pltpu.VMEM(tile, F32),
            pltpu.VMEM(tile, BF16),
            pltpu.VMEM((cb, A, 2 * LANES), BF16),
            pltpu.VMEM((cb, A, 2 * LANES), BF16),
        ],
        compiler_params=_params(("parallel", "arbitrary")),
        name="hyena",
    )(x3, x3, x3, par, ks, ks, tabs["fr_half"], tabs["tw"], tabs["lf"], tabs["lb"], tabs["gi"])


def _mix_norm_kernel(rg_ref, hy_ref, grg_ref, ghy_ref, o_ref, *, d_rg):
    o_ref[:, :d_rg] = _rms(rg_ref[...], grg_ref[...]).astype(BF16)
    hy = hy_ref[...]
    scale = lax.rsqrt(jnp.mean(hy * hy, axis=0, keepdims=True) + EPS)
    o_ref[:, d_rg:] = (hy * scale * ghy_ref[:, 0:1]).T.astype(BF16)


def _mix_norm(rg, hyt, g_rg, g_hy_col):
    T, d_rg = rg.shape
    d_hy = hyt.shape[0]
    tm = _pick(T, (256, 128))
    kern = functools.partial(_mix_norm_kernel, d_rg=d_rg)
    return pl.pallas_call(
        kern,
        grid=(T // tm,),
        in_specs=[
            pl.BlockSpec((tm, d_rg), lambda i: (i, 0)),
            pl.BlockSpec((d_hy, tm), lambda i: (0, i)),
            pl.BlockSpec((1, d_rg), lambda i: (0, 0)),
            pl.BlockSpec((d_hy, LANES), lambda i: (0, 0)),
        ],
        out_specs=pl.BlockSpec((tm, d_rg + d_hy), lambda i: (i, 0)),
        out_shape=jax.ShapeDtypeStruct((T, d_rg + d_hy), BF16),
        compiler_params=_params(("parallel",)),
        name="mix_norm",
    )(rg, hyt, g_rg.reshape(1, d_rg), g_hy_col)


def _post_mix_kernel(xa_ref, xb_ref, f_ref, g1_ref, g2_ref, x1_ref, hn_ref, *, n_first):
    x = jnp.where(pl.program_id(0) < n_first, xa_ref[0], xb_ref[0])
    x1 = x + _rms(f_ref[0], g1_ref[...])
    x1_ref[0] = x1
    hn_ref[0] = _rms(x1, g2_ref[...]).astype(BF16)


def _post_mix(xs, f, g1, g2, *, L):
    D = xs[0].shape[-1]
    n_first = xs[0].shape[0]
    Bt = sum(x.shape[0] for x in xs)
    tm = _pick(L, (256, 128))
    nt = L // tm
    block = (1, tm, D)
    index = lambda b, i: (b, i, 0)
    if len(xs) == 2:
        spec_a, spec_b = _pair_specs(block, n_first, nt - 1, index)
    else:
        spec_a = spec_b = pl.BlockSpec(block, index)
        xs = list(xs) * 2
    row = pl.BlockSpec(block, index)
    vec = pl.BlockSpec((1, D), lambda b, i: (0, 0))
    x1, hn = pl.pallas_call(
        functools.partial(_post_mix_kernel, n_first=n_first),
        grid=(Bt, nt),
        in_specs=[spec_a, spec_b, row, vec, vec],
        out_specs=[row, row],
        out_shape=[jax.ShapeDtypeStruct((Bt, L, D), F32), jax.ShapeDtypeStruct((Bt, L, D), BF16)],
        compiler_params=_params(("parallel", "parallel")),
        name="post_mix",
    )(xs[0], xs[1], f.reshape(Bt, L, D), g1.reshape(1, D), g2.reshape(1, D))
    return x1.reshape(Bt * L, D), hn.reshape(Bt * L, D)


def _post_ffn_pair_kernel(x_ref, f_ref, g_ref, ya_ref, yb_ref, *, n_first):
    y = x_ref[0] + _rms(f_ref[0], g_ref[...])

    @pl.when(pl.program_id(0) < n_first)
    def _():
        ya_ref[0] = y

    @pl.when(pl.program_id(0) >= n_first)
    def _():
        yb_ref[0] = y


def _post_ffn_kernel(x_ref, f_ref, g_ref, y_ref):
    y_ref[0] = x_ref[0] + _rms(f_ref[0], g_ref[...])


def _post_ffn(x, f, g, *, L, splits):
    T, D = x.shape
    Bt = T // L
    tm = _pick(L, (256, 128))
    nt = L // tm
    block = (1, tm, D)
    index = lambda b, i: (b, i, 0)
    row = pl.BlockSpec(block, index)
    args = (x.reshape(Bt, L, D), f.reshape(Bt, L, D), g.reshape(1, D))
    in_specs = [row, row, pl.BlockSpec((1, D), lambda b, i: (0, 0))]
    if len(splits) == 1:
        return (pl.pallas_call(
            _post_ffn_kernel,
            grid=(Bt, nt),
            in_specs=in_specs,
            out_specs=row,
            out_shape=jax.ShapeDtypeStruct((Bt, L, D), F32),
            compiler_params=_params(("parallel", "parallel")),
            name="post_ffn",
        )(*args),)
    n_first = splits[0]
    spec_a, spec_b = _pair_specs(block, n_first, nt - 1, index)
    return tuple(pl.pallas_call(
        functools.partial(_post_ffn_pair_kernel, n_first=n_first),
        grid=(Bt, nt),
        in_specs=in_specs,
        out_specs=[spec_a, spec_b],
        out_shape=[jax.ShapeDtypeStruct((n, L, D), F32) for n in splits],
        compiler_params=_params(("arbitrary", "arbitrary")),
        name="post_ffn",
    )(*args))


FFN_LAG = 2


def _ffn_in_kernel(a_ref, wg_ref, wu_ref, cw_ref, cb_ref, o_ref, g_ref, u_ref, *, tm, rc, n_tiles, tiles_per_seq):
    s = pl.program_id(0)
    ring = FFN_LAG + 1

    @pl.when(s == 0)
    def _():
        g_ref[...] = jnp.zeros_like(g_ref)
        u_ref[...] = jnp.zeros_like(u_ref)

    new = s % ring
    mid = (s + 1) % ring
    nxt = (s + 2) % ring

    e = jnp.maximum(s - FFN_LAG, 0) % n_tiles
    first = (e % tiles_per_seq) == 0
    last = (e % tiles_per_seq) == tiles_per_seq - 1
    cw = cw_ref[...]
    cb = cb_ref[...]
    pad = SUBLANES
    for r0 in range(0, tm, rc):
        a = a_ref[r0:r0 + rc, :]
        g_new = jnp.dot(a, wg_ref[...], preferred_element_type=F32)
        u_new = jnp.dot(a, wu_ref[...], preferred_element_type=F32)

        if r0 == 0:
            top = jnp.where(first, 0.0, g_ref[new, tm - pad:tm, :])
        else:
            top = g_ref[mid, r0 - pad:r0, :]
        if r0 + rc == tm:
            bottom = jnp.where(last, 0.0, g_ref[nxt, 0:pad, :])
        else:
            bottom = g_ref[mid, r0 + rc:r0 + rc + pad, :]
        big = jnp.concatenate([top, g_ref[mid, r0:r0 + rc, :], bottom], axis=0)
        y = (big[pad - 1:pad - 1 + rc] * cw[0:1, :] + big[pad:pad + rc] * cw[1:2, :]
             + big[pad + 1:pad + 1 + rc] * cw[2:3, :] + cb)
        o_ref[r0:r0 + rc, :] = (jax.nn.gelu(y) * u_ref[mid, r0:r0 + rc, :]).astype(BF16)

        g_ref[new, r0:r0 + rc, :] = g_new
        u_ref[new, r0:r0 + rc, :] = u_new


def _ffn_in(a, w, cw, cb, *, L, d_ff):
    T, D = a.shape
    tm = _pick(L, (1024, 512, 256, 128))
    tn = _pick(d_ff, (256, 128))
    ncol = d_ff // tn
    n_tiles = T // tm
    steps = ncol * n_tiles

    def mm_tile(s):
        return jnp.minimum(s, steps - 1)

    def ew_tile(s):
        return jnp.maximum(s - FFN_LAG, 0)

    kern = functools.partial(_ffn_in_kernel, tm=tm, rc=_pick(tm, (256, 128)), n_tiles=n_tiles, tiles_per_seq=L // tm)
    return pl.pallas_call(
        kern,
        grid=(steps + FFN_LAG,),
        in_specs=[
            pl.BlockSpec((tm, D), lambda s: (mm_tile(s) % n_tiles, 0)),
            pl.BlockSpec((D, tn), lambda s: (0, mm_tile(s) // n_tiles)),
            pl.BlockSpec((D, tn), lambda s: (0, ncol + mm_tile(s) // n_tiles)),
            pl.BlockSpec((3, tn), lambda s: (0, ew_tile(s) // n_tiles)),
            pl.BlockSpec((1, tn), lambda s: (0, ew_tile(s) // n_tiles)),
        ],
        out_specs=pl.BlockSpec((tm, tn), lambda s: (ew_tile(s) % n_tiles, ew_tile(s) // n_tiles)),
        out_shape=jax.ShapeDtypeStruct((T, d_ff), BF16),
        scratch_shapes=[pltpu.VMEM((FFN_LAG + 1, tm, tn), F32), pltpu.VMEM((FFN_LAG + 1, tm, tn), F32)],
        compiler_params=_params(("arbitrary",)),
        name="ffn_in",
    )(a, w, w, cw, cb.reshape(1, d_ff))


def _lane_bcast(v):
    return jnp.broadcast_to(v[..., None], v.shape + (LANES,))


def _hyena_filter_spectrum(L, d_hy, n_order, w1, b1, w2, b2, w3, b3, w4, freq, tabs):
    emb = w1.shape[0]
    bands = (emb - 1) // 2
    n = jnp.arange(2 * L)
    m = jnp.where(n < L, n, 2 * L - n).astype(F32)
    sign = jnp.where(n < L, 1.0, jnp.where(n == L, 0.0, -1.0)).astype(F32)
    t = m / (L - 1)
    band = jnp.linspace(1e-4, bands - 1, bands, dtype=F32)
    ang = (2.0 * math.pi / L) * m[None, :] * band[:, None]
    zt = jnp.concatenate([t[None, :], jnp.cos(ang), -jnp.sin(ang)], axis=0)
    kpad = -(-emb // BF16_ROWS) * BF16_ROWS
    zt = jnp.pad(zt, ((0, kpad - emb), (0, 0)))
    aux = jnp.zeros((SUBLANES, 2 * L), F32).at[0].set(sign).at[1].set(t)
    w1t = jnp.pad(w1.T, ((0, 0), (0, kpad - emb))).astype(BF16)
    fw = w2.shape[0]
    cols = jnp.zeros((fw, LANES), F32)
    cols = cols.at[:, 0].set(freq).at[:, 1].set(b1).at[:, 2].set(b2).at[:, 3].set(b3)
    max_decay = math.log(HY_TARGET) / HY_FAST_DECAY
    min_decay = math.log(HY_TARGET) / HY_SLOW_DECAY
    delta = _lane_bcast(jnp.linspace(min_decay, max_decay, d_hy, dtype=F32))
    kt = _filter_time(zt, aux, w1t, w2.T.astype(BF16), w3.T.astype(BF16), cols, w4.T.astype(BF16), delta,
                      L=L, d_hy=d_hy, n_order=n_order)
    return _filter_spec(kt, tabs)


def _encoder_layer(xs, p, splits):
    (pre_mix_norm, w_in, rg_conv_w, rg_conv_b, rg_a_w, rg_a_b, rg_x_w, rg_x_b, rg_lambda, hy_conv_w, hy_conv_b,
     hy_w1, hy_b1, hy_w2, hy_b2, hy_w3, hy_b3, hy_w4, hy_sin_freq, hy_bias, rg_out_norm, hy_out_norm, w_out,
     post_mix_norm, pre_ffn_norm, w_ffn_in, ffn_conv_w, ffn_conv_b, w_ffn_out, post_ffn_norm) = p
    _, L, D = xs[0].shape
    Bt = sum(x.shape[0] for x in xs)
    T = Bt * L
    d_rg = rg_conv_w.shape[-1]
    n_order, d_hy = hy_bias.shape
    d_ff = ffn_conv_w.shape[-1]
    nh = rg_a_w.shape[1]
    assert d_rg == nh * LANES and n_order == 2 and L % (LANES * BF16_ROWS) == 0

    hn, hnt = _norm_in(xs, pre_mix_norm, L=L)
    w_hyt = w_in[:, 2 * d_rg:].T.astype(BF16)
    tm = _pick(T, (1024, 512, 256, 128))
    proj_rg = _mm(hn, w_in[:, :2 * d_rg].astype(BF16), tm=tm, tn=_pick(2 * d_rg, (1024, 512, 256, 128)), name="proj_rg")
    hyt = _mm(w_hyt, hnt, tm=_pick(3 * d_hy, (1024, 512, 256, 128)), tn=tm, name="proj_hy")

    w_cat = jnp.concatenate([rg_a_w[0], rg_x_w[0], rg_a_w[1], rg_x_w[1]], axis=-1).astype(BF16)
    b_cat = jnp.concatenate([rg_a_b[0], rg_x_b[0], rg_a_b[1], rg_x_b[1]], axis=0)
    b_cat = b_cat.reshape(4, nh, LANES).transpose(1, 0, 2).reshape(nh, 1, 4 * LANES)
    rg_y = _rglru(proj_rg.reshape(Bt, L, 2 * d_rg), rg_conv_w, rg_conv_b.reshape(1, d_rg), w_cat, b_cat,
                  rg_lambda, d_rg=d_rg)

    tabs = _dft_tables(L)
    ks = _hyena_filter_spectrum(L, d_hy, n_order, hy_w1, hy_b1, hy_w2, hy_b2, hy_w3, hy_b3, hy_w4, hy_sin_freq, tabs)
    cw = hy_conv_w.reshape(3, 3, d_hy)
    par = jnp.concatenate([cw.transpose(1, 0, 2).reshape(9, d_hy), hy_conv_b.reshape(3, d_hy), hy_bias,
                           jnp.zeros((2, d_hy), F32)], axis=0)
    par = _lane_bcast(par.T)
    hy_y = _hyena(hyt, par, ks, tabs, Bt=Bt, L=L, d_hy=d_hy)

    mix = _mix_norm(rg_y.reshape(T, d_rg), hy_y.reshape(d_hy, T), rg_out_norm, _lane_bcast(hy_out_norm))
    f = _mm(mix, w_out.astype(BF16), tm=tm, tn=_pick(D, (1024, 512, 256, 128)), name="out_proj")
    x1, hn2 = _post_mix(xs, f, post_mix_norm, pre_ffn_norm, L=L)

    fg = _ffn_in(hn2, w_ffn_in.astype(BF16), ffn_conv_w, ffn_conv_b, L=L, d_ff=d_ff)
    f2 = _mm(fg, w_ffn_out.astype(BF16), tm=_pick(T, (512, 256, 128)), tn=_pick(D, (512, 256, 128)), name="ffn_out")
    return _post_ffn(x1, f2, post_ffn_norm, L=L, splits=splits)


def _run_trunk(xs, params):
    splits = tuple(x.shape[0] for x in xs)
    depth = params[0].shape[0]
    for l in range(depth):
        last = l == depth - 1
        xs = _encoder_layer(xs, [q[l] for q in params], splits if last else (sum(splits),))
    return xs


def kernel(x_prompt, x_sample, pre_mix_norm, w_in, rg_conv_w, rg_conv_b, rg_a_w, rg_a_b, rg_x_w, rg_x_b, rg_lambda, hy_conv_w, hy_conv_b, hy_w1, hy_b1, hy_w2, hy_b2, hy_w3, hy_b3, hy_w4, hy_sin_freq, hy_bias, rg_out_norm, hy_out_norm, w_out, post_mix_norm, pre_ffn_norm, w_ffn_in, ffn_conv_w, ffn_conv_b, w_ffn_out, post_ffn_norm):
    params = (pre_mix_norm, w_in, rg_conv_w, rg_conv_b, rg_a_w, rg_a_b, rg_x_w, rg_x_b, rg_lambda, hy_conv_w,
              hy_conv_b, hy_w1, hy_b1, hy_w2, hy_b2, hy_w3, hy_b3, hy_w4, hy_sin_freq, hy_bias, rg_out_norm,
              hy_out_norm, w_out, post_mix_norm, pre_ffn_norm, w_ffn_in, ffn_conv_w, ffn_conv_b, w_ffn_out,
              post_ffn_norm)
    if x_prompt.shape[1:] == x_sample.shape[1:]:
        return _run_trunk((x_prompt, x_sample), params)
    return (_run_trunk((x_prompt,), params)[0], _run_trunk((x_sample,), params)[0])
```

```python
import functools
import math

import numpy as np
import jax
import jax.numpy as jnp
from jax import lax
from jax.experimental import pallas as pl
from jax.experimental.pallas import tpu as pltpu

F32 = jnp.float32
BF16 = jnp.bfloat16

EPS = 1e-6
TINY = 1e-30
RG_C = 8.0
RG_CONV_LEFT = 2
HY_FAST_DECAY = 0.3
HY_SLOW_DECAY = 1.5
HY_TARGET = 1e-2

LANES = 128
SUBLANES = 8
BF16_ROWS = 16
VMEM_LIMIT = 56 * 1024 * 1024


def _pick(n, candidates):
    for c in candidates:
        if n % c == 0:
            return c
    raise ValueError(f"no tile in {candidates} divides {n}")


def _params(sem):
    return pltpu.CompilerParams(dimension_semantics=sem, vmem_limit_bytes=VMEM_LIMIT)


def _rms(x, g):
    return x * lax.rsqrt(jnp.mean(x * x, axis=-1, keepdims=True) + EPS) * g


def _pair_specs(block, n_first, inner_last, make_index):
    def first(o, i):
        return make_index(jnp.minimum(o, n_first - 1), jnp.where(o < n_first, i, inner_last))

    def second(o, i):
        return make_index(jnp.maximum(o - n_first, 0), jnp.where(o >= n_first, i, 0))

    return pl.BlockSpec(block, first), pl.BlockSpec(block, second)


def _norm_in_kernel(xa_ref, xb_ref, g_ref, hn_ref, hnt_ref, *, n_first):
    x = jnp.where(pl.program_id(0) < n_first, xa_ref[0], xb_ref[0])
    y = _rms(x, g_ref[...])
    hn_ref[0] = y.astype(BF16)
    hnt_ref[...] = y.T.astype(BF16)


def _norm_in(xs, g, *, L):
    D = xs[0].shape[-1]
    n_first = xs[0].shape[0]
    Bt = sum(x.shape[0] for x in xs)
    tm = _pick(L, (256, 128))
    nt = L // tm
    block = (1, tm, D)
    index = lambda b, i: (b, i, 0)
    if len(xs) == 2:
        spec_a, spec_b = _pair_specs(block, n_first, nt - 1, index)
    else:
        spec_a = spec_b = pl.BlockSpec(block, index)
        xs = list(xs) * 2
    hn, hnt = pl.pallas_call(
        functools.partial(_norm_in_kernel, n_first=n_first),
        grid=(Bt, nt),
        in_specs=[spec_a, spec_b, pl.BlockSpec((1, D), lambda b, i: (0, 0))],
        out_specs=[pl.BlockSpec(block, index), pl.BlockSpec((D, tm), lambda b, i: (0, b * nt + i))],
        out_shape=[jax.ShapeDtypeStruct((Bt, L, D), BF16), jax.ShapeDtypeStruct((D, Bt * L), BF16)],
        compiler_params=_params(("parallel", "parallel")),
        name="norm_in",
    )(xs[0], xs[1], g.reshape(1, D))
    return hn.reshape(Bt * L, D), hnt


def _mm_kernel(a_ref, b_ref, o_ref):
    o_ref[...] = jnp.dot(a_ref[...], b_ref[...], preferred_element_type=F32).astype(o_ref.dtype)


def _mm(a, b, *, tm, tn, name):
    M, K = a.shape
    _, N = b.shape
    return pl.pallas_call(
        _mm_kernel,
        grid=(N // tn, M // tm),
        in_specs=[pl.BlockSpec((tm, K), lambda j, i: (i, 0)), pl.BlockSpec((K, tn), lambda j, i: (0, j))],
        out_specs=pl.BlockSpec((tm, tn), lambda j, i: (i, j)),
        out_shape=jax.ShapeDtypeStruct((M, N), F32),
        compiler_params=_params(("parallel", "parallel")),
        name=name,
    )(a, b)


def _transpose_cast_kernel(w_ref, o_ref):
    o_ref[...] = w_ref[...].T.astype(BF16)


def _transpose_cast(w, *, col0, ncols):
    K = w.shape[0]
    tc = _pick(ncols, (256, 128))
    assert col0 % tc == 0
    return pl.pallas_call(
        _transpose_cast_kernel,
        grid=(ncols // tc,),
        in_specs=[pl.BlockSpec((K, tc), lambda j: (0, col0 // tc + j))],
        out_specs=pl.BlockSpec((tc, K), lambda j: (j, 0)),
        out_shape=jax.ShapeDtypeStruct((ncols, K), BF16),
        compiler_params=_params(("parallel",)),
        name="w_transpose",
    )(w)


def _scan_chunk(a, b, carry, reverse):
    R = a.shape[0]
    nv = R // SUBLANES
    a3 = a.reshape(nv, SUBLANES, LANES)
    b3 = b.reshape(nv, SUBLANES, LANES)
    row = lax.broadcasted_iota(jnp.int32, (nv, SUBLANES, LANES), 1)
    for d in (1, 2, 4):
        shift = SUBLANES - d if reverse else d
        valid = (row < SUBLANES - d) if reverse else (row >= d)
        sa = pltpu.roll(a3, shift, axis=1)
        sb = pltpu.roll(b3, shift, axis=1)
        b3 = b3 + jnp.where(valid, a3, 0.0) * sb
        a3 = a3 * jnp.where(valid, sa, 1.0)
    edge = 0 if reverse else SUBLANES - 1
    a_edge = jnp.broadcast_to(a3[:, edge:edge + 1, :], a3.shape)
    b_edge = jnp.broadcast_to(b3[:, edge:edge + 1, :], b3.shape)
    order = range(nv - 1, -1, -1) if reverse else range(nv)
    carries = [None] * nv
    for j in order:
        carries[j] = carry
        carry = b_edge[j] + a_edge[j] * carry
    h3 = b3 + a3 * jnp.stack(carries)
    return h3.reshape(R, LANES), carry


def _rglru_kernel(x_ref, g_ref, cw_ref, cb_ref, w_ref, bias_ref, lam_ref, o_ref, xp_ref, xc_ref, hf_ref, *, L, R):
    pad = SUBLANES
    zeros = jnp.zeros((pad, LANES), F32)
    xp_ref[0:pad, :] = zeros
    xp_ref[L + pad:L + 2 * pad, :] = zeros
    xp_ref[pad:L + pad, :] = x_ref[0]
    cw = cw_ref[...]
    cb = cb_ref[...]
    lam = lam_ref[...]
    decay = RG_C * jax.nn.softplus(-lam)
    nc = L // R
    hw = 2 * LANES

    def gates(xc, d):
        z = jnp.dot(xc.astype(BF16), w_ref[0, :, d * hw:(d + 1) * hw], preferred_element_type=F32)
        z = z + bias_ref[0, :, d * hw:(d + 1) * hw]
        r = jax.nn.sigmoid(z[:, :LANES])
        i = jax.nn.sigmoid(z[:, LANES:])
        s = r * decay[d:d + 1, :]
        a = jnp.exp(-s)
        q = jnp.tanh(s) * (1.0 + a * a)
        b = (q * lax.rsqrt(jnp.maximum(q, TINY))) * (i * xc)
        return a, b

    def fwd(c, carry):
        base = pl.multiple_of(c * R, R)
        big = xp_ref[pl.ds(base, R + 2 * pad), :]
        xc = cb
        for k in range(cw.shape[0]):
            off = pad + k - RG_CONV_LEFT
            xc = xc + big[off:off + R, :] * cw[k:k + 1, :]
        xc_ref[pl.ds(base, R), :] = xc
        a, b = gates(xc, 0)
        h, carry = _scan_chunk(a, b, carry, False)
        hf_ref[pl.ds(base, R), :] = h
        return carry

    lax.fori_loop(0, nc, fwd, jnp.zeros((SUBLANES, LANES), F32))

    def bwd(s, carry):
        base = pl.multiple_of((nc - 1 - s) * R, R)
        xc = xc_ref[pl.ds(base, R), :]
        a, b = gates(xc, 1)
        h, carry = _scan_chunk(a, b, carry, True)
        gate = jax.nn.gelu(g_ref[0, pl.ds(base, R), :])
        o_ref[0, pl.ds(base, R), :] = (hf_ref[pl.ds(base, R), :] + h) * gate
        return carry

    lax.fori_loop(0, nc, bwd, jnp.zeros((SUBLANES, LANES), F32))


def _rglru(proj, cw, cb, w_cat, b_cat, lam, *, d_rg):
    Bt, L, _ = proj.shape
    nh = d_rg // LANES
    R = _pick(L, (512, 256, 128, 64, 32, 16, 8))
    kern = functools.partial(_rglru_kernel, L=L, R=R)
    return pl.pallas_call(
        kern,
        grid=(Bt, nh),
        in_specs=[
            pl.BlockSpec((1, L, LANES), lambda b, h: (b, 0, h)),
            pl.BlockSpec((1, L, LANES), lambda b, h: (b, 0, nh + h)),
            pl.BlockSpec((cw.shape[0], LANES), lambda b, h: (0, h)),
            pl.BlockSpec((1, LANES), lambda b, h: (0, h)),
            pl.BlockSpec((1, LANES, 4 * LANES), lambda b, h: (h, 0, 0)),
            pl.BlockSpec((1, 1, 4 * LANES), lambda b, h: (h, 0, 0)),
            pl.BlockSpec((2, LANES), lambda b, h: (0, h)),
        ],
        out_specs=pl.BlockSpec((1, L, LANES), lambda b, h: (b, 0, h)),
        out_shape=jax.ShapeDtypeStruct((Bt, L, d_rg), F32),
        scratch_shapes=[
            pltpu.VMEM((L + 2 * SUBLANES, LANES), F32),
            pltpu.VMEM((L, LANES), F32),
            pltpu.VMEM((L, LANES), F32),
        ],
        compiler_params=_params(("parallel", "parallel")),
        name="rglru",
    )(proj, proj, cw, cb, w_cat, b_cat, lam)


def _dft_tables(L):
    n = 2 * L
    A = L // LANES
    nb = np.arange(2 * A, dtype=np.float64)[None, :]
    kb = np.arange(A, dtype=np.float64)[:, None] + 0.5
    phi = 2.0 * np.pi * kb * nb / (2 * A)
    fr_full = np.concatenate([np.cos(phi), -np.sin(phi)], 0)
    na = np.arange(LANES, dtype=np.float64)[None, :]
    psi = 2.0 * np.pi * kb * na / n
    tw = np.stack([np.cos(psi), -np.sin(psi)])
    aa = 2.0 * np.pi * np.outer(np.arange(LANES), np.arange(LANES)) / LANES
    c, s = np.cos(aa), np.sin(aa)
    lf = np.block([[c, -s], [s, c]])
    lb = np.block([[c, s], [-s, c]])
    theta = phi[:, :A].T
    gi = np.concatenate([np.cos(theta), -np.sin(theta)], 1) * (2.0 / n)
    return dict(
        A=A,
        fr_full=jnp.asarray(fr_full, BF16), fr_half=jnp.asarray(fr_full[:, :A], BF16),
        tw=jnp.asarray(tw, F32), lf=jnp.asarray(lf, BF16), lb=jnp.asarray(lb, BF16), gi=jnp.asarray(gi, BF16),
    )


def _row_stage(fr_ref, tiles, tw_ref, dst_ref, c0):
    A = tw_ref.shape[1]
    y = jnp.dot(fr_ref[...], jnp.concatenate(tiles, axis=1), preferred_element_type=F32)
    twr = tw_ref[0]
    twi = tw_ref[1]
    for q in range(len(tiles)):
        yr = y[:A, q * LANES:(q + 1) * LANES]
        yi = y[A:, q * LANES:(q + 1) * LANES]
        dst_ref[c0 + q, :, :LANES] = (yr * twr - yi * twi).astype(BF16)
        dst_ref[c0 + q, :, LANES:] = (yr * twi + yi * twr).astype(BF16)


def _lane_stage(src_ref, m_ref, c0, gc):
    A = src_ref.shape[1]
    a2 = src_ref[pl.ds(c0, gc)].reshape(gc * A, 2 * LANES)
    return jnp.dot(a2, m_ref[...], preferred_element_type=F32).reshape(gc, A, 2 * LANES)


def _filter_time_kernel(zt_ref, aux_ref, w1_ref, w2_ref, w3_ref, col_ref, w4_ref, delta_ref, o_ref, h3_ref):
    @pl.when(pl.program_id(1) == 0)
    def _():
        fr = col_ref[:, 0:1]
        h = jnp.sin(fr * (jnp.dot(w1_ref[...], zt_ref[...].astype(BF16), preferred_element_type=F32) + col_ref[:, 1:2]))
        h = jnp.sin(fr * (jnp.dot(w2_ref[...], h.astype(BF16), preferred_element_type=F32) + col_ref[:, 2:3]))
        h = jnp.sin(fr * (jnp.dot(w3_ref[...], h.astype(BF16), preferred_element_type=F32) + col_ref[:, 3:4]))
        h3_ref[...] = h.astype(BF16)

    sign = aux_ref[0:1, :]
    t = aux_ref[1:2, :]
    k = jnp.dot(w4_ref[...], h3_ref[...], preferred_element_type=F32)
    o_ref[...] = k * jnp.exp(-t * jnp.abs(delta_ref[:, 0:1])) * sign


def _filter_time(zt, aux, w1t, w2t, w3t, cols, w4t, delta, *, L, d_hy, n_order):
    fw = w2t.shape[0]
    lt = _pick(L, (1024, 512, 256, 128))
    nt = 2 * L // lt
    tr = _pick(d_hy, (512, 256, 128))
    nr = d_hy // tr
    const = lambda j, r: (0, 0)

    def w4_map(j, r):
        direction = (j >= nt // 2).astype(jnp.int32)
        return (((r // nr) * 2 + direction) * nr + r % nr, 0)

    return pl.pallas_call(
        _filter_time_kernel,
        grid=(nt, n_order * nr),
        in_specs=[
            pl.BlockSpec((zt.shape[0], lt), lambda j, r: (0, j)),
            pl.BlockSpec((SUBLANES, lt), lambda j, r: (0, j)),
            pl.BlockSpec(w1t.shape, const),
            pl.BlockSpec(w2t.shape, const),
            pl.BlockSpec(w3t.shape, const),
            pl.BlockSpec(cols.shape, const),
            pl.BlockSpec((tr, fw), w4_map),
            pl.BlockSpec((tr, LANES), lambda j, r: (r % nr, 0)),
        ],
        out_specs=pl.BlockSpec((tr, lt), lambda j, r: (r, j)),
        out_shape=jax.ShapeDtypeStruct((n_order * d_hy, 2 * L), F32),
        scratch_shapes=[pltpu.VMEM((fw, lt), BF16)],
        compiler_params=_params(("arbitrary", "arbitrary")),
        name="hy_filter_time",
    )(zt, aux, w1t, w2t, w3t, cols, w4t, delta)


def _filter_spec_kernel(k_ref, fr_ref, tw_ref, lf_ref, o_ref, a_ref, *, cb, gc):
    for c0 in range(0, cb, gc):
        _row_stage(fr_ref, [k_ref[c0 + q].astype(BF16) for q in range(gc)], tw_ref, a_ref, c0)
    for c0 in range(0, cb, gc):
        o_ref[pl.ds(c0, gc)] = _lane_stage(a_ref, lf_ref, c0, gc)


def _filter_spec(kt, tabs):
    C = kt.shape[0]
    A = tabs["A"]
    cb = _pick(C, (32, 16, 8))
    gc = 8
    kern = functools.partial(_filter_spec_kernel, cb=cb, gc=gc)
    return pl.pallas_call(
        kern,
        grid=(C // cb,),
        in_specs=[
            pl.BlockSpec((cb, 2 * A, LANES), lambda i: (i, 0, 0)),
            pl.BlockSpec((2 * A, 2 * A), lambda i: (0, 0)),
            pl.BlockSpec((2, A, LANES), lambda i: (0, 0, 0)),
            pl.BlockSpec((2 * LANES, 2 * LANES), lambda i: (0, 0)),
        ],
        out_specs=pl.BlockSpec((cb, A, 2 * LANES), lambda i: (i, 0, 0)),
        out_shape=jax.ShapeDtypeStruct((C, A, 2 * LANES), F32),
        scratch_shapes=[pltpu.VMEM((cb, A, 2 * LANES), BF16)],
        compiler_params=_params(("parallel",)),
        name="hy_filter_spec",
    )(kt.reshape(C, 2 * A, LANES), tabs["fr_full"], tabs["tw"], tabs["lf"])


def _shift_time(x, step):
    A = x.shape[1]
    lane = lax.broadcasted_iota(jnp.int32, x.shape, 2)
    row = lax.broadcasted_iota(jnp.int32, x.shape, 1)
    if step == 1:
        r = pltpu.roll(x, 1, axis=2)
        y = jnp.where(lane == 0, pltpu.roll(r, 1, axis=1), r)
        return jnp.where((lane == 0) & (row == 0), 0.0, y)
    r = pltpu.roll(x, LANES - 1, axis=2)
    y = jnp.where(lane == LANES - 1, pltpu.roll(r, A - 1, axis=1), r)
    return jnp.where((lane == LANES - 1) & (row == A - 1), 0.0, y)


def _hyena_kernel(v_ref, x1_ref, x2_ref, par_ref, ks0_ref, ks1_ref, fr_ref, tw_ref, lf_ref, lb_ref, gi_ref, o_ref,
                  hv_ref, hx1_ref, hx2_ref, ub_ref, a_ref, b_ref, *, cb, gc, cc):
    A = tw_ref.shape[1]
    groups = [g * gc for g in range(cb // gc)]

    def short_conv(s, src, dst, c0):
        par = par_ref[pl.ds(c0, cc)]
        x = src[pl.ds(c0, cc)]
        y = (_shift_time(x, 1) * par[:, 3 * s:3 * s + 1, :] + x * par[:, 3 * s + 1:3 * s + 2, :]
             + _shift_time(x, -1) * par[:, 3 * s + 2:3 * s + 3, :] + par[:, 9 + s:10 + s, :])
        dst[pl.ds(c0, cc)] = y
        return y

    for c0 in range(0, cb, cc):
        ub_ref[pl.ds(c0, cc)] = short_conv(0, v_ref, hv_ref, c0).astype(BF16)

    twr = tw_ref[0]
    twi = tw_ref[1]

    def long_conv(ks_ref, finish, side_work):
        for c0 in groups:
            _row_stage(fr_ref, [ub_ref[c0 + q] for q in range(gc)], tw_ref, a_ref, c0)
        for c0 in groups:
            x = _lane_stage(a_ref, lf_ref, c0, gc)
            xr = x[:, :, :LANES]
            xi = x[:, :, LANES:]
            kr = ks_ref[pl.ds(c0, gc), :, :LANES]
            ki = ks_ref[pl.ds(c0, gc), :, LANES:]
            b_ref[pl.ds(c0, gc), :, :LANES] = (xr * kr - xi * ki).astype(BF16)
            b_ref[pl.ds(c0, gc), :, LANES:] = (xr * ki + xi * kr).astype(BF16)
        for work in side_work:
            work()
        for c0 in groups:
            qv = _lane_stage(b_ref, lb_ref, c0, gc)
            qr = qv[:, :, :LANES]
            qi = qv[:, :, LANES:]
            a_ref[pl.ds(c0, gc), :, :LANES] = (qr * twr + qi * twi).astype(BF16)
            a_ref[pl.ds(c0, gc), :, LANES:] = (qi * twr - qr * twi).astype(BF16)
        for c0 in groups:
            rhs = jnp.concatenate(
                [jnp.concatenate([a_ref[c0 + q, :, :LANES], a_ref[c0 + q, :, LANES:]], axis=0) for q in range(gc)], axis=1)
            y = jnp.dot(gi_ref[...], rhs, preferred_element_type=F32)
            for q in range(gc):
                finish(c0 + q, y[:, q * LANES:(q + 1) * LANES])

    def finish0(c, y):
        z = hx1_ref[c] * (y + hv_ref[c] * par_ref[c, 12:13, :])
        hx1_ref[c] = z
        ub_ref[c] = z.astype(BF16)

    long_conv(ks0_ref, finish0,
              [functools.partial(short_conv, 1, x1_ref, hx1_ref, c0) for c0 in range(0, cb, cc)])

    def finish1(c, y):
        o_ref[c] = hx2_ref[c] * (y + hx1_ref[c] * par_ref[c, 13:14, :])

    long_conv(ks1_ref, finish1,
              [functools.partial(short_conv, 2, x2_ref, hx2_ref, c0) for c0 in range(0, cb, cc)])


def _hyena(hyt, par, ks, tabs, *, Bt, L, d_hy):
    A = tabs["A"]
    cb = _pick(d_hy, (32, 16, 8))
    gc = 8
    ncb = d_hy // cb
    x3 = hyt.reshape(3 * d_hy, Bt * A, LANES)
    kern = functools.partial(_hyena_kernel, cb=cb, gc=gc, cc=4)
    const2 = lambda j, b: (0, 0)
    tile = (cb, A, LANES)
    return pl.pallas_call(
        kern,
        grid=(ncb, Bt),
        in_specs=[
            pl.BlockSpec(tile, lambda j, b: (j, b, 0)),
            pl.BlockSpec(tile, lambda j, b: (ncb + j, b, 0)),
            pl.BlockSpec(tile, lambda j, b: (2 * ncb + j, b, 0)),
            pl.BlockSpec((cb, 16, LANES), lambda j, b: (j, 0, 0)),
            pl.BlockSpec((cb, A, 2 * LANES), lambda j, b: (j, 0, 0)),
            pl.BlockSpec((cb, A, 2 * LANES), lambda j, b: (ncb + j, 0, 0)),
            pl.BlockSpec((2 * A, A), const2),
            pl.BlockSpec((2, A, LANES), lambda j, b: (0, 0, 0)),
            pl.BlockSpec((2 * LANES, 2 * LANES), const2),
            pl.BlockSpec((2 * LANES, 2 * LANES), const2),
            pl.BlockSpec((A, 2 * A), const2),
        ],
        out_specs=pl.BlockSpec(tile, lambda j, b: (j, b, 0)),
        out_shape=jax.ShapeDtypeStruct((d_hy, Bt * A, LANES), F32),
        scratch_shapes=[
            pltpu.VMEM(tile, F32),
            pltpu.VMEM(tile, F32),
            pltpu.VMEM(tile, F32),
            pltpu.VMEM(tile, BF16),
            pltpu.VMEM((cb, A, 2 * LANES), BF16),
            pltpu.VMEM((cb, A, 2 * LANES), BF16),
        ],
        compiler_params=_params(("parallel", "arbitrary")),
        name="hyena",
    )(x3, x3, x3, par, ks, ks, tabs["fr_half"], tabs["tw"], tabs["lf"], tabs["lb"], tabs["gi"])


def _mix_norm_kernel(rg_ref, hy_ref, grg_ref, ghy_ref, o_ref, *, d_rg):
    o_ref[:, :d_rg] = _rms(rg_ref[...], grg_ref[...]).astype(BF16)
    hy = hy_ref[...]
    scale = lax.rsqrt(jnp.mean(hy * hy, axis=0, keepdims=True) + EPS)
    o_ref[:, d_rg:] = (hy * scale * ghy_ref[:, 0:1]).T.astype(BF16)


def _mix_norm(rg, hyt, g_rg, g_hy_col):
    T, d_rg = rg.shape
    d_hy = hyt.shape[0]
    tm = _pick(T, (256, 128))
    kern = functools.partial(_mix_norm_kernel, d_rg=d_rg)
    return pl.pallas_call(
        kern,
        grid=(T // tm,),
        in_specs=[
            pl.BlockSpec((tm, d_rg), lambda i: (i, 0)),
            pl.BlockSpec((d_hy, tm), lambda i: (0, i)),
            pl.BlockSpec((1, d_rg), lambda i: (0, 0)),
            pl.BlockSpec((d_hy, LANES), lambda i: (0, 0)),
        ],
        out_specs=pl.BlockSpec((tm, d_rg + d_hy), lambda i: (i, 0)),
        out_shape=jax.ShapeDtypeStruct((T, d_rg + d_hy), BF16),
        compiler_params=_params(("parallel",)),
        name="mix_norm",
    )(rg, hyt, g_rg.reshape(1, d_rg), g_hy_col)


def _post_mix_kernel(xa_ref, xb_ref, f_ref, g1_ref, g2_ref, x1_ref, hn_ref, *, n_first):
    x = jnp.where(pl.program_id(0) < n_first, xa_ref[0], xb_ref[0])
    x1 = x + _rms(f_ref[0], g1_ref[...])
    x1_ref[0] = x1
    hn_ref[0] = _rms(x1, g2_ref[...]).astype(BF16)


def _post_mix(xs, f, g1, g2, *, L):
    D = xs[0].shape[-1]
    n_first = xs[0].shape[0]
    Bt = sum(x.shape[0] for x in xs)
    tm = _pick(L, (256, 128))
    nt = L // tm
    block = (1, tm, D)
    index = lambda b, i: (b, i, 0)
    if len(xs) == 2:
        spec_a, spec_b = _pair_specs(block, n_first, nt - 1, index)
    else:
        spec_a = spec_b = pl.BlockSpec(block, index)
        xs = list(xs) * 2
    row = pl.BlockSpec(block, index)
    vec = pl.BlockSpec((1, D), lambda b, i: (0, 0))
    x1, hn = pl.pallas_call(
        functools.partial(_post_mix_kernel, n_first=n_first),
        grid=(Bt, nt),
        in_specs=[spec_a, spec_b, row, vec, vec],
        out_specs=[row, row],
        out_shape=[jax.ShapeDtypeStruct((Bt, L, D), F32), jax.ShapeDtypeStruct((Bt, L, D), BF16)],
        compiler_params=_params(("parallel", "parallel")),
        name="post_mix",
    )(xs[0], xs[1], f.reshape(Bt, L, D), g1.reshape(1, D), g2.reshape(1, D))
    return x1.reshape(Bt * L, D), hn.reshape(Bt * L, D)


def _post_ffn_pair_kernel(x_ref, f_ref, g_ref, ya_ref, yb_ref, *, n_first):
    y = x_ref[0] + _rms(f_ref[0], g_ref[...])

    @pl.when(pl.program_id(0) < n_first)
    def _():
        ya_ref[0] = y

    @pl.when(pl.program_id(0) >= n_first)
    def _():
        yb_ref[0] = y


def _post_ffn_kernel(x_ref, f_ref, g_ref, y_ref):
    y_ref[0] = x_ref[0] + _rms(f_ref[0], g_ref[...])


def _post_ffn(x, f, g, *, L, splits):
    T, D = x.shape
    Bt = T // L
    tm = _pick(L, (256, 128))
    nt = L // tm
    block = (1, tm, D)
    index = lambda b, i: (b, i, 0)
    row = pl.BlockSpec(block, index)
    args = (x.reshape(Bt, L, D), f.reshape(Bt, L, D), g.reshape(1, D))
    in_specs = [row, row, pl.BlockSpec((1, D), lambda b, i: (0, 0))]
    if len(splits) == 1:
        return (pl.pallas_call(
            _post_ffn_kernel,
            grid=(Bt, nt),
            in_specs=in_specs,
            out_specs=row,
            out_shape=jax.ShapeDtypeStruct((Bt, L, D), F32),
            compiler_params=_params(("parallel", "parallel")),
            name="post_ffn",
        )(*args),)
    n_first = splits[0]
    spec_a, spec_b = _pair_specs(block, n_first, nt - 1, index)
    return tuple(pl.pallas_call(
        functools.partial(_post_ffn_pair_kernel, n_first=n_first),
        grid=(Bt, nt),
        in_specs=in_specs,
        out_specs=[spec_a, spec_b],
        out_shape=[jax.ShapeDtypeStruct((n, L, D), F32) for n in splits],
        compiler_params=_params(("arbitrary", "arbitrary")),
        name="post_ffn",
    )(*args))


FFN_LAG = 2


def _ffn_in_kernel(a_ref, wg_ref, wu_ref, cw_ref, cb_ref, o_ref, g_ref, u_ref, *, tm, rc, n_tiles, tiles_per_seq):
    s = pl.program_id(0)
    ring = FFN_LAG + 1

    @pl.when(s == 0)
    def _():
        g_ref[...] = jnp.zeros_like(g_ref)
        u_ref[...] = jnp.zeros_like(u_ref)

    new = s % ring
    mid = (s + 1) % ring
    nxt = (s + 2) % ring

    e = jnp.maximum(s - FFN_LAG, 0) % n_tiles
    first = (e % tiles_per_seq) == 0
    last = (e % tiles_per_seq) == tiles_per_seq - 1
    cw = cw_ref[...]
    cb = cb_ref[...]
    pad = SUBLANES
    for r0 in range(0, tm, rc):
        a = a_ref[r0:r0 + rc, :]
        g_new = jnp.dot(a, wg_ref[...], preferred_element_type=F32)
        u_new = jnp.dot(a, wu_ref[...], preferred_element_type=F32)

        if r0 == 0:
            top = jnp.where(first, 0.0, g_ref[new, tm - pad:tm, :])
        else:
            top = g_ref[mid, r0 - pad:r0, :]
        if r0 + rc == tm:
            bottom = jnp.where(last, 0.0, g_ref[nxt, 0:pad, :])
        else:
            bottom = g_ref[mid, r0 + rc:r0 + rc + pad, :]
        big = jnp.concatenate([top, g_ref[mid, r0:r0 + rc, :], bottom], axis=0)
        y = (big[pad - 1:pad - 1 + rc] * cw[0:1, :] + big[pad:pad + rc] * cw[1:2, :]
             + big[pad + 1:pad + 1 + rc] * cw[2:3, :] + cb)
        o_ref[r0:r0 + rc, :] = (jax.nn.gelu(y) * u_ref[mid, r0:r0 + rc, :]).astype(BF16)

        g_ref[new, r0:r0 + rc, :] = g_new
        u_ref[new, r0:r0 + rc, :] = u_new


def _ffn_in(a, w, cw, cb, *, L, d_ff):
    T, D = a.shape
    tm = _pick(L, (1024, 512, 256, 128))
    tn = _pick(d_ff, (256, 128))
    ncol = d_ff // tn
    n_tiles = T // tm
    steps = ncol * n_tiles

    def mm_tile(s):
        return jnp.minimum(s, steps - 1)

    def ew_tile(s):
        return jnp.maximum(s - FFN_LAG, 0)

    kern = functools.partial(_ffn_in_kernel, tm=tm, rc=_pick(tm, (256, 128)), n_tiles=n_tiles, tiles_per_seq=L // tm)
    return pl.pallas_call(
        kern,
        grid=(steps + FFN_LAG,),
        in_specs=[
            pl.BlockSpec((tm, D), lambda s: (mm_tile(s) % n_tiles, 0)),
            pl.BlockSpec((D, tn), lambda s: (0, mm_tile(s) // n_tiles)),
            pl.BlockSpec((D, tn), lambda s: (0, ncol + mm_tile(s) // n_tiles)),
            pl.BlockSpec((3, tn), lambda s: (0, ew_tile(s) // n_tiles)),
            pl.BlockSpec((1, tn), lambda s: (0, ew_tile(s) // n_tiles)),
        ],
        out_specs=pl.BlockSpec((tm, tn), lambda s: (ew_tile(s) % n_tiles, ew_tile(s) // n_tiles)),
        out_shape=jax.ShapeDtypeStruct((T, d_ff), BF16),
        scratch_shapes=[pltpu.VMEM((FFN_LAG + 1, tm, tn), F32), pltpu.VMEM((FFN_LAG + 1, tm, tn), F32)],
        compiler_params=_params(("arbitrary",)),
        name="ffn_in",
    )(a, w, w, cw, cb.reshape(1, d_ff))


def _lane_bcast(v):
    return jnp.broadcast_to(v[..., None], v.shape + (LANES,))


def _hyena_filter_spectrum(L, d_hy, n_order, w1, b1, w2, b2, w3, b3, w4, freq, tabs):
    emb = w1.shape[0]
    bands = (emb - 1) // 2
    n = jnp.arange(2 * L)
    m = jnp.where(n < L, n, 2 * L - n).astype(F32)
    sign = jnp.where(n < L, 1.0, jnp.where(n == L, 0.0, -1.0)).astype(F32)
    t = m / (L - 1)
    band = jnp.linspace(1e-4, bands - 1, bands, dtype=F32)
    ang = (2.0 * math.pi / L) * m[None, :] * band[:, None]
    zt = jnp.concatenate([t[None, :], jnp.cos(ang), -jnp.sin(ang)], axis=0)
    kpad = -(-emb // BF16_ROWS) * BF16_ROWS
    zt = jnp.pad(zt, ((0, kpad - emb), (0, 0)))
    aux = jnp.zeros((SUBLANES, 2 * L), F32).at[0].set(sign).at[1].set(t)
    w1t = jnp.pad(w1.T, ((0, 0), (0, kpad - emb))).astype(BF16)
    fw = w2.shape[0]
    cols = jnp.zeros((fw, LANES), F32)
    cols = cols.at[:, 0].set(freq).at[:, 1].set(b1).at[:, 2].set(b2).at[:, 3].set(b3)
    max_decay = math.log(HY_TARGET) / HY_FAST_DECAY
    min_decay = math.log(HY_TARGET) / HY_SLOW_DECAY
    delta = _lane_bcast(jnp.linspace(min_decay, max_decay, d_hy, dtype=F32))
    kt = _filter_time(zt, aux, w1t, w2.T.astype(BF16), w3.T.astype(BF16), cols, w4.T.astype(BF16), delta,
                      L=L, d_hy=d_hy, n_order=n_order)
    return _filter_spec(kt, tabs)


def _encoder_layer(xs, p, splits):
    (pre_mix_norm, w_in, rg_conv_w, rg_conv_b, rg_a_w, rg_a_b, rg_x_w, rg_x_b, rg_lambda, hy_conv_w, hy_conv_b,
     hy_w1, hy_b1, hy_w2, hy_b2, hy_w3, hy_b3, hy_w4, hy_sin_freq, hy_bias, rg_out_norm, hy_out_norm, w_out,
     post_mix_norm, pre_ffn_norm, w_ffn_in, ffn_conv_w, ffn_conv_b, w_ffn_out, post_ffn_norm) = p
    _, L, D = xs[0].shape
    Bt = sum(x.shape[0] for x in xs)
    T = Bt * L
    d_rg = rg_conv_w.shape[-1]
    n_order, d_hy = hy_bias.shape
    d_ff = ffn_conv_w.shape[-1]
    nh = rg_a_w.shape[1]
    assert d_rg == nh * LANES and n_order == 2 and L % (LANES * BF16_ROWS) == 0

    hn, hnt = _norm_in(xs, pre_mix_norm, L=L)
    w_hyt = _transpose_cast(w_in, col0=2 * d_rg, ncols=3 * d_hy)
    tm = _pick(T, (1024, 512, 256, 128))
    proj_rg = _mm(hn, w_in[:, :2 * d_rg].astype(BF16), tm=tm, tn=_pick(2 * d_rg, (1024, 512, 256, 128)), name="proj_rg")
    hyt = _mm(w_hyt, hnt, tm=_pick(3 * d_hy, (1024, 512, 256, 128)), tn=tm, name="proj_hy")

    w_cat = jnp.concatenate([rg_a_w[0], rg_x_w[0], rg_a_w[1], rg_x_w[1]], axis=-1).astype(BF16)
    b_cat = jnp.concatenate([rg_a_b[0], rg_x_b[0], rg_a_b[1], rg_x_b[1]], axis=0)
    b_cat = b_cat.reshape(4, nh, LANES).transpose(1, 0, 2).reshape(nh, 1, 4 * LANES)
    rg_y = _rglru(proj_rg.reshape(Bt, L, 2 * d_rg), rg_conv_w, rg_conv_b.reshape(1, d_rg), w_cat, b_cat,
                  rg_lambda, d_rg=d_rg)

    tabs = _dft_tables(L)
    ks = _hyena_filter_spectrum(L, d_hy, n_order, hy_w1, hy_b1, hy_w2, hy_b2, hy_w3, hy_b3, hy_w4, hy_sin_freq, tabs)
    cw = hy_conv_w.reshape(3, 3, d_hy)
    par = jnp.concatenate([cw.transpose(1, 0, 2).reshape(9, d_hy), hy_conv_b.reshape(3, d_hy), hy_bias,
                           jnp.zeros((2, d_hy), F32)], axis=0)
    par = _lane_bcast(par.T)
    hy_y = _hyena(hyt, par, ks, tabs, Bt=Bt, L=L, d_hy=d_hy)

    mix = _mix_norm(rg_y.reshape(T, d_rg), hy_y.reshape(d_hy, T), rg_out_norm, _lane_bcast(hy_out_norm))
    f = _mm(mix, w_out.astype(BF16), tm=tm, tn=_pick(D, (1024, 512, 256, 128)), name="out_proj")
    x1, hn2 = _post_mix(xs, f, post_mix_norm, pre_ffn_norm, L=L)

    fg = _ffn_in(hn2, w_ffn_in.astype(BF16), ffn_conv_w, ffn_conv_b, L=L, d_ff=d_ff)
    f2 = _mm(fg, w_ffn_out.astype(BF16), tm=_pick(T, (512, 256, 128)), tn=_pick(D, (512, 256, 128)), name="ffn_out")
    return _post_ffn(x1, f2, post_ffn_norm, L=L, splits=splits)


def _run_trunk(xs, params):
    splits = tuple(x.shape[0] for x in xs)
    depth = params[0].shape[0]
    for l in range(depth):
        last = l == depth - 1
        xs = _encoder_layer(xs, [q[l] for q in params], splits if last else (sum(splits),))
    return xs


def kernel(x_prompt, x_sample, pre_mix_norm, w_in, rg_conv_w, rg_conv_b, rg_a_w, rg_a_b, rg_x_w, rg_x_b, rg_lambda, hy_conv_w, hy_conv_b, hy_w1, hy_b1, hy_w2, hy_b2, hy_w3, hy_b3, hy_w4, hy_sin_freq, hy_bias, rg_out_norm, hy_out_norm, w_out, post_mix_norm, pre_ffn_norm, w_ffn_in, ffn_conv_w, ffn_conv_b, w_ffn_out, post_ffn_norm):
    params = (pre_mix_norm, w_in, rg_conv_w, rg_conv_b, rg_a_w, rg_a_b, rg_x_w, rg_x_b, rg_lambda, hy_conv_w,
              hy_conv_b, hy_w1, hy_b1, hy_w2, hy_b2, hy_w3, hy_b3, hy_w4, hy_sin_freq, hy_bias, rg_out_norm,
              hy_out_norm, w_out, post_mix_norm, pre_ffn_norm, w_ffn_in, ffn_conv_w, ffn_conv_b, w_ffn_out,
              post_ffn_norm)
    if x_prompt.shape[1:] == x_sample.shape[1:]:
        return _run_trunk((x_prompt, x_sample), params)
    return (_run_trunk((x_prompt,), params)[0], _run_trunk((x_sample,), params)[0])
```

```python
import functools
import math

import numpy as np
import jax
import jax.numpy as jnp
from jax import lax
from jax.experimental import pallas as pl
from jax.experimental.pallas import tpu as pltpu

F32 = jnp.float32
BF16 = jnp.bfloat16

EPS = 1e-6
TINY = 1e-30
RG_C = 8.0
RG_CONV_LEFT = 2
HY_FAST_DECAY = 0.3
HY_SLOW_DECAY = 1.5
HY_TARGET = 1e-2

LANES = 128
SUBLANES = 8
BF16_ROWS = 16
VMEM_LIMIT = 56 * 1024 * 1024


def _pick(n, candidates):
    for c in candidates:
        if n % c == 0:
            return c
    raise ValueError(f"no tile in {candidates} divides {n}")


def _params(sem):
    return pltpu.CompilerParams(dimension_semantics=sem, vmem_limit_bytes=VMEM_LIMIT)


def _rms(x, g):
    return x * lax.rsqrt(jnp.mean(x * x, axis=-1, keepdims=True) + EPS) * g


def _pair_specs(block, n_first, inner_last, make_index):
    def first(o, i):
        return make_index(jnp.minimum(o, n_first - 1), jnp.where(o < n_first, i, inner_last))

    def second(o, i):
        return make_index(jnp.maximum(o - n_first, 0), jnp.where(o >= n_first, i, 0))

    return pl.BlockSpec(block, first), pl.BlockSpec(block, second)


def _norm_in_kernel(xa_ref, xb_ref, g_ref, hn_ref, *, n_first):
    x = jnp.where(pl.program_id(0) < n_first, xa_ref[0], xb_ref[0])
    hn_ref[0] = _rms(x, g_ref[...]).astype(BF16)


def _norm_in(xs, g, *, L):
    D = xs[0].shape[-1]
    n_first = xs[0].shape[0]
    Bt = sum(x.shape[0] for x in xs)
    tm = _pick(L, (256, 128))
    nt = L // tm
    block = (1, tm, D)
    index = lambda b, i: (b, i, 0)
    if len(xs) == 2:
        spec_a, spec_b = _pair_specs(block, n_first, nt - 1, index)
    else:
        spec_a = spec_b = pl.BlockSpec(block, index)
        xs = list(xs) * 2
    hn = pl.pallas_call(
        functools.partial(_norm_in_kernel, n_first=n_first),
        grid=(Bt, nt),
        in_specs=[spec_a, spec_b, pl.BlockSpec((1, D), lambda b, i: (0, 0))],
        out_specs=pl.BlockSpec(block, index),
        out_shape=jax.ShapeDtypeStruct((Bt, L, D), BF16),
        compiler_params=_params(("parallel", "parallel")),
        name="norm_in",
    )(xs[0], xs[1], g.reshape(1, D))
    return hn.reshape(Bt * L, D)


def _mm_kernel(a_ref, b_ref, o_ref):
    o_ref[...] = jnp.dot(a_ref[...], b_ref[...], preferred_element_type=F32).astype(o_ref.dtype)


def _mm(a, b, *, tm, tn, name):
    M, K = a.shape
    _, N = b.shape
    return pl.pallas_call(
        _mm_kernel,
        grid=(N // tn, M // tm),
        in_specs=[pl.BlockSpec((tm, K), lambda j, i: (i, 0)), pl.BlockSpec((K, tn), lambda j, i: (0, j))],
        out_specs=pl.BlockSpec((tm, tn), lambda j, i: (i, j)),
        out_shape=jax.ShapeDtypeStruct((M, N), F32),
        compiler_params=_params(("parallel", "parallel")),
        name=name,
    )(a, b)


def _mm_nt_kernel(a_ref, bt_ref, o_ref):
    o_ref[...] = lax.dot_general(a_ref[...], bt_ref[...], (((1,), (1,)), ((), ())), preferred_element_type=F32)


def _mm_nt(a, bt, *, tm, tn, name):
    M, K = a.shape
    N = bt.shape[0]
    return pl.pallas_call(
        _mm_nt_kernel,
        grid=(N // tn, M // tm),
        in_specs=[pl.BlockSpec((tm, K), lambda j, i: (i, 0)), pl.BlockSpec((tn, K), lambda j, i: (j, 0))],
        out_specs=pl.BlockSpec((tm, tn), lambda j, i: (i, j)),
        out_shape=jax.ShapeDtypeStruct((M, N), F32),
        compiler_params=_params(("parallel", "parallel")),
        name=name,
    )(a, bt)


def _cast_kernel(w_ref, o_ref):
    o_ref[...] = w_ref[...].astype(BF16)


def _cast_cols(w, *, col0, ncols):
    K = w.shape[0]
    tc = _pick(ncols, (512, 256, 128))
    assert col0 % tc == 0
    return pl.pallas_call(
        _cast_kernel,
        grid=(ncols // tc,),
        in_specs=[pl.BlockSpec((K, tc), lambda j: (0, col0 // tc + j))],
        out_specs=pl.BlockSpec((K, tc), lambda j: (0, j)),
        out_shape=jax.ShapeDtypeStruct((K, ncols), BF16),
        compiler_params=_params(("parallel",)),
        name="w_cast",
    )(w)


def _transpose_cast_kernel(w_ref, o_ref):
    o_ref[...] = w_ref[...].T.astype(BF16)


def _transpose_cast(w, *, col0, ncols):
    K = w.shape[0]
    tc = _pick(ncols, (256, 128))
    assert col0 % tc == 0
    return pl.pallas_call(
        _transpose_cast_kernel,
        grid=(ncols // tc,),
        in_specs=[pl.BlockSpec((K, tc), lambda j: (0, col0 // tc + j))],
        out_specs=pl.BlockSpec((tc, K), lambda j: (j, 0)),
        out_shape=jax.ShapeDtypeStruct((ncols, K), BF16),
        compiler_params=_params(("parallel",)),
        name="w_transpose",
    )(w)


def _scan_chunk(a, b, carry, reverse):
    R = a.shape[0]
    nv = R // SUBLANES
    a3 = a.reshape(nv, SUBLANES, LANES)
    b3 = b.reshape(nv, SUBLANES, LANES)
    row = lax.broadcasted_iota(jnp.int32, (nv, SUBLANES, LANES), 1)
    for d in (1, 2, 4):
        shift = SUBLANES - d if reverse else d
        valid = (row < SUBLANES - d) if reverse else (row >= d)
        sa = pltpu.roll(a3, shift, axis=1)
        sb = pltpu.roll(b3, shift, axis=1)
        b3 = b3 + jnp.where(valid, a3, 0.0) * sb
        a3 = a3 * jnp.where(valid, sa, 1.0)
    edge = 0 if reverse else SUBLANES - 1
    a_edge = jnp.broadcast_to(a3[:, edge:edge + 1, :], a3.shape)
    b_edge = jnp.broadcast_to(b3[:, edge:edge + 1, :], b3.shape)
    order = range(nv - 1, -1, -1) if reverse else range(nv)
    carries = [None] * nv
    for j in order:
        carries[j] = carry
        carry = b_edge[j] + a_edge[j] * carry
    h3 = b3 + a3 * jnp.stack(carries)
    return h3.reshape(R, LANES), carry


def _rglru_kernel(x_ref, g_ref, cw_ref, cb_ref, w_ref, bias_ref, lam_ref, o_ref, xp_ref, xc_ref, hf_ref, *, L, R):
    pad = SUBLANES
    zeros = jnp.zeros((pad, LANES), F32)
    xp_ref[0:pad, :] = zeros
    xp_ref[L + pad:L + 2 * pad, :] = zeros
    xp_ref[pad:L + pad, :] = x_ref[0]
    cw = cw_ref[...]
    cb = cb_ref[...]
    lam = lam_ref[...]
    decay = RG_C * jax.nn.softplus(-lam)
    nc = L // R
    hw = 2 * LANES

    def gates(xc, d):
        z = jnp.dot(xc.astype(BF16), w_ref[0, :, d * hw:(d + 1) * hw], preferred_element_type=F32)
        z = z + bias_ref[0, :, d * hw:(d + 1) * hw]
        r = jax.nn.sigmoid(z[:, :LANES])
        i = jax.nn.sigmoid(z[:, LANES:])
        s = r * decay[d:d + 1, :]
        a = jnp.exp(-s)
        q = jnp.tanh(s) * (1.0 + a * a)
        b = (q * lax.rsqrt(jnp.maximum(q, TINY))) * (i * xc)
        return a, b

    def fwd(c, carry):
        base = pl.multiple_of(c * R, R)
        big = xp_ref[pl.ds(base, R + 2 * pad), :]
        xc = cb
        for k in range(cw.shape[0]):
            off = pad + k - RG_CONV_LEFT
            xc = xc + big[off:off + R, :] * cw[k:k + 1, :]
        xc_ref[pl.ds(base, R), :] = xc
        a, b = gates(xc, 0)
        h, carry = _scan_chunk(a, b, carry, False)
        hf_ref[pl.ds(base, R), :] = h
        return carry

    lax.fori_loop(0, nc, fwd, jnp.zeros((SUBLANES, LANES), F32))

    def bwd(s, carry):
        base = pl.multiple_of((nc - 1 - s) * R, R)
        xc = xc_ref[pl.ds(base, R), :]
        a, b = gates(xc, 1)
        h, carry = _scan_chunk(a, b, carry, True)
        gate = jax.nn.gelu(g_ref[0, pl.ds(base, R), :])
        o_ref[0, pl.ds(base, R), :] = (hf_ref[pl.ds(base, R), :] + h) * gate
        return carry

    lax.fori_loop(0, nc, bwd, jnp.zeros((SUBLANES, LANES), F32))


def _rglru(proj, cw, cb, w_cat, b_cat, lam, *, d_rg):
    Bt, L, _ = proj.shape
    nh = d_rg // LANES
    R = _pick(L, (512, 256, 128, 64, 32, 16, 8))
    kern = functools.partial(_rglru_kernel, L=L, R=R)
    return pl.pallas_call(
        kern,
        grid=(Bt, nh),
        in_specs=[
            pl.BlockSpec((1, L, LANES), lambda b, h: (b, 0, h)),
            pl.BlockSpec((1, L, LANES), lambda b, h: (b, 0, nh + h)),
            pl.BlockSpec((cw.shape[0], LANES), lambda b, h: (0, h)),
            pl.BlockSpec((1, LANES), lambda b, h: (0, h)),
            pl.BlockSpec((1, LANES, 4 * LANES), lambda b, h: (h, 0, 0)),
            pl.BlockSpec((1, 1, 4 * LANES), lambda b, h: (h, 0, 0)),
            pl.BlockSpec((2, LANES), lambda b, h: (0, h)),
        ],
        out_specs=pl.BlockSpec((1, L, LANES), lambda b, h: (b, 0, h)),
        out_shape=jax.ShapeDtypeStruct((Bt, L, d_rg), F32),
        scratch_shapes=[
            pltpu.VMEM((L + 2 * SUBLANES, LANES), F32),
            pltpu.VMEM((L, LANES), F32),
            pltpu.VMEM((L, LANES), F32),
        ],
        compiler_params=_params(("parallel", "parallel")),
        name="rglru",
    )(proj, proj, cw, cb, w_cat, b_cat, lam)


def _dft_tables(L):
    n = 2 * L
    A = L // LANES
    nb = np.arange(2 * A, dtype=np.float64)[None, :]
    kb = np.arange(A, dtype=np.float64)[:, None] + 0.5
    phi = 2.0 * np.pi * kb * nb / (2 * A)
    fr_full = np.concatenate([np.cos(phi), -np.sin(phi)], 0)
    na = np.arange(LANES, dtype=np.float64)[None, :]
    psi = 2.0 * np.pi * kb * na / n
    tw = np.stack([np.cos(psi), -np.sin(psi)])
    aa = 2.0 * np.pi * np.outer(np.arange(LANES), np.arange(LANES)) / LANES
    c, s = np.cos(aa), np.sin(aa)
    lf = np.block([[c, -s], [s, c]])
    lb = np.block([[c, s], [-s, c]])
    theta = phi[:, :A].T
    gi = np.concatenate([np.cos(theta), -np.sin(theta)], 1) * (2.0 / n)
    return dict(
        A=A,
        fr_full=jnp.asarray(fr_full, BF16), fr_half=jnp.asarray(fr_full[:, :A], BF16),
        tw=jnp.asarray(tw, F32), lf=jnp.asarray(lf, BF16), lb=jnp.asarray(lb, BF16), gi=jnp.asarray(gi, BF16),
    )


def _row_stage(fr_ref, tiles, tw_ref, dst_ref, c0):
    A = tw_ref.shape[1]
    y = jnp.dot(fr_ref[...], jnp.concatenate(tiles, axis=1), preferred_element_type=F32)
    twr = tw_ref[0]
    twi = tw_ref[1]
    for q in range(len(tiles)):
        yr = y[:A, q * LANES:(q + 1) * LANES]
        yi = y[A:, q * LANES:(q + 1) * LANES]
        dst_ref[c0 + q, :, :LANES] = (yr * twr - yi * twi).astype(BF16)
        dst_ref[c0 + q, :, LANES:] = (yr * twi + yi * twr).astype(BF16)


def _lane_stage(src_ref, m_ref, c0, gc):
    A = src_ref.shape[1]
    a2 = src_ref[pl.ds(c0, gc)].reshape(gc * A, 2 * LANES)
    return jnp.dot(a2, m_ref[...], preferred_element_type=F32).reshape(gc, A, 2 * LANES)


def _filter_time_kernel(zt_ref, aux_ref, w1_ref, w2_ref, w3_ref, col_ref, w4_ref, delta_ref, o_ref, h3_ref):
    @pl.when(pl.program_id(1) == 0)
    def _():
        fr = col_ref[:, 0:1]
        h = jnp.sin(fr * (jnp.dot(w1_ref[...], zt_ref[...].astype(BF16), preferred_element_type=F32) + col_ref[:, 1:2]))
        h = jnp.sin(fr * (jnp.dot(w2_ref[...], h.astype(BF16), preferred_element_type=F32) + col_ref[:, 2:3]))
        h = jnp.sin(fr * (jnp.dot(w3_ref[...], h.astype(BF16), preferred_element_type=F32) + col_ref[:, 3:4]))
        h3_ref[...] = h.astype(BF16)

    sign = aux_ref[0:1, :]
    t = aux_ref[1:2, :]
    k = jnp.dot(w4_ref[...], h3_ref[...], preferred_element_type=F32)
    o_ref[...] = k * jnp.exp(-t * jnp.abs(delta_ref[:, 0:1])) * sign


def _filter_time(zt, aux, w1t, w2t, w3t, cols, w4t, delta, *, L, d_hy, n_order):
    fw = w2t.shape[0]
    lt = _pick(L, (1024, 512, 256, 128))
    nt = 2 * L // lt
    tr = _pick(d_hy, (512, 256, 128))
    nr = d_hy // tr
    const = lambda j, r: (0, 0)

    def w4_map(j, r):
        direction = (j >= nt // 2).astype(jnp.int32)
        return (((r // nr) * 2 + direction) * nr + r % nr, 0)

    return pl.pallas_call(
        _filter_time_kernel,
        grid=(nt, n_order * nr),
        in_specs=[
            pl.BlockSpec((zt.shape[0], lt), lambda j, r: (0, j)),
            pl.BlockSpec((SUBLANES, lt), lambda j, r: (0, j)),
            pl.BlockSpec(w1t.shape, const),
            pl.BlockSpec(w2t.shape, const),
            pl.BlockSpec(w3t.shape, const),
            pl.BlockSpec(cols.shape, const),
            pl.BlockSpec((tr, fw), w4_map),
            pl.BlockSpec((tr, LANES), lambda j, r: (r % nr, 0)),
        ],
        out_specs=pl.BlockSpec((tr, lt), lambda j, r: (r, j)),
        out_shape=jax.ShapeDtypeStruct((n_order * d_hy, 2 * L), F32),
        scratch_shapes=[pltpu.VMEM((fw, lt), BF16)],
        compiler_params=_params(("arbitrary", "arbitrary")),
        name="hy_filter_time",
    )(zt, aux, w1t, w2t, w3t, cols, w4t, delta)


def _filter_spec_kernel(k_ref, fr_ref, tw_ref, lf_ref, o_ref, a_ref, *, cb, gc):
    for c0 in range(0, cb, gc):
        _row_stage(fr_ref, [k_ref[c0 + q].astype(BF16) for q in range(gc)], tw_ref, a_ref, c0)
    for c0 in range(0, cb, gc):
        o_ref[pl.ds(c0, gc)] = _lane_stage(a_ref, lf_ref, c0, gc)


def _filter_spec(kt, tabs):
    C = kt.shape[0]
    A = tabs["A"]
    cb = _pick(C, (32, 16, 8))
    gc = 8
    kern = functools.partial(_filter_spec_kernel, cb=cb, gc=gc)
    return pl.pallas_call(
        kern,
        grid=(C // cb,),
        in_specs=[
            pl.BlockSpec((cb, 2 * A, LANES), lambda i: (i, 0, 0)),
            pl.BlockSpec((2 * A, 2 * A), lambda i: (0, 0)),
            pl.BlockSpec((2, A, LANES), lambda i: (0, 0, 0)),
            pl.BlockSpec((2 * LANES, 2 * LANES), lambda i: (0, 0)),
        ],
        out_specs=pl.BlockSpec((cb, A, 2 * LANES), lambda i: (i, 0, 0)),
        out_shape=jax.ShapeDtypeStruct((C, A, 2 * LANES), F32),
        scratch_shapes=[pltpu.VMEM((cb, A, 2 * LANES), BF16)],
        compiler_params=_params(("parallel",)),
        name="hy_filter_spec",
    )(kt.reshape(C, 2 * A, LANES), tabs["fr_full"], tabs["tw"], tabs["lf"])


def _shift_time(x, step):
    A = x.shape[1]
    lane = lax.broadcasted_iota(jnp.int32, x.shape, 2)
    row = lax.broadcasted_iota(jnp.int32, x.shape, 1)
    if step == 1:
        r = pltpu.roll(x, 1, axis=2)
        y = jnp.where(lane == 0, pltpu.roll(r, 1, axis=1), r)
        return jnp.where((lane == 0) & (row == 0), 0.0, y)
    r = pltpu.roll(x, LANES - 1, axis=2)
    y = jnp.where(lane == LANES - 1, pltpu.roll(r, A - 1, axis=1), r)
    return jnp.where((lane == LANES - 1) & (row == A - 1), 0.0, y)


def _hyena_kernel(v_ref, x1_ref, x2_ref, par_ref, ks0_ref, ks1_ref, fr_ref, tw_ref, lf_ref, lb_ref, gi_ref, o_ref,
                  hv_ref, hx1_ref, hx2_ref, ub_ref, a_ref, b_ref, *, cb, gc, cc):
    A = tw_ref.shape[1]
    groups = [g * gc for g in range(cb // gc)]

    def short_conv(s, src, dst, c0):
        par = par_ref[pl.ds(c0, cc)]
        x = src[pl.ds(c0, cc)]
        y = (_shift_time(x, 1) * par[:, 3 * s:3 * s + 1, :] + x * par[:, 3 * s + 1:3 * s + 2, :]
             + _shift_time(x, -1) * par[:, 3 * s + 2:3 * s + 3, :] + par[:, 9 + s:10 + s, :])
        dst[pl.ds(c0, cc)] = y
        return y

    for c0 in range(0, cb, cc):
        ub_ref[pl.ds(c0, cc)] = short_conv(0, v_ref, hv_ref, c0).astype(BF16)

    twr = tw_ref[0]
    twi = tw_ref[1]

    def long_conv(ks_ref, finish, side_work):
        for c0 in groups:
            _row_stage(fr_ref, [ub_ref[c0 + q] for q in range(gc)], tw_ref, a_ref, c0)
        for c0 in groups:
            x = _lane_stage(a_ref, lf_ref, c0, gc)
            xr = x[:, :, :LANES]
            xi = x[:, :, LANES:]
            kr = ks_ref[pl.ds(c0, gc), :, :LANES]
            ki = ks_ref[pl.ds(c0, gc), :, LANES:]
            b_ref[pl.ds(c0, gc), :, :LANES] = (xr * kr - xi * ki).astype(BF16)
            b_ref[pl.ds(c0, gc), :, LANES:] = (xr * ki + xi * kr).astype(BF16)
        for work in side_work:
            work()
        for c0 in groups:
            qv = _lane_stage(b_ref, lb_ref, c0, gc)
            qr = qv[:, :, :LANES]
            qi = qv[:, :, LANES:]
            a_ref[pl.ds(c0, gc), :, :LANES] = (qr * twr + qi * twi).astype(BF16)
            a_ref[pl.ds(c0, gc), :, LANES:] = (qi * twr - qr * twi).astype(BF16)
        for c0 in groups:
            rhs = jnp.concatenate(
                [jnp.concatenate([a_ref[c0 + q, :, :LANES], a_ref[c0 + q, :, LANES:]], axis=0) for q in range(gc)], axis=1)
            y = jnp.dot(gi_ref[...], rhs, preferred_element_type=F32)
            for q in range(gc):
                finish(c0 + q, y[:, q * LANES:(q + 1) * LANES])

    def finish0(c, y):
        z = hx1_ref[c] * (y + hv_ref[c] * par_ref[c, 12:13, :])
        hx1_ref[c] = z
        ub_ref[c] = z.astype(BF16)

    long_conv(ks0_ref, finish0,
              [functools.partial(short_conv, 1, x1_ref, hx1_ref, c0) for c0 in range(0, cb, cc)])

    def finish1(c, y):
        o_ref[c] = hx2_ref[c] * (y + hx1_ref[c] * par_ref[c, 13:14, :])

    long_conv(ks1_ref, finish1,
              [functools.partial(short_conv, 2, x2_ref, hx2_ref, c0) for c0 in range(0, cb, cc)])


def _hyena(hyt, par, ks, tabs, *, Bt, L, d_hy):
    A = tabs["A"]
    cb = _pick(d_hy, (32, 16, 8))
    gc = 8
    ncb = d_hy // cb
    x3 = hyt.reshape(3 * d_hy, Bt * A, LANES)
    kern = functools.partial(_hyena_kernel, cb=cb, gc=gc, cc=4)
    const2 = lambda j, b: (0, 0)
    tile = (cb, A, LANES)
    return pl.pallas_call(
        kern,
        grid=(ncb, Bt),
        in_specs=[
            pl.BlockSpec(tile, lambda j, b: (j, b, 0)),
            pl.BlockSpec(tile, lambda j, b: (ncb + j, b, 0)),
            pl.BlockSpec(tile, lambda j, b: (2 * ncb + j, b, 0)),
            pl.BlockSpec((cb, 16, LANES), lambda j, b: (j, 0, 0)),
            pl.BlockSpec((cb, A, 2 * LANES), lambda j, b: (j, 0, 0)),
            pl.BlockSpec((cb, A, 2 * LANES), lambda j, b: (ncb + j, 0, 0)),
            pl.BlockSpec((2 * A, A), const2),
            pl.BlockSpec((2, A, LANES), lambda j, b: (0, 0, 0)),
            pl.BlockSpec((2 * LANES, 2 * LANES), const2),
            pl.BlockSpec((2 * LANES, 2 * LANES), const2),
            pl.BlockSpec((A, 2 * A), const2),
        ],
        out_specs=pl.BlockSpec(tile, lambda j, b: (j, b, 0)),
        out_shape=jax.ShapeDtypeStruct((d_hy, Bt * A, LANES), F32),
        scratch_shapes=[
            pltpu.VMEM(tile, F32),
            pltpu.VMEM(tile, F32),
            pltpu.VMEM(tile, F32),
            pltpu.VMEM(tile, BF16),
            pltpu.VMEM((cb, A, 2 * LANES), BF16),
            pltpu.VMEM((cb, A, 2 * LANES), BF16),
        ],
        compiler_params=_params(("parallel", "arbitrary")),
        name="hyena",
    )(x3, x3, x3, par, ks, ks, tabs["fr_half"], tabs["tw"], tabs["lf"], tabs["lb"], tabs["gi"])


def _mix_norm_kernel(rg_ref, hy_ref, grg_ref, ghy_ref, o_ref, *, d_rg):
    o_ref[:, :d_rg] = _rms(rg_ref[...], grg_ref[...]).astype(BF16)
    hy = hy_ref[...]
    scale = lax.rsqrt(jnp.mean(hy * hy, axis=0, keepdims=True) + EPS)
    o_ref[:, d_rg:] = (hy * scale * ghy_ref[:, 0:1]).T.astype(BF16)


def _mix_norm(rg, hyt, g_rg, g_hy_col):
    T, d_rg = rg.shape
    d_hy = hyt.shape[0]
    tm = _pick(T, (256, 128))
    kern = functools.partial(_mix_norm_kernel, d_rg=d_rg)
    return pl.pallas_call(
        kern,
        grid=(T // tm,),
        in_specs=[
            pl.BlockSpec((tm, d_rg), lambda i: (i, 0)),
            pl.BlockSpec((d_hy, tm), lambda i: (0, i)),
            pl.BlockSpec((1, d_rg), lambda i: (0, 0)),
            pl.BlockSpec((d_hy, LANES), lambda i: (0, 0)),
        ],
        out_specs=pl.BlockSpec((tm, d_rg + d_hy), lambda i: (i, 0)),
        out_shape=jax.ShapeDtypeStruct((T, d_rg + d_hy), BF16),
        compiler_params=_params(("parallel",)),
        name="mix_norm",
    )(rg, hyt, g_rg.reshape(1, d_rg), g_hy_col)


def _post_mix_kernel(xa_ref, xb_ref, f_ref, g1_ref, g2_ref, x1_ref, hn_ref, *, n_first):
    x = jnp.where(pl.program_id(0) < n_first, xa_ref[0], xb_ref[0])
    x1 = x + _rms(f_ref[0], g1_ref[...])
    x1_ref[0] = x1
    hn_ref[0] = _rms(x1, g2_ref[...]).astype(BF16)


def _post_mix(xs, f, g1, g2, *, L):
    D = xs[0].shape[-1]
    n_first = xs[0].shape[0]
    Bt = sum(x.shape[0] for x in xs)
    tm = _pick(L, (256, 128))
    nt = L // tm
    block = (1, tm, D)
    index = lambda b, i: (b, i, 0)
    if len(xs) == 2:
        spec_a, spec_b = _pair_specs(block, n_first, nt - 1, index)
    else:
        spec_a = spec_b = pl.BlockSpec(block, index)
        xs = list(xs) * 2
    row = pl.BlockSpec(block, index)
    vec = pl.BlockSpec((1, D), lambda b, i: (0, 0))
    x1, hn = pl.pallas_call(
        functools.partial(_post_mix_kernel, n_first=n_first),
        grid=(Bt, nt),
        in_specs=[spec_a, spec_b, row, vec, vec],
        out_specs=[row, row],
        out_shape=[jax.ShapeDtypeStruct((Bt, L, D), F32), jax.ShapeDtypeStruct((Bt, L, D), BF16)],
        compiler_params=_params(("parallel", "parallel")),
        name="post_mix",
    )(xs[0], xs[1], f.reshape(Bt, L, D), g1.reshape(1, D), g2.reshape(1, D))
    return x1.reshape(Bt * L, D), hn.reshape(Bt * L, D)


def _post_ffn_pair_kernel(x_ref, f_ref, g_ref, ya_ref, yb_ref, *, n_first):
    y = x_ref[0] + _rms(f_ref[0], g_ref[...])

    @pl.when(pl.program_id(0) < n_first)
    def _():
        ya_ref[0] = y

    @pl.when(pl.program_id(0) >= n_first)
    def _():
        yb_ref[0] = y


def _post_ffn_kernel(x_ref, f_ref, g_ref, y_ref):
    y_ref[0] = x_ref[0] + _rms(f_ref[0], g_ref[...])


def _post_ffn(x, f, g, *, L, splits):
    T, D = x.shape
    Bt = T // L
    tm = _pick(L, (256, 128))
    nt = L // tm
    block = (1, tm, D)
    index = lambda b, i: (b, i, 0)
    row = pl.BlockSpec(block, index)
    args = (x.reshape(Bt, L, D), f.reshape(Bt, L, D), g.reshape(1, D))
    in_specs = [row, row, pl.BlockSpec((1, D), lambda b, i: (0, 0))]
    if len(splits) == 1:
        return (pl.pallas_call(
            _post_ffn_kernel,
            grid=(Bt, nt),
            in_specs=in_specs,
            out_specs=row,
            out_shape=jax.ShapeDtypeStruct((Bt, L, D), F32),
            compiler_params=_params(("parallel", "parallel")),
            name="post_ffn",
        )(*args),)
    n_first = splits[0]
    spec_a, spec_b = _pair_specs(block, n_first, nt - 1, index)
    return tuple(pl.pallas_call(
        functools.partial(_post_ffn_pair_kernel, n_first=n_first),
        grid=(Bt, nt),
        in_specs=in_specs,
        out_specs=[spec_a, spec_b],
        out_shape=[jax.ShapeDtypeStruct((n, L, D), F32) for n in splits],
        compiler_params=_params(("arbitrary", "arbitrary")),
        name="post_ffn",
    )(*args))


FFN_LAG = 2


def _ffn_in_kernel(a_ref, wg_ref, wu_ref, cw_ref, cb_ref, o_ref, g_ref, u_ref, *, tm, rc, n_tiles, tiles_per_seq):
    s = pl.program_id(0)
    ring = FFN_LAG + 1

    @pl.when(s == 0)
    def _():
        g_ref[...] = jnp.zeros_like(g_ref)
        u_ref[...] = jnp.zeros_like(u_ref)

    new = s % ring
    mid = (s + 1) % ring
    nxt = (s + 2) % ring

    e = jnp.maximum(s - FFN_LAG, 0) % n_tiles
    first = (e % tiles_per_seq) == 0
    last = (e % tiles_per_seq) == tiles_per_seq - 1
    cw = cw_ref[...]
    cb = cb_ref[...]
    pad = SUBLANES
    for r0 in range(0, tm, rc):
        a = a_ref[r0:r0 + rc, :]
        g_new = jnp.dot(a, wg_ref[...], preferred_element_type=F32)
        u_new = jnp.dot(a, wu_ref[...], preferred_element_type=F32)

        if r0 == 0:
            top = jnp.where(first, 0.0, g_ref[new, tm - pad:tm, :])
        else:
            top = g_ref[mid, r0 - pad:r0, :]
        if r0 + rc == tm:
            bottom = jnp.where(last, 0.0, g_ref[nxt, 0:pad, :])
        else:
            bottom = g_ref[mid, r0 + rc:r0 + rc + pad, :]
        big = jnp.concatenate([top, g_ref[mid, r0:r0 + rc, :], bottom], axis=0)
        y = (big[pad - 1:pad - 1 + rc] * cw[0:1, :] + big[pad:pad + rc] * cw[1:2, :]
             + big[pad + 1:pad + 1 + rc] * cw[2:3, :] + cb)
        o_ref[r0:r0 + rc, :] = (jax.nn.gelu(y) * u_ref[mid, r0:r0 + rc, :]).astype(BF16)

        g_ref[new, r0:r0 + rc, :] = g_new
        u_ref[new, r0:r0 + rc, :] = u_new


def _ffn_in(a, w, cw, cb, *, L, d_ff):
    T, D = a.shape
    tm = _pick(L, (1024, 512, 256, 128))
    tn = _pick(d_ff, (256, 128))
    ncol = d_ff // tn
    n_tiles = T // tm
    steps = ncol * n_tiles

    def mm_tile(s):
        return jnp.minimum(s, steps - 1)

    def ew_tile(s):
        return jnp.maximum(s - FFN_LAG, 0)

    kern = functools.partial(_ffn_in_kernel, tm=tm, rc=_pick(tm, (256, 128)), n_tiles=n_tiles, tiles_per_seq=L // tm)
    return pl.pallas_call(
        kern,
        grid=(steps + FFN_LAG,),
        in_specs=[
            pl.BlockSpec((tm, D), lambda s: (mm_tile(s) % n_tiles, 0)),
            pl.BlockSpec((D, tn), lambda s: (0, mm_tile(s) // n_tiles)),
            pl.BlockSpec((D, tn), lambda s: (0, ncol + mm_tile(s) // n_tiles)),
            pl.BlockSpec((3, tn), lambda s: (0, ew_tile(s) // n_tiles)),
            pl.BlockSpec((1, tn), lambda s: (0, ew_tile(s) // n_tiles)),
        ],
        out_specs=pl.BlockSpec((tm, tn), lambda s: (ew_tile(s) % n_tiles, ew_tile(s) // n_tiles)),
        out_shape=jax.ShapeDtypeStruct((T, d_ff), BF16),
        scratch_shapes=[pltpu.VMEM((FFN_LAG + 1, tm, tn), F32), pltpu.VMEM((FFN_LAG + 1, tm, tn), F32)],
        compiler_params=_params(("arbitrary",)),
        name="ffn_in",
    )(a, w, w, cw, cb.reshape(1, d_ff))


def _lane_bcast(v):
    return jnp.broadcast_to(v[..., None], v.shape + (LANES,))


def _hyena_filter_spectrum(L, d_hy, n_order, w1, b1, w2, b2, w3, b3, w4, freq, tabs):
    emb = w1.shape[0]
    bands = (emb - 1) // 2
    n = jnp.arange(2 * L)
    m = jnp.where(n < L, n, 2 * L - n).astype(F32)
    sign = jnp.where(n < L, 1.0, jnp.where(n == L, 0.0, -1.0)).astype(F32)
    t = m / (L - 1)
    band = jnp.linspace(1e-4, bands - 1, bands, dtype=F32)
    ang = (2.0 * math.pi / L) * m[None, :] * band[:, None]
    zt = jnp.concatenate([t[None, :], jnp.cos(ang), -jnp.sin(ang)], axis=0)
    kpad = -(-emb // BF16_ROWS) * BF16_ROWS
    zt = jnp.pad(zt, ((0, kpad - emb), (0, 0)))
    aux = jnp.zeros((SUBLANES, 2 * L), F32).at[0].set(sign).at[1].set(t)
    w1t = jnp.pad(w1.T, ((0, 0), (0, kpad - emb))).astype(BF16)
    fw = w2.shape[0]
    cols = jnp.zeros((fw, LANES), F32)
    cols = cols.at[:, 0].set(freq).at[:, 1].set(b1).at[:, 2].set(b2).at[:, 3].set(b3)
    max_decay = math.log(HY_TARGET) / HY_FAST_DECAY
    min_decay = math.log(HY_TARGET) / HY_SLOW_DECAY
    delta = _lane_bcast(jnp.linspace(min_decay, max_decay, d_hy, dtype=F32))
    kt = _filter_time(zt, aux, w1t, w2.T.astype(BF16), w3.T.astype(BF16), cols, w4.T.astype(BF16), delta,
                      L=L, d_hy=d_hy, n_order=n_order)
    return _filter_spec(kt, tabs)


def _encoder_layer(xs, p, splits):
    (pre_mix_norm, w_in, rg_conv_w, rg_conv_b, rg_a_w, rg_a_b, rg_x_w, rg_x_b, rg_lambda, hy_conv_w, hy_conv_b,
     hy_w1, hy_b1, hy_w2, hy_b2, hy_w3, hy_b3, hy_w4, hy_sin_freq, hy_bias, rg_out_norm, hy_out_norm, w_out,
     post_mix_norm, pre_ffn_norm, w_ffn_in, ffn_conv_w, ffn_conv_b, w_ffn_out, post_ffn_norm) = p
    _, L, D = xs[0].shape
    Bt = sum(x.shape[0] for x in xs)
    T = Bt * L
    d_rg = rg_conv_w.shape[-1]
    n_order, d_hy = hy_bias.shape
    d_ff = ffn_conv_w.shape[-1]
    nh = rg_a_w.shape[1]
    assert d_rg == nh * LANES and n_order == 2 and L % (LANES * BF16_ROWS) == 0

    hn = _norm_in(xs, pre_mix_norm, L=L)
    w_hyt = _transpose_cast(w_in, col0=2 * d_rg, ncols=3 * d_hy)
    tm = _pick(T, (1024, 512, 256, 128))
    w_rg = _cast_cols(w_in, col0=0, ncols=2 * d_rg)
    proj_rg = _mm(hn, w_rg, tm=tm, tn=_pick(2 * d_rg, (1024, 512, 256, 128)), name="proj_rg")
    hyt = _mm_nt(w_hyt, hn, tm=_pick(3 * d_hy, (1024, 512, 256, 128)), tn=tm, name="proj_hy")

    w_cat = jnp.concatenate([rg_a_w[0], rg_x_w[0], rg_a_w[1], rg_x_w[1]], axis=-1).astype(BF16)
    b_cat = jnp.concatenate([rg_a_b[0], rg_x_b[0], rg_a_b[1], rg_x_b[1]], axis=0)
    b_cat = b_cat.reshape(4, nh, LANES).transpose(1, 0, 2).reshape(nh, 1, 4 * LANES)
    rg_y = _rglru(proj_rg.reshape(Bt, L, 2 * d_rg), rg_conv_w, rg_conv_b.reshape(1, d_rg), w_cat, b_cat,
                  rg_lambda, d_rg=d_rg)

    tabs = _dft_tables(L)
    ks = _hyena_filter_spectrum(L, d_hy, n_order, hy_w1, hy_b1, hy_w2, hy_b2, hy_w3, hy_b3, hy_w4, hy_sin_freq, tabs)
    cw = hy_conv_w.reshape(3, 3, d_hy)
    par = jnp.concatenate([cw.transpose(1, 0, 2).reshape(9, d_hy), hy_conv_b.reshape(3, d_hy), hy_bias,
                           jnp.zeros((2, d_hy), F32)], axis=0)
    par = _lane_bcast(par.T)
    hy_y = _hyena(hyt, par, ks, tabs, Bt=Bt, L=L, d_hy=d_hy)

    mix = _mix_norm(rg_y.reshape(T, d_rg), hy_y.reshape(d_hy, T), rg_out_norm, _lane_bcast(hy_out_norm))
    f = _mm(mix, w_out.astype(BF16), tm=tm, tn=_pick(D, (1024, 512, 256, 128)), name="out_proj")
    x1, hn2 = _post_mix(xs, f, post_mix_norm, pre_ffn_norm, L=L)

    fg = _ffn_in(hn2, w_ffn_in.astype(BF16), ffn_conv_w, ffn_conv_b, L=L, d_ff=d_ff)
    f2 = _mm(fg, w_ffn_out.astype(BF16), tm=_pick(T, (512, 256, 128)), tn=_pick(D, (512, 256, 128)), name="ffn_out")
    return _post_ffn(x1, f2, post_ffn_norm, L=L, splits=splits)


def _run_trunk(xs, params):
    splits = tuple(x.shape[0] for x in xs)
    depth = params[0].shape[0]
    for l in range(depth):
        last = l == depth - 1
        xs = _encoder_layer(xs, [q[l] for q in params], splits if last else (sum(splits),))
    return xs


def kernel(x_prompt, x_sample, pre_mix_norm, w_in, rg_conv_w, rg_conv_b, rg_a_w, rg_a_b, rg_x_w, rg_x_b, rg_lambda, hy_conv_w, hy_conv_b, hy_w1, hy_b1, hy_w2, hy_b2, hy_w3, hy_b3, hy_w4, hy_sin_freq, hy_bias, rg_out_norm, hy_out_norm, w_out, post_mix_norm, pre_ffn_norm, w_ffn_in, ffn_conv_w, ffn_conv_b, w_ffn_out, post_ffn_norm):
    params = (pre_mix_norm, w_in, rg_conv_w, rg_conv_b, rg_a_w, rg_a_b, rg_x_w, rg_x_b, rg_lambda, hy_conv_w,
              hy_conv_b, hy_w1, hy_b1, hy_w2, hy_b2, hy_w3, hy_b3, hy_w4, hy_sin_freq, hy_bias, rg_out_norm,
              hy_out_norm, w_out, post_mix_norm, pre_ffn_norm, w_ffn_in, ffn_conv_w, ffn_conv_b, w_ffn_out,
              post_ffn_norm)
    if x_prompt.shape[1:] == x_sample.shape[1:]:
        return _run_trunk((x_prompt, x_sample), params)
    return (_run_trunk((x_prompt,), params)[0], _run_trunk((x_sample,), params)[0])
```

```python
import functools
import math

import numpy as np
import jax
import jax.numpy as jnp
from jax import lax
from jax.experimental import pallas as pl
from jax.experimental.pallas import tpu as pltpu

F32 = jnp.float32
BF16 = jnp.bfloat16

EPS = 1e-6
TINY = 1e-30
RG_C = 8.0
RG_CONV_LEFT = 2
HY_FAST_DECAY = 0.3
HY_SLOW_DECAY = 1.5
HY_TARGET = 1e-2

LANES = 128
SUBLANES = 8
BF16_ROWS = 16
VMEM_LIMIT = 56 * 1024 * 1024


def _pick(n, candidates):
    for c in candidates:
        if n % c == 0:
            return c
    raise ValueError(f"no tile in {candidates} divides {n}")


def _params(sem):
    return pltpu.CompilerParams(dimension_semantics=sem, vmem_limit_bytes=VMEM_LIMIT)


def _rms(x, g):
    return x * lax.rsqrt(jnp.mean(x * x, axis=-1, keepdims=True) + EPS) * g


def _pair_specs(block, n_first, inner_last, make_index):
    def first(o, i):
        return make_index(jnp.minimum(o, n_first - 1), jnp.where(o < n_first, i, inner_last))

    def second(o, i):
        return make_index(jnp.maximum(o - n_first, 0), jnp.where(o >= n_first, i, 0))

    return pl.BlockSpec(block, first), pl.BlockSpec(block, second)


def _norm_in_kernel(xa_ref, xb_ref, g_ref, hn_ref, *, n_first):
    x = jnp.where(pl.program_id(0) < n_first, xa_ref[0], xb_ref[0])
    hn_ref[0] = _rms(x, g_ref[...]).astype(BF16)


def _norm_in(xs, g, *, L):
    D = xs[0].shape[-1]
    n_first = xs[0].shape[0]
    Bt = sum(x.shape[0] for x in xs)
    tm = _pick(L, (256, 128))
    nt = L // tm
    block = (1, tm, D)
    index = lambda b, i: (b, i, 0)
    if len(xs) == 2:
        spec_a, spec_b = _pair_specs(block, n_first, nt - 1, index)
    else:
        spec_a = spec_b = pl.BlockSpec(block, index)
        xs = list(xs) * 2
    hn = pl.pallas_call(
        functools.partial(_norm_in_kernel, n_first=n_first),
        grid=(Bt, nt),
        in_specs=[spec_a, spec_b, pl.BlockSpec((1, D), lambda b, i: (0, 0))],
        out_specs=pl.BlockSpec(block, index),
        out_shape=jax.ShapeDtypeStruct((Bt, L, D), BF16),
        compiler_params=_params(("parallel", "parallel")),
        name="norm_in",
    )(xs[0], xs[1], g.reshape(1, D))
    return hn.reshape(Bt * L, D)


def _mm_kernel(a_ref, b_ref, o_ref):
    o_ref[...] = jnp.dot(a_ref[...], b_ref[...], preferred_element_type=F32).astype(o_ref.dtype)


def _mm(a, b, *, tm, tn, name):
    M, K = a.shape
    _, N = b.shape
    return pl.pallas_call(
        _mm_kernel,
        grid=(N // tn, M // tm),
        in_specs=[pl.BlockSpec((tm, K), lambda j, i: (i, 0)), pl.BlockSpec((K, tn), lambda j, i: (0, j))],
        out_specs=pl.BlockSpec((tm, tn), lambda j, i: (i, j)),
        out_shape=jax.ShapeDtypeStruct((M, N), F32),
        compiler_params=_params(("parallel", "parallel")),
        name=name,
    )(a, b)


def _mm_nt_kernel(a_ref, bt_ref, o_ref):
    o_ref[...] = lax.dot_general(a_ref[...], bt_ref[...], (((1,), (1,)), ((), ())), preferred_element_type=F32)


def _mm_nt(a, bt, *, tm, tn, name):
    M, K = a.shape
    N = bt.shape[0]
    return pl.pallas_call(
        _mm_nt_kernel,
        grid=(N // tn, M // tm),
        in_specs=[pl.BlockSpec((tm, K), lambda j, i: (i, 0)), pl.BlockSpec((tn, K), lambda j, i: (j, 0))],
        out_specs=pl.BlockSpec((tm, tn), lambda j, i: (i, j)),
        out_shape=jax.ShapeDtypeStruct((M, N), F32),
        compiler_params=_params(("parallel", "parallel")),
        name=name,
    )(a, bt)


def _cast_kernel(w_ref, o_ref):
    o_ref[...] = w_ref[...].astype(BF16)


def _cast_cols(w, *, col0, ncols):
    K = w.shape[0]
    tc = _pick(ncols, (512, 256, 128))
    assert col0 % tc == 0
    return pl.pallas_call(
        _cast_kernel,
        grid=(ncols // tc,),
        in_specs=[pl.BlockSpec((K, tc), lambda j: (0, col0 // tc + j))],
        out_specs=pl.BlockSpec((K, tc), lambda j: (0, j)),
        out_shape=jax.ShapeDtypeStruct((K, ncols), BF16),
        compiler_params=_params(("parallel",)),
        name="w_cast",
    )(w)


def _transpose_cast_kernel(w_ref, o_ref):
    o_ref[...] = w_ref[...].T.astype(BF16)


def _transpose_cast(w, *, col0, ncols):
    K = w.shape[0]
    tc = _pick(ncols, (256, 128))
    assert col0 % tc == 0
    return pl.pallas_call(
        _transpose_cast_kernel,
        grid=(ncols // tc,),
        in_specs=[pl.BlockSpec((K, tc), lambda j: (0, col0 // tc + j))],
        out_specs=pl.BlockSpec((tc, K), lambda j: (j, 0)),
        out_shape=jax.ShapeDtypeStruct((ncols, K), BF16),
        compiler_params=_params(("parallel",)),
        name="w_transpose",
    )(w)


def _scan_chunk(a, b, carry, reverse):
    R = a.shape[0]
    nv = R // SUBLANES
    a3 = a.reshape(nv, SUBLANES, LANES)
    b3 = b.reshape(nv, SUBLANES, LANES)
    row = lax.broadcasted_iota(jnp.int32, (nv, SUBLANES, LANES), 1)
    for d in (1, 2, 4):
        shift = SUBLANES - d if reverse else d
        valid = (row < SUBLANES - d) if reverse else (row >= d)
        sa = pltpu.roll(a3, shift, axis=1)
        sb = pltpu.roll(b3, shift, axis=1)
        b3 = b3 + jnp.where(valid, a3, 0.0) * sb
        a3 = a3 * jnp.where(valid, sa, 1.0)
    edge = 0 if reverse else SUBLANES - 1
    a_edge = jnp.broadcast_to(a3[:, edge:edge + 1, :], a3.shape)
    b_edge = jnp.broadcast_to(b3[:, edge:edge + 1, :], b3.shape)
    order = range(nv - 1, -1, -1) if reverse else range(nv)
    carries = [None] * nv
    for j in order:
        carries[j] = carry
        carry = b_edge[j] + a_edge[j] * carry
    h3 = b3 + a3 * jnp.stack(carries)
    return h3.reshape(R, LANES), carry


def _rglru_kernel(x_ref, g_ref, cw_ref, cb_ref, w_ref, bias_ref, lam_ref, o_ref, xp_ref, xc_ref, hf_ref, *, L, R):
    pad = SUBLANES
    zeros = jnp.zeros((pad, LANES), F32)
    xp_ref[0:pad, :] = zeros
    xp_ref[L + pad:L + 2 * pad, :] = zeros
    xp_ref[pad:L + pad, :] = x_ref[0]
    cw = cw_ref[...]
    cb = cb_ref[...]
    lam = lam_ref[...]
    decay = RG_C * jax.nn.softplus(-lam)
    nc = L // R
    hw = 2 * LANES

    def gates(xc, d):
        z = jnp.dot(xc.astype(BF16), w_ref[0, :, d * hw:(d + 1) * hw], preferred_element_type=F32)
        z = z + bias_ref[0, :, d * hw:(d + 1) * hw]
        r = jax.nn.sigmoid(z[:, :LANES])
        i = jax.nn.sigmoid(z[:, LANES:])
        s = r * decay[d:d + 1, :]
        a = jnp.exp(-s)
        q = jnp.tanh(s) * (1.0 + a * a)
        b = (q * lax.rsqrt(jnp.maximum(q, TINY))) * (i * xc)
        return a, b

    def fwd(c, carry):
        base = pl.multiple_of(c * R, R)
        big = xp_ref[pl.ds(base, R + 2 * pad), :]
        xc = cb
        for k in range(cw.shape[0]):
            off = pad + k - RG_CONV_LEFT
            xc = xc + big[off:off + R, :] * cw[k:k + 1, :]
        xc_ref[pl.ds(base, R), :] = xc
        a, b = gates(xc, 0)
        h, carry = _scan_chunk(a, b, carry, False)
        hf_ref[pl.ds(base, R), :] = h
        return carry

    lax.fori_loop(0, nc, fwd, jnp.zeros((SUBLANES, LANES), F32))

    def bwd(s, carry):
        base = pl.multiple_of((nc - 1 - s) * R, R)
        xc = xc_ref[pl.ds(base, R), :]
        a, b = gates(xc, 1)
        h, carry = _scan_chunk(a, b, carry, True)
        gate = jax.nn.gelu(g_ref[0, pl.ds(base, R), :])
        o_ref[0, pl.ds(base, R), :] = (hf_ref[pl.ds(base, R), :] + h) * gate
        return carry

    lax.fori_loop(0, nc, bwd, jnp.zeros((SUBLANES, LANES), F32))


def _rglru(proj, cw, cb, w_cat, b_cat, lam, *, d_rg):
    Bt, L, _ = proj.shape
    nh = d_rg // LANES
    R = _pick(L, (512, 256, 128, 64, 32, 16, 8))
    kern = functools.partial(_rglru_kernel, L=L, R=R)
    return pl.pallas_call(
        kern,
        grid=(Bt, nh),
        in_specs=[
            pl.BlockSpec((1, L, LANES), lambda b, h: (b, 0, h)),
            pl.BlockSpec((1, L, LANES), lambda b, h: (b, 0, nh + h)),
            pl.BlockSpec((cw.shape[0], LANES), lambda b, h: (0, h)),
            pl.BlockSpec((1, LANES), lambda b, h: (0, h)),
            pl.BlockSpec((1, LANES, 4 * LANES), lambda b, h: (h, 0, 0)),
            pl.BlockSpec((1, 1, 4 * LANES), lambda b, h: (h, 0, 0)),
            pl.BlockSpec((2, LANES), lambda b, h: (0, h)),
        ],
        out_specs=pl.BlockSpec((1, L, LANES), lambda b, h: (b, 0, h)),
        out_shape=jax.ShapeDtypeStruct((Bt, L, d_rg), F32),
        scratch_shapes=[
            pltpu.VMEM((L + 2 * SUBLANES, LANES), F32),
            pltpu.VMEM((L, LANES), F32),
            pltpu.VMEM((L, LANES), F32),
        ],
        compiler_params=_params(("parallel", "parallel")),
        name="rglru",
    )(proj, proj, cw, cb, w_cat, b_cat, lam)


def _rglru_proj_kernel(x_ref, xprev_ref, xnext_ref, g_ref, cw_ref, cb_ref, w_ref, bias_ref, lam_ref, wa_ref, hb_ref,
                       y_ref, hy_ref, xc_ref, hf_ref, carry_ref, *, QR, R, nq, nmm):
    q = pl.program_id(2)
    pad = SUBLANES
    nsub = QR // R
    mrows = wa_ref.shape[0] // nmm
    per_mm = nsub // nmm
    cw = cw_ref[...]
    cb = cb_ref[...]
    decay = RG_C * jax.nn.softplus(-lam_ref[...])
    hw = 2 * LANES

    def gates(xc, d):
        z = jnp.dot(xc.astype(BF16), w_ref[0, :, d * hw:(d + 1) * hw], preferred_element_type=F32)
        z = z + bias_ref[0, :, d * hw:(d + 1) * hw]
        r = jax.nn.sigmoid(z[:, :LANES])
        i = jax.nn.sigmoid(z[:, LANES:])
        s = r * decay[d:d + 1, :]
        a = jnp.exp(-s)
        v = jnp.tanh(s) * (1.0 + a * a)
        b = (v * lax.rsqrt(jnp.maximum(v, TINY))) * (i * xc)
        return a, b

    def proj_rows(k):
        rows = slice(k * mrows, (k + 1) * mrows)
        hy_ref[rows, :] = lax.dot_general(wa_ref[rows, :], hb_ref[...], (((1,), (1,)), ((), ())),
                                          preferred_element_type=F32)

    @pl.when(q < nq)
    def _():
        carry = jnp.where(q == 0, 0.0, carry_ref[...])
        base = pl.multiple_of(q * QR, QR)
        for k in range(nsub):
            if k % per_mm == 0:
                proj_rows(k // per_mm)
            r0 = k * R
            top =jnp.where(q == 0, 0.0, xprev_ref[0]) if k == 0 else x_ref[0, r0 - pad:r0, :]
            bottom = jnp.where(q == nq - 1, 0.0, xnext_ref[0]) if k == nsub - 1 else x_ref[0, r0 + R:r0 + R + pad, :]
            big = jnp.concatenate([top, x_ref[0, r0:r0 + R, :], bottom], axis=0)
            xc = cb
            for t in range(cw.shape[0]):
                off = pad + t - RG_CONV_LEFT
                xc = xc + big[off:off + R, :] * cw[t:t + 1, :]
            xc_ref[pl.ds(base + r0, R), :] = xc
            a, b = gates(xc, 0)
            h, carry = _scan_chunk(a, b, carry, False)
            hf_ref[pl.ds(base + r0, R), :] = h
        carry_ref[...] = carry

    @pl.when(q >= nq)
    def _():
        carry = jnp.where(q == nq, 0.0, carry_ref[...])
        base = pl.multiple_of((2 * nq - 1 - q) * QR, QR)
        for k in range(nsub):
            if k % per_mm == 0:
                proj_rows(k // per_mm)
            r0 = (nsub - 1 - k) * R
            a, b = gates(xc_ref[pl.ds(base + r0, R), :], 1)
            h, carry = _scan_chunk(a, b, carry, True)
            gate = jax.nn.gelu(g_ref[0, r0:r0 + R, :])
            y_ref[0, r0:r0 + R, :] = (hf_ref[pl.ds(base + r0, R), :] + h) * gate
        carry_ref[...] = carry


def _rglru_proj_tiles(n_steps, rows, cols, nsub):
    best = None
    for tm in range(nsub * BF16_ROWS, rows + 1, nsub * BF16_ROWS):
        if rows % tm or n_steps % (rows // tm):
            continue
        ncols = n_steps // (rows // tm)
        if cols % ncols or (cols // ncols) % LANES:
            continue
        tn = cols // ncols
        score = (min(tn, 2 * LANES), tm)
        if tm * tn <= 512 * 1024 and (best is None or score > best[0]):
            best = (score, (tm, tn))
    return None if best is None else best[1]


def _rglru_proj(proj, cw, cb, w_cat, b_cat, lam, w_hyt, hn, *, d_rg):
    Bt, L, _ = proj.shape
    nh = d_rg // LANES
    nq = 4
    QR = L // nq
    R = _pick(QR, (512, 256, 128, 64, 32, 16, 8))
    nsub = QR // R
    nmm = _pick(nsub, (4, 2, 1))
    n_steps = Bt * nh * 2 * nq
    rows, K = w_hyt.shape
    T = hn.shape[0]
    tiles = _rglru_proj_tiles(n_steps, rows, T, nmm)
    assert tiles is not None
    tmh, tnh = tiles
    ncols = T // tnh
    qb = QR // SUBLANES

    def step(b, h, q):
        return (b * nh + h) * (2 * nq) + q

    def fq(q):
        return jnp.minimum(q, nq - 1)

    def bq(q):
        return jnp.where(q < nq, nq - 1, 2 * nq - 1 - q)

    kern = functools.partial(_rglru_proj_kernel, QR=QR, R=R, nq=nq, nmm=nmm)
    return pl.pallas_call(
        kern,
        grid=(Bt, nh, 2 * nq),
        in_specs=[
            pl.BlockSpec((1, QR, LANES), lambda b, h, q: (b, fq(q), h)),
            pl.BlockSpec((1, SUBLANES, LANES), lambda b, h, q: (b, jnp.maximum(fq(q) * qb - 1, 0), h)),
            pl.BlockSpec((1, SUBLANES, LANES), lambda b, h, q: (b, jnp.minimum((fq(q) + 1) * qb, L // SUBLANES - 1), h)),
            pl.BlockSpec((1, QR, LANES), lambda b, h, q: (b, bq(q), nh + h)),
            pl.BlockSpec((cw.shape[0], LANES), lambda b, h, q: (0, h)),
            pl.BlockSpec((1, LANES), lambda b, h, q: (0, h)),
            pl.BlockSpec((1, LANES, 4 * LANES), lambda b, h, q: (h, 0, 0)),
            pl.BlockSpec((1, 1, 4 * LANES), lambda b, h, q: (h, 0, 0)),
            pl.BlockSpec((2, LANES), lambda b, h, q: (0, h)),
            pl.BlockSpec((tmh, K), lambda b, h, q: (step(b, h, q) // ncols, 0)),
            pl.BlockSpec((tnh, K), lambda b, h, q: (step(b, h, q) % ncols, 0)),
        ],
        out_specs=[
            pl.BlockSpec((1, QR, LANES), lambda b, h, q: (b, bq(q), h)),
            pl.BlockSpec((tmh, tnh), lambda b, h, q: (step(b, h, q) // ncols, step(b, h, q) % ncols)),
        ],
        out_shape=[jax.ShapeDtypeStruct((Bt, L, d_rg), F32), jax.ShapeDtypeStruct((rows, T), F32)],
        scratch_shapes=[pltpu.VMEM((L, LANES), F32), pltpu.VMEM((L, LANES), F32), pltpu.VMEM((SUBLANES, LANES), F32)],
        compiler_params=_params(("arbitrary", "arbitrary", "arbitrary")),
        name="rglru_proj_hy",
    )(proj, proj, proj, proj, cw, cb, w_cat, b_cat, lam, w_hyt, hn)


def _dft_tables(L):
    n = 2 * L
    A = L // LANES
    nb = np.arange(2 * A, dtype=np.float64)[None, :]
    kb = np.arange(A, dtype=np.float64)[:, None] + 0.5
    phi = 2.0 * np.pi * kb * nb / (2 * A)
    fr_full = np.concatenate([np.cos(phi), -np.sin(phi)], 0)
    na = np.arange(LANES, dtype=np.float64)[None, :]
    psi = 2.0 * np.pi * kb * na / n
    tw = np.stack([np.cos(psi), -np.sin(psi)])
    aa = 2.0 * np.pi * np.outer(np.arange(LANES), np.arange(LANES)) / LANES
    c, s = np.cos(aa), np.sin(aa)
    lf = np.block([[c, -s], [s, c]])
    lb = np.block([[c, s], [-s, c]])
    theta = phi[:, :A].T
    gi = np.concatenate([np.cos(theta), -np.sin(theta)], 1) * (2.0 / n)
    return dict(
        A=A,
        fr_full=jnp.asarray(fr_full, BF16), fr_half=jnp.asarray(fr_full[:, :A], BF16),
        tw=jnp.asarray(tw, F32), lf=jnp.asarray(lf, BF16), lb=jnp.asarray(lb, BF16), gi=jnp.asarray(gi, BF16),
    )


def _row_stage(fr_ref, tiles, tw_ref, dst_ref, c0):
    A = tw_ref.shape[1]
    y = jnp.dot(fr_ref[...], jnp.concatenate(tiles, axis=1), preferred_element_type=F32)
    twr = tw_ref[0]
    twi = tw_ref[1]
    for q in range(len(tiles)):
        yr = y[:A, q * LANES:(q + 1) * LANES]
        yi = y[A:, q * LANES:(q + 1) * LANES]
        dst_ref[c0 + q, :, :LANES] = (yr * twr - yi * twi).astype(BF16)
        dst_ref[c0 + q, :, LANES:] = (yr * twi + yi * twr).astype(BF16)


def _lane_stage(src_ref, m_ref, c0, gc):
    A = src_ref.shape[1]
    a2 = src_ref[pl.ds(c0, gc)].reshape(gc * A, 2 * LANES)
    return jnp.dot(a2, m_ref[...], preferred_element_type=F32).reshape(gc, A, 2 * LANES)


def _filter_time_kernel(zt_ref, aux_ref, w1_ref, w2_ref, w3_ref, col_ref, w4_ref, delta_ref, o_ref, h3_ref):
    @pl.when(pl.program_id(1) == 0)
    def _():
        fr = col_ref[:, 0:1]
        h = jnp.sin(fr * (jnp.dot(w1_ref[...], zt_ref[...].astype(BF16), preferred_element_type=F32) + col_ref[:, 1:2]))
        h = jnp.sin(fr * (jnp.dot(w2_ref[...], h.astype(BF16), preferred_element_type=F32) + col_ref[:, 2:3]))
        h = jnp.sin(fr * (jnp.dot(w3_ref[...], h.astype(BF16), preferred_element_type=F32) + col_ref[:, 3:4]))
        h3_ref[...] = h.astype(BF16)

    sign = aux_ref[0:1, :]
    t = aux_ref[1:2, :]
    k = jnp.dot(w4_ref[...], h3_ref[...], preferred_element_type=F32)
    o_ref[...] = k * jnp.exp(-t * jnp.abs(delta_ref[:, 0:1])) * sign


def _filter_time(zt, aux, w1t, w2t, w3t, cols, w4t, delta, *, L, d_hy, n_order):
    fw = w2t.shape[0]
    lt = _pick(L, (1024, 512, 256, 128))
    nt = 2 * L // lt
    tr = _pick(d_hy, (512, 256, 128))
    nr = d_hy // tr
    const = lambda j, r: (0, 0)

    def w4_map(j, r):
        direction = (j >= nt // 2).astype(jnp.int32)
        return (((r // nr) * 2 + direction) * nr + r % nr, 0)

    return pl.pallas_call(
        _filter_time_kernel,
        grid=(nt, n_order * nr),
        in_specs=[
            pl.BlockSpec((zt.shape[0], lt), lambda j, r: (0, j)),
            pl.BlockSpec((SUBLANES, lt), lambda j, r: (0, j)),
            pl.BlockSpec(w1t.shape, const),
            pl.BlockSpec(w2t.shape, const),
            pl.BlockSpec(w3t.shape, const),
            pl.BlockSpec(cols.shape, const),
            pl.BlockSpec((tr, fw), w4_map),
            pl.BlockSpec((tr, LANES), lambda j, r: (r % nr, 0)),
        ],
        out_specs=pl.BlockSpec((tr, lt), lambda j, r: (r, j)),
        out_shape=jax.ShapeDtypeStruct((n_order * d_hy, 2 * L), F32),
        scratch_shapes=[pltpu.VMEM((fw, lt), BF16)],
        compiler_params=_params(("arbitrary", "arbitrary")),
        name="hy_filter_time",
    )(zt, aux, w1t, w2t, w3t, cols, w4t, delta)


def _filter_spec_kernel(k_ref, fr_ref, tw_ref, lf_ref, o_ref, a_ref, *, cb, gc):
    for c0 in range(0, cb, gc):
        _row_stage(fr_ref, [k_ref[c0 + q].astype(BF16) for q in range(gc)], tw_ref, a_ref, c0)
    for c0 in range(0, cb, gc):
        o_ref[pl.ds(c0, gc)] = _lane_stage(a_ref, lf_ref, c0, gc)


def _filter_spec(kt, tabs):
    C = kt.shape[0]
    A = tabs["A"]
    cb = _pick(C, (32, 16, 8))
    gc = 8
    kern = functools.partial(_filter_spec_kernel, cb=cb, gc=gc)
    return pl.pallas_call(
        kern,
        grid=(C // cb,),
        in_specs=[
            pl.BlockSpec((cb, 2 * A, LANES), lambda i: (i, 0, 0)),
            pl.BlockSpec((2 * A, 2 * A), lambda i: (0, 0)),
            pl.BlockSpec((2, A, LANES), lambda i: (0, 0, 0)),
            pl.BlockSpec((2 * LANES, 2 * LANES), lambda i: (0, 0)),
        ],
        out_specs=pl.BlockSpec((cb, A, 2 * LANES), lambda i: (i, 0, 0)),
        out_shape=jax.ShapeDtypeStruct((C, A, 2 * LANES), F32),
        scratch_shapes=[pltpu.VMEM((cb, A, 2 * LANES), BF16)],
        compiler_params=_params(("parallel",)),
        name="hy_filter_spec",
    )(kt.reshape(C, 2 * A, LANES), tabs["fr_full"], tabs["tw"], tabs["lf"])


def _shift_time(x, step):
    A = x.shape[1]
    lane = lax.broadcasted_iota(jnp.int32, x.shape, 2)
    row = lax.broadcasted_iota(jnp.int32, x.shape, 1)
    if step == 1:
        r = pltpu.roll(x, 1, axis=2)
        y = jnp.where(lane == 0, pltpu.roll(r, 1, axis=1), r)
        return jnp.where((lane == 0) & (row == 0), 0.0, y)
    r = pltpu.roll(x, LANES - 1, axis=2)
    y = jnp.where(lane == LANES - 1, pltpu.roll(r, A - 1, axis=1), r)
    return jnp.where((lane == LANES - 1) & (row == A - 1), 0.0, y)


def _hyena_kernel(v_ref, x1_ref, x2_ref, par_ref, ks0_ref, ks1_ref, fr_ref, tw_ref, lf_ref, lb_ref, gi_ref, o_ref,
                  hv_ref, hx1_ref, hx2_ref, ub_ref, a_ref, b_ref, *, cb, gc, cc):
    A = tw_ref.shape[1]
    groups = [g * gc for g in range(cb // gc)]

    def short_conv(s, src, dst, c0):
        par = par_ref[pl.ds(c0, cc)]
        x = src[pl.ds(c0, cc)]
        y = (_shift_time(x, 1) * par[:, 3 * s:3 * s + 1, :] + x * par[:, 3 * s + 1:3 * s + 2, :]
             + _shift_time(x, -1) * par[:, 3 * s + 2:3 * s + 3, :] + par[:, 9 + s:10 + s, :])
        dst[pl.ds(c0, cc)] = y
        return y

    for c0 in range(0, cb, cc):
        ub_ref[pl.ds(c0, cc)] = short_conv(0, v_ref, hv_ref, c0).astype(BF16)

    twr = tw_ref[0]
    twi = tw_ref[1]

    def long_conv(ks_ref, finish, side_work):
        for c0 in groups:
            _row_stage(fr_ref, [ub_ref[c0 + q] for q in range(gc)], tw_ref, a_ref, c0)
        for c0 in groups:
            x = _lane_stage(a_ref, lf_ref, c0, gc)
            xr = x[:, :, :LANES]
            xi = x[:, :, LANES:]
            kr = ks_ref[pl.ds(c0, gc), :, :LANES]
            ki = ks_ref[pl.ds(c0, gc), :, LANES:]
            b_ref[pl.ds(c0, gc), :, :LANES] = (xr * kr - xi * ki).astype(BF16)
            b_ref[pl.ds(c0, gc), :, LANES:] = (xr * ki + xi * kr).astype(BF16)
        for work in side_work:
            work()
        for c0 in groups:
            qv = _lane_stage(b_ref, lb_ref, c0, gc)
            qr = qv[:, :, :LANES]
            qi = qv[:, :, LANES:]
            a_ref[pl.ds(c0, gc), :, :LANES] = (qr * twr + qi * twi).astype(BF16)
            a_ref[pl.ds(c0, gc), :, LANES:] = (qi * twr - qr * twi).astype(BF16)
        for c0 in groups:
            rhs = jnp.concatenate(
                [jnp.concatenate([a_ref[c0 + q, :, :LANES], a_ref[c0 + q, :, LANES:]], axis=0) for q in range(gc)], axis=1)
            y = jnp.dot(gi_ref[...], rhs, preferred_element_type=F32)
            for q in range(gc):
                finish(c0 + q, y[:, q * LANES:(q + 1) * LANES])

    def finish0(c, y):
        z = hx1_ref[c] * (y + hv_ref[c] * par_ref[c, 12:13, :])
        hx1_ref[c] = z
        ub_ref[c] = z.astype(BF16)

    long_conv(ks0_ref, finish0,
              [functools.partial(short_conv, 1, x1_ref, hx1_ref, c0) for c0 in range(0, cb, cc)])

    def finish1(c, y):
        o_ref[c] = hx2_ref[c] * (y + hx1_ref[c] * par_ref[c, 13:14, :])

    long_conv(ks1_ref, finish1,
              [functools.partial(short_conv, 2, x2_ref, hx2_ref, c0) for c0 in range(0, cb, cc)])


def _hyena(hyt, par, ks, tabs, *, Bt, L, d_hy):
    A = tabs["A"]
    cb = _pick(d_hy, (32, 16, 8))
    gc = 8
    ncb = d_hy // cb
    x3 = hyt.reshape(3 * d_hy, Bt * A, LANES)
    kern = functools.partial(_hyena_kernel, cb=cb, gc=gc, cc=4)
    const2 = lambda j, b: (0, 0)
    tile = (cb, A, LANES)
    return pl.pallas_call(
        kern,
        grid=(ncb, Bt),
        in_specs=[
            pl.BlockSpec(tile, lambda j, b: (j, b, 0)),
            pl.BlockSpec(tile, lambda j, b: (ncb + j, b, 0)),
            pl.BlockSpec(tile, lambda j, b: (2 * ncb + j, b, 0)),
            pl.BlockSpec((cb, 16, LANES), lambda j, b: (j, 0, 0)),
            pl.BlockSpec((cb, A, 2 * LANES), lambda j, b: (j, 0, 0)),
            pl.BlockSpec((cb, A, 2 * LANES), lambda j, b: (ncb + j, 0, 0)),
            pl.BlockSpec((2 * A, A), const2),
            pl.BlockSpec((2, A, LANES), lambda j, b: (0, 0, 0)),
            pl.BlockSpec((2 * LANES, 2 * LANES), const2),
            pl.BlockSpec((2 * LANES, 2 * LANES), const2),
            pl.BlockSpec((A, 2 * A), const2),
        ],
        out_specs=pl.BlockSpec(tile, lambda j, b: (j, b, 0)),
        out_shape=jax.ShapeDtypeStruct((d_hy, Bt * A, LANES), F32),
        scratch_shapes=[
            pltpu.VMEM(tile, F32),
            pltpu.VMEM(tile, F32),
            pltpu.VMEM(tile, F32),
            pltpu.VMEM(tile, BF16),
            pltpu.VMEM((cb, A, 2 * LANES), BF16),
            pltpu.VMEM((cb, A, 2 * LANES), BF16),
        ],
        compiler_params=_params(("parallel", "arbitrary")),
        name="hyena",
    )(x3, x3, x3, par, ks, ks, tabs["fr_half"], tabs["tw"], tabs["lf"], tabs["lb"], tabs["gi"])


def _mix_norm_kernel(rg_ref, hy_ref, grg_ref, ghy_ref, o_ref, *, d_rg):
    o_ref[:, :d_rg] = _rms(rg_ref[...], grg_ref[...]).astype(BF16)
    hy = hy_ref[...]
    scale = lax.rsqrt(jnp.mean(hy * hy, axis=0, keepdims=True) + EPS)
    o_ref[:, d_rg:] = (hy * scale * ghy_ref[:, 0:1]).T.astype(BF16)


def _mix_norm(rg, hyt, g_rg, g_hy_col):
    T, d_rg = rg.shape
    d_hy = hyt.shape[0]
    tm = _pick(T, (256, 128))
    kern = functools.partial(_mix_norm_kernel, d_rg=d_rg)
    return pl.pallas_call(
        kern,
        grid=(T // tm,),
        in_specs=[
            pl.BlockSpec((tm, d_rg), lambda i: (i, 0)),
            pl.BlockSpec((d_hy, tm), lambda i: (0, i)),
            pl.BlockSpec((1, d_rg), lambda i: (0, 0)),
            pl.BlockSpec((d_hy, LANES), lambda i: (0, 0)),
        ],
        out_specs=pl.BlockSpec((tm, d_rg + d_hy), lambda i: (i, 0)),
        out_shape=jax.ShapeDtypeStruct((T, d_rg + d_hy), BF16),
        compiler_params=_params(("parallel",)),
        name="mix_norm",
    )(rg, hyt, g_rg.reshape(1, d_rg), g_hy_col)


def _post_mix_kernel(xa_ref, xb_ref, f_ref, g1_ref, g2_ref, x1_ref, hn_ref, *, n_first):
    x = jnp.where(pl.program_id(0) < n_first, xa_ref[0], xb_ref[0])
    x1 = x + _rms(f_ref[0], g1_ref[...])
    x1_ref[0] = x1
    hn_ref[0] = _rms(x1, g2_ref[...]).astype(BF16)


def _post_mix(xs, f, g1, g2, *, L):
    D = xs[0].shape[-1]
    n_first = xs[0].shape[0]
    Bt = sum(x.shape[0] for x in xs)
    tm = _pick(L, (256, 128))
    nt = L // tm
    block = (1, tm, D)
    index = lambda b, i: (b, i, 0)
    if len(xs) == 2:
        spec_a, spec_b = _pair_specs(block, n_first, nt - 1, index)
    else:
        spec_a = spec_b = pl.BlockSpec(block, index)
        xs = list(xs) * 2
    row = pl.BlockSpec(block, index)
    vec = pl.BlockSpec((1, D), lambda b, i: (0, 0))
    x1, hn = pl.pallas_call(
        functools.partial(_post_mix_kernel, n_first=n_first),
        grid=(Bt, nt),
        in_specs=[spec_a, spec_b, row, vec, vec],
        out_specs=[row, row],
        out_shape=[jax.ShapeDtypeStruct((Bt, L, D), F32), jax.ShapeDtypeStruct((Bt, L, D), BF16)],
        compiler_params=_params(("parallel", "parallel")),
        name="post_mix",
    )(xs[0], xs[1], f.reshape(Bt, L, D), g1.reshape(1, D), g2.reshape(1, D))
    return x1.reshape(Bt * L, D), hn.reshape(Bt * L, D)


def _post_ffn_pair_kernel(x_ref, f_ref, g_ref, ya_ref, yb_ref, *, n_first):
    y = x_ref[0] + _rms(f_ref[0], g_ref[...])

    @pl.when(pl.program_id(0) < n_first)
    def _():
        ya_ref[0] = y

    @pl.when(pl.program_id(0) >= n_first)
    def _():
        yb_ref[0] = y


def _post_ffn_kernel(x_ref, f_ref, g_ref, y_ref):
    y_ref[0] = x_ref[0] + _rms(f_ref[0], g_ref[...])


def _post_ffn(x, f, g, *, L, splits):
    T, D = x.shape
    Bt = T // L
    tm = _pick(L, (256, 128))
    nt = L // tm
    block = (1, tm, D)
    index = lambda b, i: (b, i, 0)
    row = pl.BlockSpec(block, index)
    args = (x.reshape(Bt, L, D), f.reshape(Bt, L, D), g.reshape(1, D))
    in_specs = [row, row, pl.BlockSpec((1, D), lambda b, i: (0, 0))]
    if len(splits) == 1:
        return (pl.pallas_call(
            _post_ffn_kernel,
            grid=(Bt, nt),
            in_specs=in_specs,
            out_specs=row,
            out_shape=jax.ShapeDtypeStruct((Bt, L, D), F32),
            compiler_params=_params(("parallel", "parallel")),
            name="post_ffn",
        )(*args),)
    n_first = splits[0]
    spec_a, spec_b = _pair_specs(block, n_first, nt - 1, index)
    return tuple(pl.pallas_call(
        functools.partial(_post_ffn_pair_kernel, n_first=n_first),
        grid=(Bt, nt),
        in_specs=in_specs,
        out_specs=[spec_a, spec_b],
        out_shape=[jax.ShapeDtypeStruct((n, L, D), F32) for n in splits],
        compiler_params=_params(("arbitrary", "arbitrary")),
        name="post_ffn",
    )(*args))


FFN_LAG = 2


def _ffn_in_kernel(a_ref, wg_ref, wu_ref, cw_ref, cb_ref, o_ref, g_ref, u_ref, *, tm, rc, n_tiles, tiles_per_seq):
    s = pl.program_id(0)
    ring = FFN_LAG + 1

    @pl.when(s == 0)
    def _():
        g_ref[...] = jnp.zeros_like(g_ref)
        u_ref[...] = jnp.zeros_like(u_ref)

    new = s % ring
    mid = (s + 1) % ring
    nxt = (s + 2) % ring

    e = jnp.maximum(s - FFN_LAG, 0) % n_tiles
    first = (e % tiles_per_seq) == 0
    last = (e % tiles_per_seq) == tiles_per_seq - 1
    cw = cw_ref[...]
    cb = cb_ref[...]
    pad = SUBLANES
    for r0 in range(0, tm, rc):
        a = a_ref[r0:r0 + rc, :]
        g_new = jnp.dot(a, wg_ref[...], preferred_element_type=F32)
        u_new = jnp.dot(a, wu_ref[...], preferred_element_type=F32)

        if r0 == 0:
            top = jnp.where(first, 0.0, g_ref[new, tm - pad:tm, :])
        else:
            top = g_ref[mid, r0 - pad:r0, :]
        if r0 + rc == tm:
            bottom = jnp.where(last, 0.0, g_ref[nxt, 0:pad, :])
        else:
            bottom = g_ref[mid, r0 + rc:r0 + rc + pad, :]
        big = jnp.concatenate([top, g_ref[mid, r0:r0 + rc, :], bottom], axis=0)
        y = (big[pad - 1:pad - 1 + rc] * cw[0:1, :] + big[pad:pad + rc] * cw[1:2, :]
             + big[pad + 1:pad + 1 + rc] * cw[2:3, :] + cb)
        o_ref[r0:r0 + rc, :] = (jax.nn.gelu(y) * u_ref[mid, r0:r0 + rc, :]).astype(BF16)

        g_ref[new, r0:r0 + rc, :] = g_new
        u_ref[new, r0:r0 + rc, :] = u_new


def _ffn_in(a, w, cw, cb, *, L, d_ff):
    T, D = a.shape
    tm = _pick(L, (1024, 512, 256, 128))
    tn = _pick(d_ff, (256, 128))
    ncol = d_ff // tn
    n_tiles = T // tm
    steps = ncol * n_tiles

    def mm_tile(s):
        return jnp.minimum(s, steps - 1)

    def ew_tile(s):
        return jnp.maximum(s - FFN_LAG, 0)

    kern = functools.partial(_ffn_in_kernel, tm=tm, rc=_pick(tm, (256, 128)), n_tiles=n_tiles, tiles_per_seq=L // tm)
    return pl.pallas_call(
        kern,
        grid=(steps + FFN_LAG,),
        in_specs=[
            pl.BlockSpec((tm, D), lambda s: (mm_tile(s) % n_tiles, 0)),
            pl.BlockSpec((D, tn), lambda s: (0, mm_tile(s) // n_tiles)),
            pl.BlockSpec((D, tn), lambda s: (0, ncol + mm_tile(s) // n_tiles)),
            pl.BlockSpec((3, tn), lambda s: (0, ew_tile(s) // n_tiles)),
            pl.BlockSpec((1, tn), lambda s: (0, ew_tile(s) // n_tiles)),
        ],
        out_specs=pl.BlockSpec((tm, tn), lambda s: (ew_tile(s) % n_tiles, ew_tile(s) // n_tiles)),
        out_shape=jax.ShapeDtypeStruct((T, d_ff), BF16),
        scratch_shapes=[pltpu.VMEM((FFN_LAG + 1, tm, tn), F32), pltpu.VMEM((FFN_LAG + 1, tm, tn), F32)],
        compiler_params=_params(("arbitrary",)),
        name="ffn_in",
    )(a, w, w, cw, cb.reshape(1, d_ff))


def _lane_bcast(v):
    return jnp.broadcast_to(v[..., None], v.shape + (LANES,))


def _hyena_filter_spectrum(L, d_hy, n_order, w1, b1, w2, b2, w3, b3, w4, freq, tabs):
    emb = w1.shape[0]
    bands = (emb - 1) // 2
    n = jnp.arange(2 * L)
    m = jnp.where(n < L, n, 2 * L - n).astype(F32)
    sign = jnp.where(n < L, 1.0, jnp.where(n == L, 0.0, -1.0)).astype(F32)
    t = m / (L - 1)
    band = jnp.linspace(1e-4, bands - 1, bands, dtype=F32)
    ang = (2.0 * math.pi / L) * m[None, :] * band[:, None]
    zt = jnp.concatenate([t[None, :], jnp.cos(ang), -jnp.sin(ang)], axis=0)
    kpad = -(-emb // BF16_ROWS) * BF16_ROWS
    zt = jnp.pad(zt, ((0, kpad - emb), (0, 0)))
    aux = jnp.zeros((SUBLANES, 2 * L), F32).at[0].set(sign).at[1].set(t)
    w1t = jnp.pad(w1.T, ((0, 0), (0, kpad - emb))).astype(BF16)
    fw = w2.shape[0]
    cols = jnp.zeros((fw, LANES), F32)
    cols = cols.at[:, 0].set(freq).at[:, 1].set(b1).at[:, 2].set(b2).at[:, 3].set(b3)
    max_decay = math.log(HY_TARGET) / HY_FAST_DECAY
    min_decay = math.log(HY_TARGET) / HY_SLOW_DECAY
    delta = _lane_bcast(jnp.linspace(min_decay, max_decay, d_hy, dtype=F32))
    kt = _filter_time(zt, aux, w1t, w2.T.astype(BF16), w3.T.astype(BF16), cols, w4.T.astype(BF16), delta,
                      L=L, d_hy=d_hy, n_order=n_order)
    return _filter_spec(kt, tabs)


def _encoder_layer(xs, p, splits):
    (pre_mix_norm, w_in, rg_conv_w, rg_conv_b, rg_a_w, rg_a_b, rg_x_w, rg_x_b, rg_lambda, hy_conv_w, hy_conv_b,
     hy_w1, hy_b1, hy_w2, hy_b2, hy_w3, hy_b3, hy_w4, hy_sin_freq, hy_bias, rg_out_norm, hy_out_norm, w_out,
     post_mix_norm, pre_ffn_norm, w_ffn_in, ffn_conv_w, ffn_conv_b, w_ffn_out, post_ffn_norm) = p
    _, L, D = xs[0].shape
    Bt = sum(x.shape[0] for x in xs)
    T = Bt * L
    d_rg = rg_conv_w.shape[-1]
    n_order, d_hy = hy_bias.shape
    d_ff = ffn_conv_w.shape[-1]
    nh = rg_a_w.shape[1]
    assert d_rg == nh * LANES and n_order == 2 and L % (LANES * BF16_ROWS) == 0

    hn = _norm_in(xs, pre_mix_norm, L=L)
    w_hyt = _transpose_cast(w_in, col0=2 * d_rg, ncols=3 * d_hy)
    tm = _pick(T, (1024, 512, 256, 128))
    w_rg = _cast_cols(w_in, col0=0, ncols=2 * d_rg)
    proj_rg = _mm(hn, w_rg, tm=tm, tn=_pick(2 * d_rg, (1024, 512, 256, 128)), name="proj_rg")

    w_cat = jnp.concatenate([rg_a_w[0], rg_x_w[0], rg_a_w[1], rg_x_w[1]], axis=-1).astype(BF16)
    b_cat = jnp.concatenate([rg_a_b[0], rg_x_b[0], rg_a_b[1], rg_x_b[1]], axis=0)
    b_cat = b_cat.reshape(4, nh, LANES).transpose(1, 0, 2).reshape(nh, 1, 4 * LANES)
    rg_y, hyt = _rglru_proj(proj_rg.reshape(Bt, L, 2 * d_rg), rg_conv_w, rg_conv_b.reshape(1, d_rg), w_cat, b_cat,
                            rg_lambda, w_hyt, hn, d_rg=d_rg)

    tabs = _dft_tables(L)
    ks = _hyena_filter_spectrum(L, d_hy, n_order, hy_w1, hy_b1, hy_w2, hy_b2, hy_w3, hy_b3, hy_w4, hy_sin_freq, tabs)
    cw = hy_conv_w.reshape(3, 3, d_hy)
    par = jnp.concatenate([cw.transpose(1, 0, 2).reshape(9, d_hy), hy_conv_b.reshape(3, d_hy), hy_bias,
                           jnp.zeros((2, d_hy), F32)], axis=0)
    par = _lane_bcast(par.T)
    hy_y = _hyena(hyt, par, ks, tabs, Bt=Bt, L=L, d_hy=d_hy)

    mix = _mix_norm(rg_y.reshape(T, d_rg), hy_y.reshape(d_hy, T), rg_out_norm, _lane_bcast(hy_out_norm))
    f = _mm(mix, w_out.astype(BF16), tm=tm, tn=_pick(D, (1024, 512, 256, 128)), name="out_proj")
    x1, hn2 = _post_mix(xs, f, post_mix_norm, pre_ffn_norm, L=L)

    fg = _ffn_in(hn2, w_ffn_in.astype(BF16), ffn_conv_w, ffn_conv_b, L=L, d_ff=d_ff)
    f2 = _mm(fg, w_ffn_out.astype(BF16), tm=_pick(T, (512, 256, 128)), tn=_pick(D, (512, 256, 128)), name="ffn_out")
    return _post_ffn(x1, f2, post_ffn_norm, L=L, splits=splits)


def _run_trunk(xs, params):
    splits = tuple(x.shape[0] for x in xs)
    depth = params[0].shape[0]
    for l in range(depth):
        last = l == depth - 1
        xs = _encoder_layer(xs, [q[l] for q in params], splits if last else (sum(splits),))
    return xs


def kernel(x_prompt, x_sample, pre_mix_norm, w_in, rg_conv_w, rg_conv_b, rg_a_w, rg_a_b, rg_x_w, rg_x_b, rg_lambda, hy_conv_w, hy_conv_b, hy_w1, hy_b1, hy_w2, hy_b2, hy_w3, hy_b3, hy_w4, hy_sin_freq, hy_bias, rg_out_norm, hy_out_norm, w_out, post_mix_norm, pre_ffn_norm, w_ffn_in, ffn_conv_w, ffn_conv_b, w_ffn_out, post_ffn_norm):
    params = (pre_mix_norm, w_in, rg_conv_w, rg_conv_b, rg_a_w, rg_a_b, rg_x_w, rg_x_b, rg_lambda, hy_conv_w,
              hy_conv_b, hy_w1, hy_b1, hy_w2, hy_b2, hy_w3, hy_b3, hy_w4, hy_sin_freq, hy_bias, rg_out_norm,
              hy_out_norm, w_out, post_mix_norm, pre_ffn_norm, w_ffn_in, ffn_conv_w, ffn_conv_b, w_ffn_out,
              post_ffn_norm)
    if x_prompt.shape[1:] == x_sample.shape[1:]:
        return _run_trunk((x_prompt, x_sample), params)
    return (_run_trunk((x_prompt,), params)[0], _run_trunk((x_sample,), params)[0])
```

```python
import functools
import math

import numpy as np
import jax
import jax.numpy as jnp
from jax import lax
from jax.experimental import pallas as pl
from jax.experimental.pallas import tpu as pltpu

F32 = jnp.float32
BF16 = jnp.bfloat16

EPS = 1e-6
TINY = 1e-30
RG_C = 8.0
RG_CONV_LEFT = 2
HY_FAST_DECAY = 0.3
HY_SLOW_DECAY = 1.5
HY_TARGET = 1e-2

LANES = 128
SUBLANES = 8
BF16_ROWS = 16
VMEM_LIMIT = 56 * 1024 * 1024


def _pick(n, candidates):
    for c in candidates:
        if n % c == 0:
            return c
    raise ValueError(f"no tile in {candidates} divides {n}")


def _params(sem):
    return pltpu.CompilerParams(dimension_semantics=sem, vmem_limit_bytes=VMEM_LIMIT)


def _rms(x, g):
    return x * lax.rsqrt(jnp.mean(x * x, axis=-1, keepdims=True) + EPS) * g


def _pair_specs(block, n_first, inner_last, make_index):
    def first(o, i):
        return make_index(jnp.minimum(o, n_first - 1), jnp.where(o < n_first, i, inner_last))

    def second(o, i):
        return make_index(jnp.maximum(o - n_first, 0), jnp.where(o >= n_first, i, 0))

    return pl.BlockSpec(block, first), pl.BlockSpec(block, second)


def _norm_in_kernel(xa_ref, xb_ref, g_ref, hn_ref, *, n_first):
    x = jnp.where(pl.program_id(0) < n_first, xa_ref[0], xb_ref[0])
    hn_ref[0] = _rms(x, g_ref[...]).astype(BF16)


def _norm_in(xs, g, *, L):
    D = xs[0].shape[-1]
    n_first = xs[0].shape[0]
    Bt = sum(x.shape[0] for x in xs)
    tm = _pick(L, (256, 128))
    nt = L // tm
    block = (1, tm, D)
    index = lambda b, i: (b, i, 0)
    if len(xs) == 2:
        spec_a, spec_b = _pair_specs(block, n_first, nt - 1, index)
    else:
        spec_a = spec_b = pl.BlockSpec(block, index)
        xs = list(xs) * 2
    hn = pl.pallas_call(
        functools.partial(_norm_in_kernel, n_first=n_first),
        grid=(Bt, nt),
        in_specs=[spec_a, spec_b, pl.BlockSpec((1, D), lambda b, i: (0, 0))],
        out_specs=pl.BlockSpec(block, index),
        out_shape=jax.ShapeDtypeStruct((Bt, L, D), BF16),
        compiler_params=_params(("parallel", "parallel")),
        name="norm_in",
    )(xs[0], xs[1], g.reshape(1, D))
    return hn.reshape(Bt * L, D)


def _mm_kernel(a_ref, b_ref, o_ref):
    o_ref[...] = jnp.dot(a_ref[...], b_ref[...], preferred_element_type=F32).astype(o_ref.dtype)


def _mm(a, b, *, tm, tn, name):
    M, K = a.shape
    _, N = b.shape
    return pl.pallas_call(
        _mm_kernel,
        grid=(N // tn, M // tm),
        in_specs=[pl.BlockSpec((tm, K), lambda j, i: (i, 0)), pl.BlockSpec((K, tn), lambda j, i: (0, j))],
        out_specs=pl.BlockSpec((tm, tn), lambda j, i: (i, j)),
        out_shape=jax.ShapeDtypeStruct((M, N), F32),
        compiler_params=_params(("parallel", "parallel")),
        name=name,
    )(a, b)


def _mm2_kernel(a1_ref, a2_ref, b_ref, o_ref):
    k1 = a1_ref.shape[1]
    acc = jnp.dot(a1_ref[...], b_ref[:k1, :], preferred_element_type=F32)
    o_ref[...] = acc + jnp.dot(a2_ref[...], b_ref[k1:, :], preferred_element_type=F32)


def _mm2(a1, a2, b, *, tm, tn, name):
    M, K1 = a1.shape
    K2 = a2.shape[1]
    N = b.shape[1]
    return pl.pallas_call(
        _mm2_kernel,
        grid=(N // tn, M // tm),
        in_specs=[pl.BlockSpec((tm, K1), lambda j, i: (i, 0)), pl.BlockSpec((tm, K2), lambda j, i: (i, 0)),
                  pl.BlockSpec((K1 + K2, tn), lambda j, i: (0, j))],
        out_specs=pl.BlockSpec((tm, tn), lambda j, i: (i, j)),
        out_shape=jax.ShapeDtypeStruct((M, N), F32),
        compiler_params=_params(("parallel", "parallel")),
        name=name,
    )(a1, a2, b)


def _mm_nt_kernel(a_ref, bt_ref, o_ref):
    o_ref[...] = lax.dot_general(a_ref[...], bt_ref[...], (((1,), (1,)), ((), ())), preferred_element_type=F32)


def _mm_nt(a, bt, *, tm, tn, name):
    M, K = a.shape
    N = bt.shape[0]
    return pl.pallas_call(
        _mm_nt_kernel,
        grid=(N // tn, M // tm),
        in_specs=[pl.BlockSpec((tm, K), lambda j, i: (i, 0)), pl.BlockSpec((tn, K), lambda j, i: (j, 0))],
        out_specs=pl.BlockSpec((tm, tn), lambda j, i: (i, j)),
        out_shape=jax.ShapeDtypeStruct((M, N), F32),
        compiler_params=_params(("parallel", "parallel")),
        name=name,
    )(a, bt)


def _cast_kernel(w_ref, o_ref):
    o_ref[...] = w_ref[...].astype(BF16)


def _cast_cols(w, *, col0, ncols):
    K = w.shape[0]
    tc = _pick(ncols, (512, 256, 128))
    assert col0 % tc == 0
    return pl.pallas_call(
        _cast_kernel,
        grid=(ncols // tc,),
        in_specs=[pl.BlockSpec((K, tc), lambda j: (0, col0 // tc + j))],
        out_specs=pl.BlockSpec((K, tc), lambda j: (0, j)),
        out_shape=jax.ShapeDtypeStruct((K, ncols), BF16),
        compiler_params=_params(("parallel",)),
        name="w_cast",
    )(w)


def _transpose_cast_kernel(w_ref, o_ref):
    o_ref[...] = w_ref[...].T.astype(BF16)


def _transpose_cast(w, *, col0, ncols):
    K = w.shape[0]
    tc = _pick(ncols, (256, 128))
    assert col0 % tc == 0
    return pl.pallas_call(
        _transpose_cast_kernel,
        grid=(ncols // tc,),
        in_specs=[pl.BlockSpec((K, tc), lambda j: (0, col0 // tc + j))],
        out_specs=pl.BlockSpec((tc, K), lambda j: (j, 0)),
        out_shape=jax.ShapeDtypeStruct((ncols, K), BF16),
        compiler_params=_params(("parallel",)),
        name="w_transpose",
    )(w)


def _scan_chunk(a, b, carry, reverse):
    R = a.shape[0]
    nv = R // SUBLANES
    a3 = a.reshape(nv, SUBLANES, LANES)
    b3 = b.reshape(nv, SUBLANES, LANES)
    row = lax.broadcasted_iota(jnp.int32, (nv, SUBLANES, LANES), 1)
    for d in (1, 2, 4):
        shift = SUBLANES - d if reverse else d
        valid = (row < SUBLANES - d) if reverse else (row >= d)
        sa = pltpu.roll(a3, shift, axis=1)
        sb = pltpu.roll(b3, shift, axis=1)
        b3 = b3 + jnp.where(valid, a3, 0.0) * sb
        a3 = a3 * jnp.where(valid, sa, 1.0)
    edge = 0 if reverse else SUBLANES - 1
    a_edge = jnp.broadcast_to(a3[:, edge:edge + 1, :], a3.shape)
    b_edge = jnp.broadcast_to(b3[:, edge:edge + 1, :], b3.shape)
    order = range(nv - 1, -1, -1) if reverse else range(nv)
    carries = [None] * nv
    for j in order:
        carries[j] = carry
        carry = b_edge[j] + a_edge[j] * carry
    h3 = b3 + a3 * jnp.stack(carries)
    return h3.reshape(R, LANES), carry


def _rglru_kernel(x_ref, g_ref, cw_ref, cb_ref, w_ref, bias_ref, lam_ref, o_ref, xp_ref, xc_ref, hf_ref, *, L, R):
    pad = SUBLANES
    zeros = jnp.zeros((pad, LANES), F32)
    xp_ref[0:pad, :] = zeros
    xp_ref[L + pad:L + 2 * pad, :] = zeros
    xp_ref[pad:L + pad, :] = x_ref[0]
    cw = cw_ref[...]
    cb = cb_ref[...]
    lam = lam_ref[...]
    decay = RG_C * jax.nn.softplus(-lam)
    nc = L // R
    hw = 2 * LANES

    def gates(xc, d):
        z = jnp.dot(xc.astype(BF16), w_ref[0, :, d * hw:(d + 1) * hw], preferred_element_type=F32)
        z = z + bias_ref[0, :, d * hw:(d + 1) * hw]
        r = jax.nn.sigmoid(z[:, :LANES])
        i = jax.nn.sigmoid(z[:, LANES:])
        s = r * decay[d:d + 1, :]
        a = jnp.exp(-s)
        q = jnp.tanh(s) * (1.0 + a * a)
        b = (q * lax.rsqrt(jnp.maximum(q, TINY))) * (i * xc)
        return a, b

    def fwd(c, carry):
        base = pl.multiple_of(c * R, R)
        big = xp_ref[pl.ds(base, R + 2 * pad), :]
        xc = cb
        for k in range(cw.shape[0]):
            off = pad + k - RG_CONV_LEFT
            xc = xc + big[off:off + R, :] * cw[k:k + 1, :]
        xc_ref[pl.ds(base, R), :] = xc
        a, b = gates(xc, 0)
        h, carry = _scan_chunk(a, b, carry, False)
        hf_ref[pl.ds(base, R), :] = h
        return carry

    lax.fori_loop(0, nc, fwd, jnp.zeros((SUBLANES, LANES), F32))

    def bwd(s, carry):
        base = pl.multiple_of((nc - 1 - s) * R, R)
        xc = xc_ref[pl.ds(base, R), :]
        a, b = gates(xc, 1)
        h, carry = _scan_chunk(a, b, carry, True)
        gate = jax.nn.gelu(g_ref[0, pl.ds(base, R), :])
        o_ref[0, pl.ds(base, R), :] = (hf_ref[pl.ds(base, R), :] + h) * gate
        return carry

    lax.fori_loop(0, nc, bwd, jnp.zeros((SUBLANES, LANES), F32))


def _rglru(proj, cw, cb, w_cat, b_cat, lam, *, d_rg):
    Bt, L, _ = proj.shape
    nh = d_rg // LANES
    R = _pick(L, (512, 256, 128, 64, 32, 16, 8))
    kern = functools.partial(_rglru_kernel, L=L, R=R)
    return pl.pallas_call(
        kern,
        grid=(Bt, nh),
        in_specs=[
            pl.BlockSpec((1, L, LANES), lambda b, h: (b, 0, h)),
            pl.BlockSpec((1, L, LANES), lambda b, h: (b, 0, nh + h)),
            pl.BlockSpec((cw.shape[0], LANES), lambda b, h: (0, h)),
            pl.BlockSpec((1, LANES), lambda b, h: (0, h)),
            pl.BlockSpec((1, LANES, 4 * LANES), lambda b, h: (h, 0, 0)),
            pl.BlockSpec((1, 1, 4 * LANES), lambda b, h: (h, 0, 0)),
            pl.BlockSpec((2, LANES), lambda b, h: (0, h)),
        ],
        out_specs=pl.BlockSpec((1, L, LANES), lambda b, h: (b, 0, h)),
        out_shape=jax.ShapeDtypeStruct((Bt, L, d_rg), F32),
        scratch_shapes=[
            pltpu.VMEM((L + 2 * SUBLANES, LANES), F32),
            pltpu.VMEM((L, LANES), F32),
            pltpu.VMEM((L, LANES), F32),
        ],
        compiler_params=_params(("parallel", "parallel")),
        name="rglru",
    )(proj, proj, cw, cb, w_cat, b_cat, lam)


def _rglru_proj_kernel(x_ref, xprev_ref, xnext_ref, g_ref, cw_ref, cb_ref, w_ref, bias_ref, lam_ref, wa_ref, hb_ref,
                       y_ref, hy_ref, xc_ref, hf_ref, carry_ref, *, QR, R, nq, nmm):
    q = pl.program_id(2)
    pad = SUBLANES
    nsub = QR // R
    mrows = wa_ref.shape[0] // nmm
    per_mm = nsub // nmm
    cw = cw_ref[...]
    cb = cb_ref[...]
    decay = RG_C * jax.nn.softplus(-lam_ref[...])
    hw = 2 * LANES

    def gates(xc, d):
        z = jnp.dot(xc.astype(BF16), w_ref[0, :, d * hw:(d + 1) * hw], preferred_element_type=F32)
        z = z + bias_ref[0, :, d * hw:(d + 1) * hw]
        r = jax.nn.sigmoid(z[:, :LANES])
        i = jax.nn.sigmoid(z[:, LANES:])
        s = r * decay[d:d + 1, :]
        a = jnp.exp(-s)
        v = jnp.tanh(s) * (1.0 + a * a)
        b = (v * lax.rsqrt(jnp.maximum(v, TINY))) * (i * xc)
        return a, b

    def proj_rows(k):
        rows = slice(k * mrows, (k + 1) * mrows)
        hy_ref[rows, :] = lax.dot_general(wa_ref[rows, :], hb_ref[...], (((1,), (1,)), ((), ())),
                                          preferred_element_type=F32)

    @pl.when(q < nq)
    def _():
        carry = jnp.where(q == 0, 0.0, carry_ref[...])
        base = pl.multiple_of(q * QR, QR)
        for k in range(nsub):
            if k % per_mm == 0:
                proj_rows(k // per_mm)
            r0 = k * R
            top =jnp.where(q == 0, 0.0, xprev_ref[0]) if k == 0 else x_ref[0, r0 - pad:r0, :]
            bottom = jnp.where(q == nq - 1, 0.0, xnext_ref[0]) if k == nsub - 1 else x_ref[0, r0 + R:r0 + R + pad, :]
            big = jnp.concatenate([top, x_ref[0, r0:r0 + R, :], bottom], axis=0)
            xc = cb
            for t in range(cw.shape[0]):
                off = pad + t - RG_CONV_LEFT
                xc = xc + big[off:off + R, :] * cw[t:t + 1, :]
            xc_ref[pl.ds(base + r0, R), :] = xc
            a, b = gates(xc, 0)
            h, carry = _scan_chunk(a, b, carry, False)
            hf_ref[pl.ds(base + r0, R), :] = h
        carry_ref[...] = carry

    @pl.when(q >= nq)
    def _():
        carry = jnp.where(q == nq, 0.0, carry_ref[...])
        base = pl.multiple_of((2 * nq - 1 - q) * QR, QR)
        for k in range(nsub):
            if k % per_mm == 0:
                proj_rows(k // per_mm)
            r0 = (nsub - 1 - k) * R
            a, b = gates(xc_ref[pl.ds(base + r0, R), :], 1)
            h, carry = _scan_chunk(a, b, carry, True)
            gate = jax.nn.gelu(g_ref[0, r0:r0 + R, :])
            y_ref[0, r0:r0 + R, :] = (hf_ref[pl.ds(base + r0, R), :] + h) * gate
        carry_ref[...] = carry


def _rglru_proj_tiles(n_steps, rows, cols, nsub):
    best = None
    for tm in range(nsub * BF16_ROWS, rows + 1, nsub * BF16_ROWS):
        if rows % tm or n_steps % (rows // tm):
            continue
        ncols = n_steps // (rows // tm)
        if cols % ncols or (cols // ncols) % LANES:
            continue
        tn = cols // ncols
        score = (min(tn, 2 * LANES), tm)
        if tm * tn <= 512 * 1024 and (best is None or score > best[0]):
            best = (score, (tm, tn))
    return None if best is None else best[1]


def _rglru_proj(proj, cw, cb, w_cat, b_cat, lam, w_hyt, hn, *, d_rg):
    Bt, L, _ = proj.shape
    nh = d_rg // LANES
    nq = 4
    QR = L // nq
    R = _pick(QR, (512, 256, 128, 64, 32, 16, 8))
    nsub = QR // R
    nmm = _pick(nsub, (4, 2, 1))
    n_steps = Bt * nh * 2 * nq
    rows, K = w_hyt.shape
    T = hn.shape[0]
    tiles = _rglru_proj_tiles(n_steps, rows, T, nmm)
    assert tiles is not None
    tmh, tnh = tiles
    ncols = T // tnh
    qb = QR // SUBLANES

    def step(b, h, q):
        return (b * nh + h) * (2 * nq) + q

    def fq(q):
        return jnp.minimum(q, nq - 1)

    def bq(q):
        return jnp.where(q < nq, nq - 1, 2 * nq - 1 - q)

    kern = functools.partial(_rglru_proj_kernel, QR=QR, R=R, nq=nq, nmm=nmm)
    return pl.pallas_call(
        kern,
        grid=(Bt, nh, 2 * nq),
        in_specs=[
            pl.BlockSpec((1, QR, LANES), lambda b, h, q: (b, fq(q), h)),
            pl.BlockSpec((1, SUBLANES, LANES), lambda b, h, q: (b, jnp.maximum(fq(q) * qb - 1, 0), h)),
            pl.BlockSpec((1, SUBLANES, LANES), lambda b, h, q: (b, jnp.minimum((fq(q) + 1) * qb, L // SUBLANES - 1), h)),
            pl.BlockSpec((1, QR, LANES), lambda b, h, q: (b, bq(q), nh + h)),
            pl.BlockSpec((cw.shape[0], LANES), lambda b, h, q: (0, h)),
            pl.BlockSpec((1, LANES), lambda b, h, q: (0, h)),
            pl.BlockSpec((1, LANES, 4 * LANES), lambda b, h, q: (h, 0, 0)),
            pl.BlockSpec((1, 1, 4 * LANES), lambda b, h, q: (h, 0, 0)),
            pl.BlockSpec((2, LANES), lambda b, h, q: (0, h)),
            pl.BlockSpec((tmh, K), lambda b, h, q: (step(b, h, q) // ncols, 0)),
            pl.BlockSpec((tnh, K), lambda b, h, q: (step(b, h, q) % ncols, 0)),
        ],
        out_specs=[
            pl.BlockSpec((1, QR, LANES), lambda b, h, q: (b, bq(q), h)),
            pl.BlockSpec((tmh, tnh), lambda b, h, q: (step(b, h, q) // ncols, step(b, h, q) % ncols)),
        ],
        out_shape=[jax.ShapeDtypeStruct((Bt, L, d_rg), F32), jax.ShapeDtypeStruct((rows, T), F32)],
        scratch_shapes=[pltpu.VMEM((L, LANES), F32), pltpu.VMEM((L, LANES), F32), pltpu.VMEM((SUBLANES, LANES), F32)],
        compiler_params=_params(("arbitrary", "arbitrary", "arbitrary")),
        name="rglru_proj_hy",
    )(proj, proj, proj, proj, cw, cb, w_cat, b_cat, lam, w_hyt, hn)


def _dft_tables(L):
    n = 2 * L
    A = L // LANES
    nb = np.arange(2 * A, dtype=np.float64)[None, :]
    kb = np.arange(A, dtype=np.float64)[:, None] + 0.5
    phi = 2.0 * np.pi * kb * nb / (2 * A)
    fr_full = np.concatenate([np.cos(phi), -np.sin(phi)], 0)
    na = np.arange(LANES, dtype=np.float64)[None, :]
    psi = 2.0 * np.pi * kb * na / n
    tw = np.stack([np.cos(psi), -np.sin(psi)])
    aa = 2.0 * np.pi * np.outer(np.arange(LANES), np.arange(LANES)) / LANES
    c, s = np.cos(aa), np.sin(aa)
    lf = np.block([[c, -s], [s, c]])
    lb = np.block([[c, s], [-s, c]])
    theta = phi[:, :A].T
    gi = np.concatenate([np.cos(theta), -np.sin(theta)], 1) * (2.0 / n)
    return dict(
        A=A,
        fr_full=jnp.asarray(fr_full, BF16), fr_half=jnp.asarray(fr_full[:, :A], BF16),
        tw=jnp.asarray(tw, F32), lf=jnp.asarray(lf, BF16), lb=jnp.asarray(lb, BF16), gi=jnp.asarray(gi, BF16),
    )


def _row_stage(fr_ref, tiles, tw_ref, dst_ref, c0):
    A = tw_ref.shape[1]
    y = jnp.dot(fr_ref[...], jnp.concatenate(tiles, axis=1), preferred_element_type=F32)
    twr = tw_ref[0]
    twi = tw_ref[1]
    for q in range(len(tiles)):
        yr = y[:A, q * LANES:(q + 1) * LANES]
        yi = y[A:, q * LANES:(q + 1) * LANES]
        dst_ref[c0 + q, :, :LANES] = (yr * twr - yi * twi).astype(BF16)
        dst_ref[c0 + q, :, LANES:] = (yr * twi + yi * twr).astype(BF16)


def _lane_stage(src_ref, m_ref, c0, gc):
    A = src_ref.shape[1]
    a2 = src_ref[pl.ds(c0, gc)].reshape(gc * A, 2 * LANES)
    return jnp.dot(a2, m_ref[...], preferred_element_type=F32).reshape(gc, A, 2 * LANES)


def _filter_time_kernel(zt_ref, aux_ref, w1_ref, w2_ref, w3_ref, col_ref, w4_ref, delta_ref, o_ref, h3_ref):
    @pl.when(pl.program_id(1) == 0)
    def _():
        fr = col_ref[:, 0:1]
        h = jnp.sin(fr * (jnp.dot(w1_ref[...], zt_ref[...].astype(BF16), preferred_element_type=F32) + col_ref[:, 1:2]))
        h = jnp.sin(fr * (jnp.dot(w2_ref[...], h.astype(BF16), preferred_element_type=F32) + col_ref[:, 2:3]))
        h = jnp.sin(fr * (jnp.dot(w3_ref[...], h.astype(BF16), preferred_element_type=F32) + col_ref[:, 3:4]))
        h3_ref[...] = h.astype(BF16)

    sign = aux_ref[0:1, :]
    t = aux_ref[1:2, :]
    k = jnp.dot(w4_ref[...], h3_ref[...], preferred_element_type=F32)
    o_ref[...] = k * jnp.exp(-t * jnp.abs(delta_ref[:, 0:1])) * sign


def _filter_time(zt, aux, w1t, w2t, w3t, cols, w4t, delta, *, L, d_hy, n_order):
    fw = w2t.shape[0]
    lt = _pick(L, (1024, 512, 256, 128))
    nt = 2 * L // lt
    tr = _pick(d_hy, (512, 256, 128))
    nr = d_hy // tr
    const = lambda j, r: (0, 0)

    def w4_map(j, r):
        direction = (j >= nt // 2).astype(jnp.int32)
        return (((r // nr) * 2 + direction) * nr + r % nr, 0)

    return pl.pallas_call(
        _filter_time_kernel,
        grid=(nt, n_order * nr),
        in_specs=[
            pl.BlockSpec((zt.shape[0], lt), lambda j, r: (0, j)),
            pl.BlockSpec((SUBLANES, lt), lambda j, r: (0, j)),
            pl.BlockSpec(w1t.shape, const),
            pl.BlockSpec(w2t.shape, const),
            pl.BlockSpec(w3t.shape, const),
            pl.BlockSpec(cols.shape, const),
            pl.BlockSpec((tr, fw), w4_map),
            pl.BlockSpec((tr, LANES), lambda j, r: (r % nr, 0)),
        ],
        out_specs=pl.BlockSpec((tr, lt), lambda j, r: (r, j)),
        out_shape=jax.ShapeDtypeStruct((n_order * d_hy, 2 * L), F32),
        scratch_shapes=[pltpu.VMEM((fw, lt), BF16)],
        compiler_params=_params(("arbitrary", "arbitrary")),
        name="hy_filter_time",
    )(zt, aux, w1t, w2t, w3t, cols, w4t, delta)


def _filter_spec_kernel(k_ref, fr_ref, tw_ref, lf_ref, o_ref, a_ref, *, cb, gc):
    for c0 in range(0, cb, gc):
        _row_stage(fr_ref, [k_ref[c0 + q].astype(BF16) for q in range(gc)], tw_ref, a_ref, c0)
    for c0 in range(0, cb, gc):
        o_ref[pl.ds(c0, gc)] = _lane_stage(a_ref, lf_ref, c0, gc)


def _filter_spec(kt, tabs):
    C = kt.shape[0]
    A = tabs["A"]
    cb = _pick(C, (32, 16, 8))
    gc = 8
    kern = functools.partial(_filter_spec_kernel, cb=cb, gc=gc)
    return pl.pallas_call(
        kern,
        grid=(C // cb,),
        in_specs=[
            pl.BlockSpec((cb, 2 * A, LANES), lambda i: (i, 0, 0)),
            pl.BlockSpec((2 * A, 2 * A), lambda i: (0, 0)),
            pl.BlockSpec((2, A, LANES), lambda i: (0, 0, 0)),
            pl.BlockSpec((2 * LANES, 2 * LANES), lambda i: (0, 0)),
        ],
        out_specs=pl.BlockSpec((cb, A, 2 * LANES), lambda i: (i, 0, 0)),
        out_shape=jax.ShapeDtypeStruct((C, A, 2 * LANES), F32),
        scratch_shapes=[pltpu.VMEM((cb, A, 2 * LANES), BF16)],
        compiler_params=_params(("parallel",)),
        name="hy_filter_spec",
    )(kt.reshape(C, 2 * A, LANES), tabs["fr_full"], tabs["tw"], tabs["lf"])


def _shift_time(x, step):
    A = x.shape[1]
    lane = lax.broadcasted_iota(jnp.int32, x.shape, 2)
    row = lax.broadcasted_iota(jnp.int32, x.shape, 1)
    if step == 1:
        r = pltpu.roll(x, 1, axis=2)
        y = jnp.where(lane == 0, pltpu.roll(r, 1, axis=1), r)
        return jnp.where((lane == 0) & (row == 0), 0.0, y)
    r = pltpu.roll(x, LANES - 1, axis=2)
    y = jnp.where(lane == LANES - 1, pltpu.roll(r, A - 1, axis=1), r)
    return jnp.where((lane == LANES - 1) & (row == A - 1), 0.0, y)


def _hyena_kernel(v_ref, x1_ref, x2_ref, par_ref, ks0_ref, ks1_ref, fr_ref, tw_ref, lf_ref, lb_ref, gi_ref, o_ref,
                  hv_ref, hx1_ref, hx2_ref, ub_ref, a_ref, b_ref, *, cb, gc, cc):
    A = tw_ref.shape[1]
    groups = [g * gc for g in range(cb // gc)]

    def short_conv(s, src, dst, c0):
        par = par_ref[pl.ds(c0, cc)]
        x = src[pl.ds(c0, cc)]
        y = (_shift_time(x, 1) * par[:, 3 * s:3 * s + 1, :] + x * par[:, 3 * s + 1:3 * s + 2, :]
             + _shift_time(x, -1) * par[:, 3 * s + 2:3 * s + 3, :] + par[:, 9 + s:10 + s, :])
        dst[pl.ds(c0, cc)] = y
        return y

    for c0 in range(0, cb, cc):
        ub_ref[pl.ds(c0, cc)] = short_conv(0, v_ref, hv_ref, c0).astype(BF16)

    twr = tw_ref[0]
    twi = tw_ref[1]

    def long_conv(ks_ref, finish, side_work):
        for c0 in groups:
            _row_stage(fr_ref, [ub_ref[c0 + q] for q in range(gc)], tw_ref, a_ref, c0)
        for c0 in groups:
            x = _lane_stage(a_ref, lf_ref, c0, gc)
            xr = x[:, :, :LANES]
            xi = x[:, :, LANES:]
            kr = ks_ref[pl.ds(c0, gc), :, :LANES]
            ki = ks_ref[pl.ds(c0, gc), :, LANES:]
            b_ref[pl.ds(c0, gc), :, :LANES] = (xr * kr - xi * ki).astype(BF16)
            b_ref[pl.ds(c0, gc), :, LANES:] = (xr * ki + xi * kr).astype(BF16)
        for work in side_work:
            work()
        for c0 in groups:
            qv = _lane_stage(b_ref, lb_ref, c0, gc)
            qr = qv[:, :, :LANES]
            qi = qv[:, :, LANES:]
            a_ref[pl.ds(c0, gc), :, :LANES] = (qr * twr + qi * twi).astype(BF16)
            a_ref[pl.ds(c0, gc), :, LANES:] = (qi * twr - qr * twi).astype(BF16)
        for c0 in groups:
            rhs = jnp.concatenate(
                [jnp.concatenate([a_ref[c0 + q, :, :LANES], a_ref[c0 + q, :, LANES:]], axis=0) for q in range(gc)], axis=1)
            y = jnp.dot(gi_ref[...], rhs, preferred_element_type=F32)
            for q in range(gc):
                finish(c0 + q, y[:, q * LANES:(q + 1) * LANES])

    def finish0(c, y):
        z = hx1_ref[c] * (y + hv_ref[c] * par_ref[c, 12:13, :])
        hx1_ref[c] = z
        ub_ref[c] = z.astype(BF16)

    long_conv(ks0_ref, finish0,
              [functools.partial(short_conv, 1, x1_ref, hx1_ref, c0) for c0 in range(0, cb, cc)])

    def finish1(c, y):
        o_ref[c] = hx2_ref[c] * (y + hx1_ref[c] * par_ref[c, 13:14, :])

    long_conv(ks1_ref, finish1,
              [functools.partial(short_conv, 2, x2_ref, hx2_ref, c0) for c0 in range(0, cb, cc)])


def _hyena(hyt, par, ks, tabs, *, Bt, L, d_hy):
    A = tabs["A"]
    cb = _pick(d_hy, (32, 16, 8))
    gc = 8
    ncb = d_hy // cb
    x3 = hyt.reshape(3 * d_hy, Bt * A, LANES)
    kern = functools.partial(_hyena_kernel, cb=cb, gc=gc, cc=4)
    const2 = lambda j, b: (0, 0)
    tile = (cb, A, LANES)
    return pl.pallas_call(
        kern,
        grid=(ncb, Bt),
        in_specs=[
            pl.BlockSpec(tile, lambda j, b: (j, b, 0)),
            pl.BlockSpec(tile, lambda j, b: (ncb + j, b, 0)),
            pl.BlockSpec(tile, lambda j, b: (2 * ncb + j, b, 0)),
            pl.BlockSpec((cb, 16, LANES), lambda j, b: (j, 0, 0)),
            pl.BlockSpec((cb, A, 2 * LANES), lambda j, b: (j, 0, 0)),
            pl.BlockSpec((cb, A, 2 * LANES), lambda j, b: (ncb + j, 0, 0)),
            pl.BlockSpec((2 * A, A), const2),
            pl.BlockSpec((2, A, LANES), lambda j, b: (0, 0, 0)),
            pl.BlockSpec((2 * LANES, 2 * LANES), const2),
            pl.BlockSpec((2 * LANES, 2 * LANES), const2),
            pl.BlockSpec((A, 2 * A), const2),
        ],
        out_specs=pl.BlockSpec(tile, lambda j, b: (j, b, 0)),
        out_shape=jax.ShapeDtypeStruct((d_hy, Bt * A, LANES), F32),
        scratch_shapes=[
            pltpu.VMEM(tile, F32),
            pltpu.VMEM(tile, F32),
            pltpu.VMEM(tile, F32),
            pltpu.VMEM(tile, BF16),
            pltpu.VMEM((cb, A, 2 * LANES), BF16),
            pltpu.VMEM((cb, A, 2 * LANES), BF16),
        ],
        compiler_params=_params(("parallel", "arbitrary")),
        name="hyena",
    )(x3, x3, x3, par, ks, ks, tabs["fr_half"], tabs["tw"], tabs["lf"], tabs["lb"], tabs["gi"])


def _rg_norm_kernel(x_ref, g_ref, o_ref):
    o_ref[...] = _rms(x_ref[...], g_ref[...]).astype(BF16)


def _rg_norm(x, g):
    T, C = x.shape
    tm = _pick(T, (512, 256, 128))
    return pl.pallas_call(
        _rg_norm_kernel,
        grid=(T // tm,),
        in_specs=[pl.BlockSpec((tm, C), lambda i: (i, 0)), pl.BlockSpec((1, C), lambda i: (0, 0))],
        out_specs=pl.BlockSpec((tm, C), lambda i: (i, 0)),
        out_shape=jax.ShapeDtypeStruct((T, C), BF16),
        compiler_params=_params(("parallel",)),
        name="rg_norm",
    )(x, g.reshape(1, C))


def _hy_norm_kernel(hy_ref, g_ref, o_ref):
    d_hy, rows, _ = hy_ref.shape
    h = hy_ref[...].reshape(d_hy, rows * LANES)
    scale = lax.rsqrt(jnp.mean(h * h, axis=0, keepdims=True) + EPS)
    o_ref[...] = (h * scale * g_ref[:, 0:1]).T.astype(BF16)


def _hy_norm(hy3, g_col):
    d_hy, nrows, _ = hy3.shape
    rows = SUBLANES
    return pl.pallas_call(
        _hy_norm_kernel,
        grid=(nrows // rows,),
        in_specs=[pl.BlockSpec((d_hy, rows, LANES), lambda i: (0, i, 0)), pl.BlockSpec((d_hy, LANES), lambda i: (0, 0))],
        out_specs=pl.BlockSpec((rows * LANES, d_hy), lambda i: (i, 0)),
        out_shape=jax.ShapeDtypeStruct((nrows * LANES, d_hy), BF16),
        compiler_params=_params(("parallel",)),
        name="hy_norm",
    )(hy3, g_col)


def _post_mix_kernel(xa_ref, xb_ref, f_ref, g1_ref, g2_ref, x1_ref, hn_ref, *, n_first):
    x = jnp.where(pl.program_id(0) < n_first, xa_ref[0], xb_ref[0])
    x1 = x + _rms(f_ref[0], g1_ref[...])
    x1_ref[0] = x1
    hn_ref[0] = _rms(x1, g2_ref[...]).astype(BF16)


def _post_mix(xs, f, g1, g2, *, L):
    D = xs[0].shape[-1]
    n_first = xs[0].shape[0]
    Bt = sum(x.shape[0] for x in xs)
    tm = _pick(L, (256, 128))
    nt = L // tm
    block = (1, tm, D)
    index = lambda b, i: (b, i, 0)
    if len(xs) == 2:
        spec_a, spec_b = _pair_specs(block, n_first, nt - 1, index)
    else:
        spec_a = spec_b = pl.BlockSpec(block, index)
        xs = list(xs) * 2
    row = pl.BlockSpec(block, index)
    vec = pl.BlockSpec((1, D), lambda b, i: (0, 0))
    x1, hn = pl.pallas_call(
        functools.partial(_post_mix_kernel, n_first=n_first),
        grid=(Bt, nt),
        in_specs=[spec_a, spec_b, row, vec, vec],
        out_specs=[row, row],
        out_shape=[jax.ShapeDtypeStruct((Bt, L, D), F32), jax.ShapeDtypeStruct((Bt, L, D), BF16)],
        compiler_params=_params(("parallel", "parallel")),
        name="post_mix",
    )(xs[0], xs[1], f.reshape(Bt, L, D), g1.reshape(1, D), g2.reshape(1, D))
    return x1.reshape(Bt * L, D), hn.reshape(Bt * L, D)


def _post_ffn_pair_kernel(x_ref, f_ref, g_ref, ya_ref, yb_ref, *, n_first):
    y = x_ref[0] + _rms(f_ref[0], g_ref[...])

    @pl.when(pl.program_id(0) < n_first)
    def _():
        ya_ref[0] = y

    @pl.when(pl.program_id(0) >= n_first)
    def _():
        yb_ref[0] = y


def _post_ffn_kernel(x_ref, f_ref, g_ref, y_ref):
    y_ref[0] = x_ref[0] + _rms(f_ref[0], g_ref[...])


def _post_ffn(x, f, g, *, L, splits):
    T, D = x.shape
    Bt = T // L
    tm = _pick(L, (256, 128))
    nt = L // tm
    block = (1, tm, D)
    index = lambda b, i: (b, i, 0)
    row = pl.BlockSpec(block, index)
    args = (x.reshape(Bt, L, D), f.reshape(Bt, L, D), g.reshape(1, D))
    in_specs = [row, row, pl.BlockSpec((1, D), lambda b, i: (0, 0))]
    if len(splits) == 1:
        return (pl.pallas_call(
            _post_ffn_kernel,
            grid=(Bt, nt),
            in_specs=in_specs,
            out_specs=row,
            out_shape=jax.ShapeDtypeStruct((Bt, L, D), F32),
            compiler_params=_params(("parallel", "parallel")),
            name="post_ffn",
        )(*args),)
    n_first = splits[0]
    spec_a, spec_b = _pair_specs(block, n_first, nt - 1, index)
    return tuple(pl.pallas_call(
        functools.partial(_post_ffn_pair_kernel, n_first=n_first),
        grid=(Bt, nt),
        in_specs=in_specs,
        out_specs=[spec_a, spec_b],
        out_shape=[jax.ShapeDtypeStruct((n, L, D), F32) for n in splits],
        compiler_params=_params(("arbitrary", "arbitrary")),
        name="post_ffn",
    )(*args))


FFN_LAG = 2


def _ffn_in_kernel(a_ref, wg_ref, wu_ref, cw_ref, cb_ref, o_ref, g_ref, u_ref, *, tm, rc, n_tiles, tiles_per_seq):
    s = pl.program_id(0)
    ring = FFN_LAG + 1

    @pl.when(s == 0)
    def _():
        g_ref[...] = jnp.zeros_like(g_ref)
        u_ref[...] = jnp.zeros_like(u_ref)

    new = s % ring
    mid = (s + 1) % ring
    nxt = (s + 2) % ring

    e = jnp.maximum(s - FFN_LAG, 0) % n_tiles
    first = (e % tiles_per_seq) == 0
    last = (e % tiles_per_seq) == tiles_per_seq - 1
    cw = cw_ref[...]
    cb = cb_ref[...]
    pad = SUBLANES
    for r0 in range(0, tm, rc):
        a = a_ref[r0:r0 + rc, :]
        g_new = jnp.dot(a, wg_ref[...], preferred_element_type=F32)
        u_new = jnp.dot(a, wu_ref[...], preferred_element_type=F32)

        if r0 == 0:
            top = jnp.where(first, 0.0, g_ref[new, tm - pad:tm, :])
        else:
            top = g_ref[mid, r0 - pad:r0, :]
        if r0 + rc == tm:
            bottom = jnp.where(last, 0.0, g_ref[nxt, 0:pad, :])
        else:
            bottom = g_ref[mid, r0 + rc:r0 + rc + pad, :]
        big = jnp.concatenate([top, g_ref[mid, r0:r0 + rc, :], bottom], axis=0)
        y = (big[pad - 1:pad - 1 + rc] * cw[0:1, :] + big[pad:pad + rc] * cw[1:2, :]
             + big[pad + 1:pad + 1 + rc] * cw[2:3, :] + cb)
        o_ref[r0:r0 + rc, :] = (jax.nn.gelu(y) * u_ref[mid, r0:r0 + rc, :]).astype(BF16)

        g_ref[new, r0:r0 + rc, :] = g_new
        u_ref[new, r0:r0 + rc, :] = u_new


def _ffn_in(a, w, cw, cb, *, L, d_ff):
    T, D = a.shape
    tm = _pick(L, (1024, 512, 256, 128))
    tn = _pick(d_ff, (256, 128))
    ncol = d_ff // tn
    n_tiles = T // tm
    steps = ncol * n_tiles

    def mm_tile(s):
        return jnp.minimum(s, steps - 1)

    def ew_tile(s):
        return jnp.maximum(s - FFN_LAG, 0)

    kern = functools.partial(_ffn_in_kernel, tm=tm, rc=_pick(tm, (256, 128)), n_tiles=n_tiles, tiles_per_seq=L // tm)
    return pl.pallas_call(
        kern,
        grid=(steps + FFN_LAG,),
        in_specs=[
            pl.BlockSpec((tm, D), lambda s: (mm_tile(s) % n_tiles, 0)),
            pl.BlockSpec((D, tn), lambda s: (0, mm_tile(s) // n_tiles)),
            pl.BlockSpec((D, tn), lambda s: (0, ncol + mm_tile(s) // n_tiles)),
            pl.BlockSpec((3, tn), lambda s: (0, ew_tile(s) // n_tiles)),
            pl.BlockSpec((1, tn), lambda s: (0, ew_tile(s) // n_tiles)),
        ],
        out_specs=pl.BlockSpec((tm, tn), lambda s: (ew_tile(s) % n_tiles, ew_tile(s) // n_tiles)),
        out_shape=jax.ShapeDtypeStruct((T, d_ff), BF16),
        scratch_shapes=[pltpu.VMEM((FFN_LAG + 1, tm, tn), F32), pltpu.VMEM((FFN_LAG + 1, tm, tn), F32)],
        compiler_params=_params(("arbitrary",)),
        name="ffn_in",
    )(a, w, w, cw, cb.reshape(1, d_ff))


def _lane_bcast(v):
    return jnp.broadcast_to(v[..., None], v.shape + (LANES,))


def _hyena_filter_spectrum(L, d_hy, n_order, w1, b1, w2, b2, w3, b3, w4, freq, tabs):
    emb = w1.shape[0]
    bands = (emb - 1) // 2
    n = jnp.arange(2 * L)
    m = jnp.where(n < L, n, 2 * L - n).astype(F32)
    sign = jnp.where(n < L, 1.0, jnp.where(n == L, 0.0, -1.0)).astype(F32)
    t = m / (L - 1)
    band = jnp.linspace(1e-4, bands - 1, bands, dtype=F32)
    ang = (2.0 * math.pi / L) * m[None, :] * band[:, None]
    zt = jnp.concatenate([t[None, :], jnp.cos(ang), -jnp.sin(ang)], axis=0)
    kpad = -(-emb // BF16_ROWS) * BF16_ROWS
    zt = jnp.pad(zt, ((0, kpad - emb), (0, 0)))
    aux = jnp.zeros((SUBLANES, 2 * L), F32).at[0].set(sign).at[1].set(t)
    w1t = jnp.pad(w1.T, ((0, 0), (0, kpad - emb))).astype(BF16)
    fw = w2.shape[0]
    cols = jnp.zeros((fw, LANES), F32)
    cols = cols.at[:, 0].set(freq).at[:, 1].set(b1).at[:, 2].set(b2).at[:, 3].set(b3)
    max_decay = math.log(HY_TARGET) / HY_FAST_DECAY
    min_decay = math.log(HY_TARGET) / HY_SLOW_DECAY
    delta = _lane_bcast(jnp.linspace(min_decay, max_decay, d_hy, dtype=F32))
    kt = _filter_time(zt, aux, w1t, w2.T.astype(BF16), w3.T.astype(BF16), cols, w4.T.astype(BF16), delta,
                      L=L, d_hy=d_hy, n_order=n_order)
    return _filter_spec(kt, tabs)


def _encoder_layer(xs, p, splits):
    (pre_mix_norm, w_in, rg_conv_w, rg_conv_b, rg_a_w, rg_a_b, rg_x_w, rg_x_b, rg_lambda, hy_conv_w, hy_conv_b,
     hy_w1, hy_b1, hy_w2, hy_b2, hy_w3, hy_b3, hy_w4, hy_sin_freq, hy_bias, rg_out_norm, hy_out_norm, w_out,
     post_mix_norm, pre_ffn_norm, w_ffn_in, ffn_conv_w, ffn_conv_b, w_ffn_out, post_ffn_norm) = p
    _, L, D = xs[0].shape
    Bt = sum(x.shape[0] for x in xs)
    T = Bt * L
    d_rg = rg_conv_w.shape[-1]
    n_order, d_hy = hy_bias.shape
    d_ff = ffn_conv_w.shape[-1]
    nh = rg_a_w.shape[1]
    assert d_rg == nh * LANES and n_order == 2 and L % (LANES * BF16_ROWS) == 0

    hn = _norm_in(xs, pre_mix_norm, L=L)
    w_hyt = _transpose_cast(w_in, col0=2 * d_rg, ncols=3 * d_hy)
    tm = _pick(T, (1024, 512, 256, 128))
    w_rg = _cast_cols(w_in, col0=0, ncols=2 * d_rg)
    proj_rg = _mm(hn, w_rg, tm=tm, tn=_pick(2 * d_rg, (1024, 512, 256, 128)), name="proj_rg")
    hyt = _mm_nt(w_hyt, hn, tm=_pick(3 * d_hy, (1024, 512, 256, 128)), tn=tm, name="proj_hy")

    w_cat = jnp.concatenate([rg_a_w[0], rg_x_w[0], rg_a_w[1], rg_x_w[1]], axis=-1).astype(BF16)
    b_cat = jnp.concatenate([rg_a_b[0], rg_x_b[0], rg_a_b[1], rg_x_b[1]], axis=0)
    b_cat = b_cat.reshape(4, nh, LANES).transpose(1, 0, 2).reshape(nh, 1, 4 * LANES)
    rg_y = _rglru(proj_rg.reshape(Bt, L, 2 * d_rg), rg_conv_w, rg_conv_b.reshape(1, d_rg), w_cat, b_cat,
                  rg_lambda, d_rg=d_rg)

    tabs = _dft_tables(L)
    ks = _hyena_filter_spectrum(L, d_hy, n_order, hy_w1, hy_b1, hy_w2, hy_b2, hy_w3, hy_b3, hy_w4, hy_sin_freq, tabs)
    cw = hy_conv_w.reshape(3, 3, d_hy)
    par = jnp.concatenate([cw.transpose(1, 0, 2).reshape(9, d_hy), hy_conv_b.reshape(3, d_hy), hy_bias,
                           jnp.zeros((2, d_hy), F32)], axis=0)
    par = _lane_bcast(par.T)
    hy_y = _hyena(hyt, par, ks, tabs, Bt=Bt, L=L, d_hy=d_hy)

    mix_rg = _rg_norm(rg_y.reshape(T, d_rg), rg_out_norm)
    mix_hy = _hy_norm(hy_y, _lane_bcast(hy_out_norm))
    f = _mm2(mix_rg, mix_hy, w_out.astype(BF16), tm=tm, tn=_pick(D, (1024, 512, 256, 128)), name="out_proj")
    x1, hn2 = _post_mix(xs, f, post_mix_norm, pre_ffn_norm, L=L)

    fg = _ffn_in(hn2, w_ffn_in.astype(BF16), ffn_conv_w, ffn_conv_b, L=L, d_ff=d_ff)
    f2 = _mm(fg, w_ffn_out.astype(BF16), tm=_pick(T, (512, 256, 128)), tn=_pick(D, (512, 256, 128)), name="ffn_out")
    return _post_ffn(x1, f2, post_ffn_norm, L=L, splits=splits)


def _run_trunk(xs, params):
    splits = tuple(x.shape[0] for x in xs)
    depth = params[0].shape[0]
    for l in range(depth):
        last = l == depth - 1
        xs = _encoder_layer(xs, [q[l] for q in params], splits if last else (sum(splits),))
    return xs


def kernel(x_prompt, x_sample, pre_mix_norm, w_in, rg_conv_w, rg_conv_b, rg_a_w, rg_a_b, rg_x_w, rg_x_b, rg_lambda, hy_conv_w, hy_conv_b, hy_w1, hy_b1, hy_w2, hy_b2, hy_w3, hy_b3, hy_w4, hy_sin_freq, hy_bias, rg_out_norm, hy_out_norm, w_out, post_mix_norm, pre_ffn_norm, w_ffn_in, ffn_conv_w, ffn_conv_b, w_ffn_out, post_ffn_norm):
    params = (pre_mix_norm, w_in, rg_conv_w, rg_conv_b, rg_a_w, rg_a_b, rg_x_w, rg_x_b, rg_lambda, hy_conv_w,
              hy_conv_b, hy_w1, hy_b1, hy_w2, hy_b2, hy_w3, hy_b3, hy_w4, hy_sin_freq, hy_bias, rg_out_norm,
              hy_out_norm, w_out, post_mix_norm, pre_ffn_norm, w_ffn_in, ffn_conv_w, ffn_conv_b, w_ffn_out,
              post_ffn_norm)
    if x_prompt.shape[1:] == x_sample.shape[1:]:
        return _run_trunk((x_prompt, x_sample), params)
    return (_run_trunk((x_prompt,), params)[0], _run_trunk((x_sample,), params)[0])
```

```python
import functools
import math

import numpy as np
import jax
import jax.numpy as jnp
from jax import lax
from jax.experimental import pallas as pl
from jax.experimental.pallas import tpu as pltpu

F32 = jnp.float32
BF16 = jnp.bfloat16

EPS = 1e-6
TINY = 1e-30
RG_C = 8.0
RG_CONV_LEFT = 2
HY_FAST_DECAY = 0.3
HY_SLOW_DECAY = 1.5
HY_TARGET = 1e-2

LANES = 128
SUBLANES = 8
BF16_ROWS = 16
VMEM_LIMIT = 56 * 1024 * 1024


def _pick(n, candidates):
    for c in candidates:
        if n % c == 0:
            return c
    raise ValueError(f"no tile in {candidates} divides {n}")


def _params(sem):
    return pltpu.CompilerParams(dimension_semantics=sem, vmem_limit_bytes=VMEM_LIMIT)


def _rms(x, g):
    return x * lax.rsqrt(jnp.mean(x * x, axis=-1, keepdims=True) + EPS) * g


def _pair_specs(block, n_first, inner_last, make_index):
    def first(o, i):
        return make_index(jnp.minimum(o, n_first - 1), jnp.where(o < n_first, i, inner_last))

    def second(o, i):
        return make_index(jnp.maximum(o - n_first, 0), jnp.where(o >= n_first, i, 0))

    return pl.BlockSpec(block, first), pl.BlockSpec(block, second)


def _norm_in_kernel(xa_ref, xb_ref, g_ref, hn_ref, *, n_first):
    x = jnp.where(pl.program_id(0) < n_first, xa_ref[0], xb_ref[0])
    hn_ref[0] = _rms(x, g_ref[...]).astype(BF16)


def _norm_in(xs, g, *, L):
    D = xs[0].shape[-1]
    n_first = xs[0].shape[0]
    Bt = sum(x.shape[0] for x in xs)
    tm = _pick(L, (256, 128))
    nt = L // tm
    block = (1, tm, D)
    index = lambda b, i: (b, i, 0)
    if len(xs) == 2:
        spec_a, spec_b = _pair_specs(block, n_first, nt - 1, index)
    else:
        spec_a = spec_b = pl.BlockSpec(block, index)
        xs = list(xs) * 2
    hn = pl.pallas_call(
        functools.partial(_norm_in_kernel, n_first=n_first),
        grid=(Bt, nt),
        in_specs=[spec_a, spec_b, pl.BlockSpec((1, D), lambda b, i: (0, 0))],
        out_specs=pl.BlockSpec(block, index),
        out_shape=jax.ShapeDtypeStruct((Bt, L, D), BF16),
        compiler_params=_params(("parallel", "parallel")),
        name="norm_in",
    )(xs[0], xs[1], g.reshape(1, D))
    return hn.reshape(Bt * L, D)


def _mm_kernel(a_ref, b_ref, o_ref):
    o_ref[...] = jnp.dot(a_ref[...], b_ref[...], preferred_element_type=F32).astype(o_ref.dtype)


def _mm(a, b, *, tm, tn, name):
    M, K = a.shape
    _, N = b.shape
    return pl.pallas_call(
        _mm_kernel,
        grid=(N // tn, M // tm),
        in_specs=[pl.BlockSpec((tm, K), lambda j, i: (i, 0)), pl.BlockSpec((K, tn), lambda j, i: (0, j))],
        out_specs=pl.BlockSpec((tm, tn), lambda j, i: (i, j)),
        out_shape=jax.ShapeDtypeStruct((M, N), F32),
        compiler_params=_params(("parallel", "parallel")),
        name=name,
    )(a, b)


def _mm2_kernel(a1_ref, a2_ref, b_ref, o_ref):
    k1 = a1_ref.shape[1]
    acc = jnp.dot(a1_ref[...], b_ref[:k1, :], preferred_element_type=F32)
    o_ref[...] = acc + jnp.dot(a2_ref[...], b_ref[k1:, :], preferred_element_type=F32)


def _mm2(a1, a2, b, *, tm, tn, name):
    M, K1 = a1.shape
    K2 = a2.shape[1]
    N = b.shape[1]
    return pl.pallas_call(
        _mm2_kernel,
        grid=(N // tn, M // tm),
        in_specs=[pl.BlockSpec((tm, K1), lambda j, i: (i, 0)), pl.BlockSpec((tm, K2), lambda j, i: (i, 0)),
                  pl.BlockSpec((K1 + K2, tn), lambda j, i: (0, j))],
        out_specs=pl.BlockSpec((tm, tn), lambda j, i: (i, j)),
        out_shape=jax.ShapeDtypeStruct((M, N), F32),
        compiler_params=_params(("parallel", "parallel")),
        name=name,
    )(a1, a2, b)


def _mm_nt_kernel(a_ref, bt_ref, o_ref):
    o_ref[...] = lax.dot_general(a_ref[...], bt_ref[...], (((1,), (1,)), ((), ())), preferred_element_type=F32)


def _mm_nt(a, bt, *, tm, tn, name):
    M, K = a.shape
    N = bt.shape[0]
    return pl.pallas_call(
        _mm_nt_kernel,
        grid=(N // tn, M // tm),
        in_specs=[pl.BlockSpec((tm, K), lambda j, i: (i, 0)), pl.BlockSpec((tn, K), lambda j, i: (j, 0))],
        out_specs=pl.BlockSpec((tm, tn), lambda j, i: (i, j)),
        out_shape=jax.ShapeDtypeStruct((M, N), F32),
        compiler_params=_params(("parallel", "parallel")),
        name=name,
    )(a, bt)


def _cast_kernel(w_ref, o_ref):
    o_ref[...] = w_ref[...].astype(BF16)


def _cast_cols(w, *, col0, ncols):
    K = w.shape[0]
    tc = _pick(ncols, (512, 256, 128))
    assert col0 % tc == 0
    return pl.pallas_call(
        _cast_kernel,
        grid=(ncols // tc,),
        in_specs=[pl.BlockSpec((K, tc), lambda j: (0, col0 // tc + j))],
        out_specs=pl.BlockSpec((K, tc), lambda j: (0, j)),
        out_shape=jax.ShapeDtypeStruct((K, ncols), BF16),
        compiler_params=_params(("parallel",)),
        name="w_cast",
    )(w)


def _transpose_cast_kernel(w_ref, o_ref):
    o_ref[...] = w_ref[...].T.astype(BF16)


def _transpose_cast(w, *, col0, ncols):
    K = w.shape[0]
    tc = _pick(ncols, (256, 128))
    assert col0 % tc == 0
    return pl.pallas_call(
        _transpose_cast_kernel,
        grid=(ncols // tc,),
        in_specs=[pl.BlockSpec((K, tc), lambda j: (0, col0 // tc + j))],
        out_specs=pl.BlockSpec((tc, K), lambda j: (j, 0)),
        out_shape=jax.ShapeDtypeStruct((ncols, K), BF16),
        compiler_params=_params(("parallel",)),
        name="w_transpose",
    )(w)


def _scan_chunk(a, b, carry, reverse):
    R = a.shape[0]
    nv = R // SUBLANES
    a3 = a.reshape(nv, SUBLANES, LANES)
    b3 = b.reshape(nv, SUBLANES, LANES)
    row = lax.broadcasted_iota(jnp.int32, (nv, SUBLANES, LANES), 1)
    for d in (1, 2, 4):
        shift = SUBLANES - d if reverse else d
        valid = (row < SUBLANES - d) if reverse else (row >= d)
        sa = pltpu.roll(a3, shift, axis=1)
        sb = pltpu.roll(b3, shift, axis=1)
        b3 = b3 + jnp.where(valid, a3, 0.0) * sb
        a3 = a3 * jnp.where(valid, sa, 1.0)
    edge = 0 if reverse else SUBLANES - 1
    a_edge = jnp.broadcast_to(a3[:, edge:edge + 1, :], a3.shape)
    b_edge = jnp.broadcast_to(b3[:, edge:edge + 1, :], b3.shape)
    order = range(nv - 1, -1, -1) if reverse else range(nv)
    carries = [None] * nv
    for j in order:
        carries[j] = carry
        carry = b_edge[j] + a_edge[j] * carry
    h3 = b3 + a3 * jnp.stack(carries)
    return h3.reshape(R, LANES), carry


def _rglru_kernel(x_ref, g_ref, cw_ref, cb_ref, w_ref, bias_ref, lam_ref, o_ref, xp_ref, xc_ref, hf_ref, *, L, R):
    pad = SUBLANES
    zeros = jnp.zeros((pad, LANES), F32)
    xp_ref[0:pad, :] = zeros
    xp_ref[L + pad:L + 2 * pad, :] = zeros
    xp_ref[pad:L + pad, :] = x_ref[0]
    cw = cw_ref[...]
    cb = cb_ref[...]
    lam = lam_ref[...]
    decay = RG_C * jax.nn.softplus(-lam)
    nc = L // R
    hw = 2 * LANES

    def gates(xc, d):
        z = jnp.dot(xc.astype(BF16), w_ref[0, :, d * hw:(d + 1) * hw], preferred_element_type=F32)
        z = z + bias_ref[0, :, d * hw:(d + 1) * hw]
        r = jax.nn.sigmoid(z[:, :LANES])
        i = jax.nn.sigmoid(z[:, LANES:])
        s = r * decay[d:d + 1, :]
        a = jnp.exp(-s)
        q = jnp.tanh(s) * (1.0 + a * a)
        b = (q * lax.rsqrt(jnp.maximum(q, TINY))) * (i * xc)
        return a, b

    def fwd(c, carry):
        base = pl.multiple_of(c * R, R)
        big = xp_ref[pl.ds(base, R + 2 * pad), :]
        xc = cb
        for k in range(cw.shape[0]):
            off = pad + k - RG_CONV_LEFT
            xc = xc + big[off:off + R, :] * cw[k:k + 1, :]
        xc_ref[pl.ds(base, R), :] = xc
        a, b = gates(xc, 0)
        h, carry = _scan_chunk(a, b, carry, False)
        hf_ref[pl.ds(base, R), :] = h
        return carry

    lax.fori_loop(0, nc, fwd, jnp.zeros((SUBLANES, LANES), F32))

    def bwd(s, carry):
        base = pl.multiple_of((nc - 1 - s) * R, R)
        xc = xc_ref[pl.ds(base, R), :]
        a, b = gates(xc, 1)
        h, carry = _scan_chunk(a, b, carry, True)
        gate = jax.nn.gelu(g_ref[0, pl.ds(base, R), :])
        o_ref[0, pl.ds(base, R), :] = (hf_ref[pl.ds(base, R), :] + h) * gate
        return carry

    lax.fori_loop(0, nc, bwd, jnp.zeros((SUBLANES, LANES), F32))


def _rglru(proj, cw, cb, w_cat, b_cat, lam, *, d_rg):
    Bt, L, _ = proj.shape
    nh = d_rg // LANES
    R = _pick(L, (512, 256, 128, 64, 32, 16, 8))
    kern = functools.partial(_rglru_kernel, L=L, R=R)
    return pl.pallas_call(
        kern,
        grid=(Bt, nh),
        in_specs=[
            pl.BlockSpec((1, L, LANES), lambda b, h: (b, 0, h)),
            pl.BlockSpec((1, L, LANES), lambda b, h: (b, 0, nh + h)),
            pl.BlockSpec((cw.shape[0], LANES), lambda b, h: (0, h)),
            pl.BlockSpec((1, LANES), lambda b, h: (0, h)),
            pl.BlockSpec((1, LANES, 4 * LANES), lambda b, h: (h, 0, 0)),
            pl.BlockSpec((1, 1, 4 * LANES), lambda b, h: (h, 0, 0)),
            pl.BlockSpec((2, LANES), lambda b, h: (0, h)),
        ],
        out_specs=pl.BlockSpec((1, L, LANES), lambda b, h: (b, 0, h)),
        out_shape=jax.ShapeDtypeStruct((Bt, L, d_rg), F32),
        scratch_shapes=[
            pltpu.VMEM((L + 2 * SUBLANES, LANES), F32),
            pltpu.VMEM((L, LANES), F32),
            pltpu.VMEM((L, LANES), F32),
        ],
        compiler_params=_params(("parallel", "parallel")),
        name="rglru",
    )(proj, proj, cw, cb, w_cat, b_cat, lam)


def _rglru_proj_kernel(x_ref, xprev_ref, xnext_ref, g_ref, cw_ref, cb_ref, w_ref, bias_ref, lam_ref, wa_ref, hb_ref,
                       y_ref, hy_ref, xc_ref, hf_ref, carry_ref, *, QR, R, nq, nmm):
    q = pl.program_id(2)
    pad = SUBLANES
    nsub = QR // R
    mrows = wa_ref.shape[0] // nmm
    per_mm = nsub // nmm
    cw = cw_ref[...]
    cb = cb_ref[...]
    decay = RG_C * jax.nn.softplus(-lam_ref[...])
    hw = 2 * LANES

    def gates(xc, d):
        z = jnp.dot(xc.astype(BF16), w_ref[0, :, d * hw:(d + 1) * hw], preferred_element_type=F32)
        z = z + bias_ref[0, :, d * hw:(d + 1) * hw]
        r = jax.nn.sigmoid(z[:, :LANES])
        i = jax.nn.sigmoid(z[:, LANES:])
        s = r * decay[d:d + 1, :]
        a = jnp.exp(-s)
        v = jnp.tanh(s) * (1.0 + a * a)
        b = (v * lax.rsqrt(jnp.maximum(v, TINY))) * (i * xc)
        return a, b

    def proj_rows(k):
        rows = slice(k * mrows, (k + 1) * mrows)
        hy_ref[rows, :] = lax.dot_general(wa_ref[rows, :], hb_ref[...], (((1,), (1,)), ((), ())),
                                          preferred_element_type=F32)

    @pl.when(q < nq)
    def _():
        carry = jnp.where(q == 0, 0.0, carry_ref[...])
        base = pl.multiple_of(q * QR, QR)
        for k in range(nsub):
            if k % per_mm == 0:
                proj_rows(k // per_mm)
            r0 = k * R
            top =jnp.where(q == 0, 0.0, xprev_ref[0]) if k == 0 else x_ref[0, r0 - pad:r0, :]
            bottom = jnp.where(q == nq - 1, 0.0, xnext_ref[0]) if k == nsub - 1 else x_ref[0, r0 + R:r0 + R + pad, :]
            big = jnp.concatenate([top, x_ref[0, r0:r0 + R, :], bottom], axis=0)
            xc = cb
            for t in range(cw.shape[0]):
                off = pad + t - RG_CONV_LEFT
                xc = xc + big[off:off + R, :] * cw[t:t + 1, :]
            xc_ref[pl.ds(base + r0, R), :] = xc
            a, b = gates(xc, 0)
            h, carry = _scan_chunk(a, b, carry, False)
            hf_ref[pl.ds(base + r0, R), :] = h
        carry_ref[...] = carry

    @pl.when(q >= nq)
    def _():
        carry = jnp.where(q == nq, 0.0, carry_ref[...])
        base = pl.multiple_of((2 * nq - 1 - q) * QR, QR)
        for k in range(nsub):
            if k % per_mm == 0:
                proj_rows(k // per_mm)
            r0 = (nsub - 1 - k) * R
            a, b = gates(xc_ref[pl.ds(base + r0, R), :], 1)
            h, carry = _scan_chunk(a, b, carry, True)
            gate = jax.nn.gelu(g_ref[0, r0:r0 + R, :])
            y_ref[0, r0:r0 + R, :] = (hf_ref[pl.ds(base + r0, R), :] + h) * gate
        carry_ref[...] = carry


def _rglru_proj_tiles(n_steps, rows, cols, nsub):
    best = None
    for tm in range(nsub * BF16_ROWS, rows + 1, nsub * BF16_ROWS):
        if rows % tm or n_steps % (rows // tm):
            continue
        ncols = n_steps // (rows // tm)
        if cols % ncols or (cols // ncols) % LANES:
            continue
        tn = cols // ncols
        score = (min(tn, 2 * LANES), tm)
        if tm * tn <= 512 * 1024 and (best is None or score > best[0]):
            best = (score, (tm, tn))
    return None if best is None else best[1]


def _rglru_proj(proj, cw, cb, w_cat, b_cat, lam, w_hyt, hn, *, d_rg):
    Bt, L, _ = proj.shape
    nh = d_rg // LANES
    nq = 4
    QR = L // nq
    R = _pick(QR, (512, 256, 128, 64, 32, 16, 8))
    nsub = QR // R
    nmm = _pick(nsub, (4, 2, 1))
    n_steps = Bt * nh * 2 * nq
    rows, K = w_hyt.shape
    T = hn.shape[0]
    tiles = _rglru_proj_tiles(n_steps, rows, T, nmm)
    assert tiles is not None
    tmh, tnh = tiles
    ncols = T // tnh
    qb = QR // SUBLANES

    def step(b, h, q):
        return (b * nh + h) * (2 * nq) + q

    def fq(q):
        return jnp.minimum(q, nq - 1)

    def bq(q):
        return jnp.where(q < nq, nq - 1, 2 * nq - 1 - q)

    kern = functools.partial(_rglru_proj_kernel, QR=QR, R=R, nq=nq, nmm=nmm)
    return pl.pallas_call(
        kern,
        grid=(Bt, nh, 2 * nq),
        in_specs=[
            pl.BlockSpec((1, QR, LANES), lambda b, h, q: (b, fq(q), h)),
            pl.BlockSpec((1, SUBLANES, LANES), lambda b, h, q: (b, jnp.maximum(fq(q) * qb - 1, 0), h)),
            pl.BlockSpec((1, SUBLANES, LANES), lambda b, h, q: (b, jnp.minimum((fq(q) + 1) * qb, L // SUBLANES - 1), h)),
            pl.BlockSpec((1, QR, LANES), lambda b, h, q: (b, bq(q), nh + h)),
            pl.BlockSpec((cw.shape[0], LANES), lambda b, h, q: (0, h)),
            pl.BlockSpec((1, LANES), lambda b, h, q: (0, h)),
            pl.BlockSpec((1, LANES, 4 * LANES), lambda b, h, q: (h, 0, 0)),
            pl.BlockSpec((1, 1, 4 * LANES), lambda b, h, q: (h, 0, 0)),
            pl.BlockSpec((2, LANES), lambda b, h, q: (0, h)),
            pl.BlockSpec((tmh, K), lambda b, h, q: (step(b, h, q) // ncols, 0)),
            pl.BlockSpec((tnh, K), lambda b, h, q: (step(b, h, q) % ncols, 0)),
        ],
        out_specs=[
            pl.BlockSpec((1, QR, LANES), lambda b, h, q: (b, bq(q), h)),
            pl.BlockSpec((tmh, tnh), lambda b, h, q: (step(b, h, q) // ncols, step(b, h, q) % ncols)),
        ],
        out_shape=[jax.ShapeDtypeStruct((Bt, L, d_rg), F32), jax.ShapeDtypeStruct((rows, T), F32)],
        scratch_shapes=[pltpu.VMEM((L, LANES), F32), pltpu.VMEM((L, LANES), F32), pltpu.VMEM((SUBLANES, LANES), F32)],
        compiler_params=_params(("arbitrary", "arbitrary", "arbitrary")),
        name="rglru_proj_hy",
    )(proj, proj, proj, proj, cw, cb, w_cat, b_cat, lam, w_hyt, hn)


def _dft_tables(L):
    n = 2 * L
    A = L // LANES
    nb = np.arange(2 * A, dtype=np.float64)[None, :]
    kb = np.arange(A, dtype=np.float64)[:, None] + 0.5
    phi = 2.0 * np.pi * kb * nb / (2 * A)
    fr_full = np.concatenate([np.cos(phi), -np.sin(phi)], 0)
    na = np.arange(LANES, dtype=np.float64)[None, :]
    psi = 2.0 * np.pi * kb * na / n
    tw = np.stack([np.cos(psi), -np.sin(psi)])
    aa = 2.0 * np.pi * np.outer(np.arange(LANES), np.arange(LANES)) / LANES
    c, s = np.cos(aa), np.sin(aa)
    lf = np.block([[c, -s], [s, c]])
    lb = np.block([[c, s], [-s, c]])
    theta = phi[:, :A].T
    gi = np.concatenate([np.cos(theta), -np.sin(theta)], 1) * (2.0 / n)
    return dict(
        A=A,
        fr_full=jnp.asarray(fr_full, BF16), fr_half=jnp.asarray(fr_full[:, :A], BF16),
        tw=jnp.asarray(tw, F32), lf=jnp.asarray(lf, BF16), lb=jnp.asarray(lb, BF16), gi=jnp.asarray(gi, BF16),
    )


def _row_stage(fr_ref, tiles, tw_ref, dst_ref, c0):
    A = tw_ref.shape[1]
    y = jnp.dot(fr_ref[...], jnp.concatenate(tiles, axis=1), preferred_element_type=F32)
    twr = tw_ref[0]
    twi = tw_ref[1]
    for q in range(len(tiles)):
        yr = y[:A, q * LANES:(q + 1) * LANES]
        yi = y[A:, q * LANES:(q + 1) * LANES]
        dst_ref[c0 + q, :, :LANES] = (yr * twr - yi * twi).astype(BF16)
        dst_ref[c0 + q, :, LANES:] = (yr * twi + yi * twr).astype(BF16)


def _lane_stage(src_ref, m_ref, c0, gc):
    A = src_ref.shape[1]
    a2 = src_ref[pl.ds(c0, gc)].reshape(gc * A, 2 * LANES)
    return jnp.dot(a2, m_ref[...], preferred_element_type=F32).reshape(gc, A, 2 * LANES)


def _filter_time_kernel(zt_ref, aux_ref, w1_ref, w2_ref, w3_ref, col_ref, w4_ref, delta_ref, o_ref, h3_ref):
    @pl.when(pl.program_id(1) == 0)
    def _():
        fr = col_ref[:, 0:1]
        h = jnp.sin(fr * (jnp.dot(w1_ref[...], zt_ref[...].astype(BF16), preferred_element_type=F32) + col_ref[:, 1:2]))
        h = jnp.sin(fr * (jnp.dot(w2_ref[...], h.astype(BF16), preferred_element_type=F32) + col_ref[:, 2:3]))
        h = jnp.sin(fr * (jnp.dot(w3_ref[...], h.astype(BF16), preferred_element_type=F32) + col_ref[:, 3:4]))
        h3_ref[...] = h.astype(BF16)

    sign = aux_ref[0:1, :]
    t = aux_ref[1:2, :]
    k = jnp.dot(w4_ref[...], h3_ref[...], preferred_element_type=F32)
    o_ref[...] = k * jnp.exp(-t * jnp.abs(delta_ref[:, 0:1])) * sign


def _filter_time(zt, aux, w1t, w2t, w3t, cols, w4t, delta, *, L, d_hy, n_order):
    fw = w2t.shape[0]
    lt = _pick(L, (1024, 512, 256, 128))
    nt = 2 * L // lt
    tr = _pick(d_hy, (512, 256, 128))
    nr = d_hy // tr
    const = lambda j, r: (0, 0)

    def w4_map(j, r):
        direction = (j >= nt // 2).astype(jnp.int32)
        return (((r // nr) * 2 + direction) * nr + r % nr, 0)

    return pl.pallas_call(
        _filter_time_kernel,
        grid=(nt, n_order * nr),
        in_specs=[
            pl.BlockSpec((zt.shape[0], lt), lambda j, r: (0, j)),
            pl.BlockSpec((SUBLANES, lt), lambda j, r: (0, j)),
            pl.BlockSpec(w1t.shape, const),
            pl.BlockSpec(w2t.shape, const),
            pl.BlockSpec(w3t.shape, const),
            pl.BlockSpec(cols.shape, const),
            pl.BlockSpec((tr, fw), w4_map),
            pl.BlockSpec((tr, LANES), lambda j, r: (r % nr, 0)),
        ],
        out_specs=pl.BlockSpec((tr, lt), lambda j, r: (r, j)),
        out_shape=jax.ShapeDtypeStruct((n_order * d_hy, 2 * L), F32),
        scratch_shapes=[pltpu.VMEM((fw, lt), BF16)],
        compiler_params=_params(("arbitrary", "arbitrary")),
        name="hy_filter_time",
    )(zt, aux, w1t, w2t, w3t, cols, w4t, delta)


def _filter_spec_kernel(k_ref, fr_ref, tw_ref, lf_ref, o_ref, a_ref, *, cb, gc):
    rows = fr_ref.shape[1]
    for c0 in range(0, cb, gc):
        k = k_ref[pl.ds(c0, gc), :].reshape(gc, rows, LANES).astype(BF16)
        _row_stage(fr_ref, [k[q] for q in range(gc)], tw_ref, a_ref, c0)
    for c0 in range(0, cb, gc):
        o_ref[pl.ds(c0, gc)] = _lane_stage(a_ref, lf_ref, c0, gc)


def _filter_spec(kt, tabs):
    C = kt.shape[0]
    A = tabs["A"]
    cb = _pick(C, (32, 16, 8))
    gc = 8
    kern = functools.partial(_filter_spec_kernel, cb=cb, gc=gc)
    return pl.pallas_call(
        kern,
        grid=(C // cb,),
        in_specs=[
            pl.BlockSpec((cb, 2 * A * LANES), lambda i: (i, 0)),
            pl.BlockSpec((2 * A, 2 * A), lambda i: (0, 0)),
            pl.BlockSpec((2, A, LANES), lambda i: (0, 0, 0)),
            pl.BlockSpec((2 * LANES, 2 * LANES), lambda i: (0, 0)),
        ],
        out_specs=pl.BlockSpec((cb, A, 2 * LANES), lambda i: (i, 0, 0)),
        out_shape=jax.ShapeDtypeStruct((C, A, 2 * LANES), F32),
        scratch_shapes=[pltpu.VMEM((cb, A, 2 * LANES), BF16)],
        compiler_params=_params(("parallel",)),
        name="hy_filter_spec",
    )(kt, tabs["fr_full"], tabs["tw"], tabs["lf"])


def _shift_time(x, step):
    A = x.shape[1]
    lane = lax.broadcasted_iota(jnp.int32, x.shape, 2)
    row = lax.broadcasted_iota(jnp.int32, x.shape, 1)
    if step == 1:
        r = pltpu.roll(x, 1, axis=2)
        y = jnp.where(lane == 0, pltpu.roll(r, 1, axis=1), r)
        return jnp.where((lane == 0) & (row == 0), 0.0, y)
    r = pltpu.roll(x, LANES - 1, axis=2)
    y = jnp.where(lane == LANES - 1, pltpu.roll(r, A - 1, axis=1), r)
    return jnp.where((lane == LANES - 1) & (row == A - 1), 0.0, y)


def _hyena_kernel(v_ref, x1_ref, x2_ref, par_ref, ks0_ref, ks1_ref, fr_ref, tw_ref, lf_ref, lb_ref, gi_ref, o_ref,
                  hv_ref, hx1_ref, hx2_ref, ub_ref, a_ref, b_ref, *, cb, gc, cc):
    A = tw_ref.shape[1]
    groups = [g * gc for g in range(cb // gc)]

    def short_conv(s, src, dst, c0):
        par = par_ref[pl.ds(c0, cc)]
        x = src[pl.ds(c0, cc), :].reshape(cc, A, LANES)
        y = (_shift_time(x, 1) * par[:, 3 * s:3 * s + 1, :] + x * par[:, 3 * s + 1:3 * s + 2, :]
             + _shift_time(x, -1) * par[:, 3 * s + 2:3 * s + 3, :] + par[:, 9 + s:10 + s, :])
        dst[pl.ds(c0, cc)] = y
        return y

    for c0 in range(0, cb, cc):
        ub_ref[pl.ds(c0, cc)] = short_conv(0, v_ref, hv_ref, c0).astype(BF16)

    twr = tw_ref[0]
    twi = tw_ref[1]

    def long_conv(ks_ref, finish, side_work):
        for c0 in groups:
            _row_stage(fr_ref, [ub_ref[c0 + q] for q in range(gc)], tw_ref, a_ref, c0)
        for c0 in groups:
            x = _lane_stage(a_ref, lf_ref, c0, gc)
            xr = x[:, :, :LANES]
            xi = x[:, :, LANES:]
            kr = ks_ref[pl.ds(c0, gc), :, :LANES]
            ki = ks_ref[pl.ds(c0, gc), :, LANES:]
            b_ref[pl.ds(c0, gc), :, :LANES] = (xr * kr - xi * ki).astype(BF16)
            b_ref[pl.ds(c0, gc), :, LANES:] = (xr * ki + xi * kr).astype(BF16)
        for work in side_work:
            work()
        for c0 in groups:
            qv = _lane_stage(b_ref, lb_ref, c0, gc)
            qr = qv[:, :, :LANES]
            qi = qv[:, :, LANES:]
            a_ref[pl.ds(c0, gc), :, :LANES] = (qr * twr + qi * twi).astype(BF16)
            a_ref[pl.ds(c0, gc), :, LANES:] = (qi * twr - qr * twi).astype(BF16)
        for c0 in groups:
            rhs = jnp.concatenate(
                [jnp.concatenate([a_ref[c0 + q, :, :LANES], a_ref[c0 + q, :, LANES:]], axis=0) for q in range(gc)], axis=1)
            y = jnp.dot(gi_ref[...], rhs, preferred_element_type=F32)
            for q in range(gc):
                finish(c0 + q, y[:, q * LANES:(q + 1) * LANES])

    def finish0(c, y):
        z = hx1_ref[c] * (y + hv_ref[c] * par_ref[c, 12:13, :])
        hx1_ref[c] = z
        ub_ref[c] = z.astype(BF16)

    long_conv(ks0_ref, finish0,
              [functools.partial(short_conv, 1, x1_ref, hx1_ref, c0) for c0 in range(0, cb, cc)])

    def finish1(c, y):
        o_ref[c] = hx2_ref[c] * (y + hx1_ref[c] * par_ref[c, 13:14, :])

    long_conv(ks1_ref, finish1,
              [functools.partial(short_conv, 2, x2_ref, hx2_ref, c0) for c0 in range(0, cb, cc)])


def _hyena(hyt, par, ks, tabs, *, Bt, L, d_hy):
    A = tabs["A"]
    cb = _pick(d_hy, (32, 16, 8))
    gc = 8
    ncb = d_hy // cb
    kern = functools.partial(_hyena_kernel, cb=cb, gc=gc, cc=SUBLANES)
    const2 = lambda j, b: (0, 0)
    tile = (cb, A, LANES)
    return pl.pallas_call(
        kern,
        grid=(ncb, Bt),
        in_specs=[
            pl.BlockSpec((cb, L), lambda j, b: (j, b)),
            pl.BlockSpec((cb, L), lambda j, b: (ncb + j, b)),
            pl.BlockSpec((cb, L), lambda j, b: (2 * ncb + j, b)),
            pl.BlockSpec((cb, 16, LANES), lambda j, b: (j, 0, 0)),
            pl.BlockSpec((cb, A, 2 * LANES), lambda j, b: (j, 0, 0)),
            pl.BlockSpec((cb, A, 2 * LANES), lambda j, b: (ncb + j, 0, 0)),
            pl.BlockSpec((2 * A, A), const2),
            pl.BlockSpec((2, A, LANES), lambda j, b: (0, 0, 0)),
            pl.BlockSpec((2 * LANES, 2 * LANES), const2),
            pl.BlockSpec((2 * LANES, 2 * LANES), const2),
            pl.BlockSpec((A, 2 * A), const2),
        ],
        out_specs=pl.BlockSpec(tile, lambda j, b: (j, b, 0)),
        out_shape=jax.ShapeDtypeStruct((d_hy, Bt * A, LANES), F32),
        scratch_shapes=[
            pltpu.VMEM(tile, F32),
            pltpu.VMEM(tile, F32),
            pltpu.VMEM(tile, F32),
            pltpu.VMEM(tile, BF16),
            pltpu.VMEM((cb, A, 2 * LANES), BF16),
            pltpu.VMEM((cb, A, 2 * LANES), BF16),
        ],
        compiler_params=_params(("parallel", "arbitrary")),
        name="hyena",
    )(hyt, hyt, hyt, par, ks, ks, tabs["fr_half"], tabs["tw"], tabs["lf"], tabs["lb"], tabs["gi"])


def _rg_norm_kernel(x_ref, g_ref, o_ref):
    o_ref[...] = _rms(x_ref[...], g_ref[...]).astype(BF16)


def _rg_norm(x, g):
    T, C = x.shape
    tm = _pick(T, (512, 256, 128))
    return pl.pallas_call(
        _rg_norm_kernel,
        grid=(T // tm,),
        in_specs=[pl.BlockSpec((tm, C), lambda i: (i, 0)), pl.BlockSpec((1, C), lambda i: (0, 0))],
        out_specs=pl.BlockSpec((tm, C), lambda i: (i, 0)),
        out_shape=jax.ShapeDtypeStruct((T, C), BF16),
        compiler_params=_params(("parallel",)),
        name="rg_norm",
    )(x, g.reshape(1, C))


def _hy_norm_kernel(hy_ref, g_ref, o_ref):
    d_hy, rows, _ = hy_ref.shape
    h = hy_ref[...].reshape(d_hy, rows * LANES)
    scale = lax.rsqrt(jnp.mean(h * h, axis=0, keepdims=True) + EPS)
    o_ref[...] = (h * scale * g_ref[:, 0:1]).T.astype(BF16)


def _hy_norm(hy3, g_col):
    d_hy, nrows, _ = hy3.shape
    rows = SUBLANES
    return pl.pallas_call(
        _hy_norm_kernel,
        grid=(nrows // rows,),
        in_specs=[pl.BlockSpec((d_hy, rows, LANES), lambda i: (0, i, 0)), pl.BlockSpec((d_hy, LANES), lambda i: (0, 0))],
        out_specs=pl.BlockSpec((rows * LANES, d_hy), lambda i: (i, 0)),
        out_shape=jax.ShapeDtypeStruct((nrows * LANES, d_hy), BF16),
        compiler_params=_params(("parallel",)),
        name="hy_norm",
    )(hy3, g_col)


def _post_mix_kernel(xa_ref, xb_ref, f_ref, g1_ref, g2_ref, x1_ref, hn_ref, *, n_first):
    x = jnp.where(pl.program_id(0) < n_first, xa_ref[0], xb_ref[0])
    x1 = x + _rms(f_ref[0], g1_ref[...])
    x1_ref[0] = x1
    hn_ref[0] = _rms(x1, g2_ref[...]).astype(BF16)


def _post_mix(xs, f, g1, g2, *, L):
    D = xs[0].shape[-1]
    n_first = xs[0].shape[0]
    Bt = sum(x.shape[0] for x in xs)
    tm = _pick(L, (256, 128))
    nt = L // tm
    block = (1, tm, D)
    index = lambda b, i: (b, i, 0)
    if len(xs) == 2:
        spec_a, spec_b = _pair_specs(block, n_first, nt - 1, index)
    else:
        spec_a = spec_b = pl.BlockSpec(block, index)
        xs = list(xs) * 2
    row = pl.BlockSpec(block, index)
    vec = pl.BlockSpec((1, D), lambda b, i: (0, 0))
    x1, hn = pl.pallas_call(
        functools.partial(_post_mix_kernel, n_first=n_first),
        grid=(Bt, nt),
        in_specs=[spec_a, spec_b, row, vec, vec],
        out_specs=[row, row],
        out_shape=[jax.ShapeDtypeStruct((Bt, L, D), F32), jax.ShapeDtypeStruct((Bt, L, D), BF16)],
        compiler_params=_params(("parallel", "parallel")),
        name="post_mix",
    )(xs[0], xs[1], f.reshape(Bt, L, D), g1.reshape(1, D), g2.reshape(1, D))
    return x1.reshape(Bt * L, D), hn.reshape(Bt * L, D)


def _post_ffn_pair_kernel(x_ref, f_ref, g_ref, ya_ref, yb_ref, *, n_first):
    y = x_ref[0] + _rms(f_ref[0], g_ref[...])

    @pl.when(pl.program_id(0) < n_first)
    def _():
        ya_ref[0] = y

    @pl.when(pl.program_id(0) >= n_first)
    def _():
        yb_ref[0] = y


def _post_ffn_kernel(x_ref, f_ref, g_ref, y_ref):
    y_ref[0] = x_ref[0] + _rms(f_ref[0], g_ref[...])


def _post_ffn(x, f, g, *, L, splits):
    T, D = x.shape
    Bt = T // L
    tm = _pick(L, (256, 128))
    nt = L // tm
    block = (1, tm, D)
    index = lambda b, i: (b, i, 0)
    row = pl.BlockSpec(block, index)
    args = (x.reshape(Bt, L, D), f.reshape(Bt, L, D), g.reshape(1, D))
    in_specs = [row, row, pl.BlockSpec((1, D), lambda b, i: (0, 0))]
    if len(splits) == 1:
        return (pl.pallas_call(
            _post_ffn_kernel,
            grid=(Bt, nt),
            in_specs=in_specs,
            out_specs=row,
            out_shape=jax.ShapeDtypeStruct((Bt, L, D), F32),
            compiler_params=_params(("parallel", "parallel")),
            name="post_ffn",
        )(*args),)
    n_first = splits[0]
    spec_a, spec_b = _pair_specs(block, n_first, nt - 1, index)
    return tuple(pl.pallas_call(
        functools.partial(_post_ffn_pair_kernel, n_first=n_first),
        grid=(Bt, nt),
        in_specs=in_specs,
        out_specs=[spec_a, spec_b],
        out_shape=[jax.ShapeDtypeStruct((n, L, D), F32) for n in splits],
        compiler_params=_params(("arbitrary", "arbitrary")),
        name="post_ffn",
    )(*args))


FFN_LAG = 2


def _ffn_in_kernel(a_ref, wg_ref, wu_ref, cw_ref, cb_ref, o_ref, g_ref, u_ref, *, tm, rc, n_tiles, tiles_per_seq):
    s = pl.program_id(0)
    ring = FFN_LAG + 1

    @pl.when(s == 0)
    def _():
        g_ref[...] = jnp.zeros_like(g_ref)
        u_ref[...] = jnp.zeros_like(u_ref)

    new = s % ring
    mid = (s + 1) % ring
    nxt = (s + 2) % ring

    e = jnp.maximum(s - FFN_LAG, 0) % n_tiles
    first = (e % tiles_per_seq) == 0
    last = (e % tiles_per_seq) == tiles_per_seq - 1
    cw = cw_ref[...]
    cb = cb_ref[...]
    pad = SUBLANES
    for r0 in range(0, tm, rc):
        a = a_ref[r0:r0 + rc, :]
        g_new = jnp.dot(a, wg_ref[...], preferred_element_type=F32)
        u_new = jnp.dot(a, wu_ref[...], preferred_element_type=F32)

        if r0 == 0:
            top = jnp.where(first, 0.0, g_ref[new, tm - pad:tm, :])
        else:
            top = g_ref[mid, r0 - pad:r0, :]
        if r0 + rc == tm:
            bottom = jnp.where(last, 0.0, g_ref[nxt, 0:pad, :])
        else:
            bottom = g_ref[mid, r0 + rc:r0 + rc + pad, :]
        big = jnp.concatenate([top, g_ref[mid, r0:r0 + rc, :], bottom], axis=0)
        y = (big[pad - 1:pad - 1 + rc] * cw[0:1, :] + big[pad:pad + rc] * cw[1:2, :]
             + big[pad + 1:pad + 1 + rc] * cw[2:3, :] + cb)
        o_ref[r0:r0 + rc, :] = (jax.nn.gelu(y) * u_ref[mid, r0:r0 + rc, :]).astype(BF16)

        g_ref[new, r0:r0 + rc, :] = g_new
        u_ref[new, r0:r0 + rc, :] = u_new


def _ffn_in(a, w, cw, cb, *, L, d_ff):
    T, D = a.shape
    tm = _pick(L, (1024, 512, 256, 128))
    tn = _pick(d_ff, (256, 128))
    ncol = d_ff // tn
    n_tiles = T // tm
    steps = ncol * n_tiles

    def mm_tile(s):
        return jnp.minimum(s, steps - 1)

    def ew_tile(s):
        return jnp.maximum(s - FFN_LAG, 0)

    kern = functools.partial(_ffn_in_kernel, tm=tm, rc=_pick(tm, (256, 128)), n_tiles=n_tiles, tiles_per_seq=L // tm)
    return pl.pallas_call(
        kern,
        grid=(steps + FFN_LAG,),
        in_specs=[
            pl.BlockSpec((tm, D), lambda s: (mm_tile(s) % n_tiles, 0)),
            pl.BlockSpec((D, tn), lambda s: (0, mm_tile(s) // n_tiles)),
            pl.BlockSpec((D, tn), lambda s: (0, ncol + mm_tile(s) // n_tiles)),
            pl.BlockSpec((3, tn), lambda s: (0, ew_tile(s) // n_tiles)),
            pl.BlockSpec((1, tn), lambda s: (0, ew_tile(s) // n_tiles)),
        ],
        out_specs=pl.BlockSpec((tm, tn), lambda s: (ew_tile(s) % n_tiles, ew_tile(s) // n_tiles)),
        out_shape=jax.ShapeDtypeStruct((T, d_ff), BF16),
        scratch_shapes=[pltpu.VMEM((FFN_LAG + 1, tm, tn), F32), pltpu.VMEM((FFN_LAG + 1, tm, tn), F32)],
        compiler_params=_params(("arbitrary",)),
        name="ffn_in",
    )(a, w, w, cw, cb.reshape(1, d_ff))


def _lane_bcast(v):
    return jnp.broadcast_to(v[..., None], v.shape + (LANES,))


def _hyena_filter_spectrum(L, d_hy, n_order, w1, b1, w2, b2, w3, b3, w4, freq, tabs):
    emb = w1.shape[0]
    bands = (emb - 1) // 2
    n = jnp.arange(2 * L)
    m = jnp.where(n < L, n, 2 * L - n).astype(F32)
    sign = jnp.where(n < L, 1.0, jnp.where(n == L, 0.0, -1.0)).astype(F32)
    t = m / (L - 1)
    band = jnp.linspace(1e-4, bands - 1, bands, dtype=F32)
    ang = (2.0 * math.pi / L) * m[None, :] * band[:, None]
    zt = jnp.concatenate([t[None, :], jnp.cos(ang), -jnp.sin(ang)], axis=0)
    kpad = -(-emb // BF16_ROWS) * BF16_ROWS
    zt = jnp.pad(zt, ((0, kpad - emb), (0, 0)))
    aux = jnp.zeros((SUBLANES, 2 * L), F32).at[0].set(sign).at[1].set(t)
    w1t = jnp.pad(w1.T, ((0, 0), (0, kpad - emb))).astype(BF16)
    fw = w2.shape[0]
    cols = jnp.zeros((fw, LANES), F32)
    cols = cols.at[:, 0].set(freq).at[:, 1].set(b1).at[:, 2].set(b2).at[:, 3].set(b3)
    max_decay = math.log(HY_TARGET) / HY_FAST_DECAY
    min_decay = math.log(HY_TARGET) / HY_SLOW_DECAY
    delta = _lane_bcast(jnp.linspace(min_decay, max_decay, d_hy, dtype=F32))
    kt = _filter_time(zt, aux, w1t, w2.T.astype(BF16), w3.T.astype(BF16), cols, w4.T.astype(BF16), delta,
                      L=L, d_hy=d_hy, n_order=n_order)
    return _filter_spec(kt, tabs)


def _encoder_layer(xs, p, splits):
    (pre_mix_norm, w_in, rg_conv_w, rg_conv_b, rg_a_w, rg_a_b, rg_x_w, rg_x_b, rg_lambda, hy_conv_w, hy_conv_b,
     hy_w1, hy_b1, hy_w2, hy_b2, hy_w3, hy_b3, hy_w4, hy_sin_freq, hy_bias, rg_out_norm, hy_out_norm, w_out,
     post_mix_norm, pre_ffn_norm, w_ffn_in, ffn_conv_w, ffn_conv_b, w_ffn_out, post_ffn_norm) = p
    _, L, D = xs[0].shape
    Bt = sum(x.shape[0] for x in xs)
    T = Bt * L
    d_rg = rg_conv_w.shape[-1]
    n_order, d_hy = hy_bias.shape
    d_ff = ffn_conv_w.shape[-1]
    nh = rg_a_w.shape[1]
    assert d_rg == nh * LANES and n_order == 2 and L % (LANES * BF16_ROWS) == 0

    hn = _norm_in(xs, pre_mix_norm, L=L)
    w_hyt = _transpose_cast(w_in, col0=2 * d_rg, ncols=3 * d_hy)
    tm = _pick(T, (1024, 512, 256, 128))
    w_rg = _cast_cols(w_in, col0=0, ncols=2 * d_rg)
    proj_rg = _mm(hn, w_rg, tm=tm, tn=_pick(2 * d_rg, (1024, 512, 256, 128)), name="proj_rg")

    w_cat = jnp.concatenate([rg_a_w[0], rg_x_w[0], rg_a_w[1], rg_x_w[1]], axis=-1).astype(BF16)
    b_cat = jnp.concatenate([rg_a_b[0], rg_x_b[0], rg_a_b[1], rg_x_b[1]], axis=0)
    b_cat = b_cat.reshape(4, nh, LANES).transpose(1, 0, 2).reshape(nh, 1, 4 * LANES)
    rg_y, hyt = _rglru_proj(proj_rg.reshape(Bt, L, 2 * d_rg), rg_conv_w, rg_conv_b.reshape(1, d_rg), w_cat, b_cat,
                            rg_lambda, w_hyt, hn, d_rg=d_rg)

    tabs = _dft_tables(L)
    ks = _hyena_filter_spectrum(L, d_hy, n_order, hy_w1, hy_b1, hy_w2, hy_b2, hy_w3, hy_b3, hy_w4, hy_sin_freq, tabs)
    cw = hy_conv_w.reshape(3, 3, d_hy)
    par = jnp.concatenate([cw.transpose(1, 0, 2).reshape(9, d_hy), hy_conv_b.reshape(3, d_hy), hy_bias,
                           jnp.zeros((2, d_hy), F32)], axis=0)
    par = _lane_bcast(par.T)
    hy_y = _hyena(hyt, par, ks, tabs, Bt=Bt, L=L, d_hy=d_hy)

    mix_rg = _rg_norm(rg_y.reshape(T, d_rg), rg_out_norm)
    mix_hy = _hy_norm(hy_y, _lane_bcast(hy_out_norm))
    f = _mm2(mix_rg, mix_hy, w_out.astype(BF16), tm=tm, tn=_pick(D, (1024, 512, 256, 128)), name="out_proj")
    x1, hn2 = _post_mix(xs, f, post_mix_norm, pre_ffn_norm, L=L)

    fg = _ffn_in(hn2, w_ffn_in.astype(BF16), ffn_conv_w, ffn_conv_b, L=L, d_ff=d_ff)
    f2 = _mm(fg, w_ffn_out.astype(BF16), tm=_pick(T, (512, 256, 128)), tn=_pick(D, (512, 256, 128)), name="ffn_out")
    return _post_ffn(x1, f2, post_ffn_norm, L=L, splits=splits)


def _run_trunk(xs, params):
    splits = tuple(x.shape[0] for x in xs)
    depth = params[0].shape[0]
    for l in range(depth):
        last = l == depth - 1
        xs = _encoder_layer(xs, [q[l] for q in params], splits if last else (sum(splits),))
    return xs


def kernel(x_prompt, x_sample, pre_mix_norm, w_in, rg_conv_w, rg_conv_b, rg_a_w, rg_a_b, rg_x_w, rg_x_b, rg_lambda, hy_conv_w, hy_conv_b, hy_w1, hy_b1, hy_w2, hy_b2, hy_w3, hy_b3, hy_w4, hy_sin_freq, hy_bias, rg_out_norm, hy_out_norm, w_out, post_mix_norm, pre_ffn_norm, w_ffn_in, ffn_conv_w, ffn_conv_b, w_ffn_out, post_ffn_norm):
    params = (pre_mix_norm, w_in, rg_conv_w, rg_conv_b, rg_a_w, rg_a_b, rg_x_w, rg_x_b, rg_lambda, hy_conv_w,
              hy_conv_b, hy_w1, hy_b1, hy_w2, hy_b2, hy_w3, hy_b3, hy_w4, hy_sin_freq, hy_bias, rg_out_norm,
              hy_out_norm, w_out, post_mix_norm, pre_ffn_norm, w_ffn_in, ffn_conv_w, ffn_conv_b, w_ffn_out,
              post_ffn_norm)
    if x_prompt.shape[1:] == x_sample.shape[1:]:
        return _run_trunk((x_prompt, x_sample), params)
    return (_run_trunk((x_prompt,), params)[0], _run_trunk((x_sample,), params)[0])
```

```python
import functools
import math

import numpy as np
import jax
import jax.numpy as jnp
from jax import lax
from jax.experimental import pallas as pl
from jax.experimental.pallas import tpu as pltpu

F32 = jnp.float32
BF16 = jnp.bfloat16

EPS = 1e-6
TINY = 1e-30
RG_C = 8.0
RG_CONV_LEFT = 2
HY_FAST_DECAY = 0.3
HY_SLOW_DECAY = 1.5
HY_TARGET = 1e-2

LANES = 128
SUBLANES = 8
BF16_ROWS = 16
VMEM_LIMIT = 56 * 1024 * 1024


def _pick(n, candidates):
    for c in candidates:
        if n % c == 0:
            return c
    raise ValueError(f"no tile in {candidates} divides {n}")


def _params(sem):
    return pltpu.CompilerParams(dimension_semantics=sem, vmem_limit_bytes=VMEM_LIMIT)


def _gelu(y):
    c = 2.0 * math.sqrt(2.0 / math.pi)
    w = y * (c + (c * 0.044715) * (y * y))
    return y / (1.0 + jnp.exp(-w))


def _rms(x, g):
    return x * lax.rsqrt(jnp.mean(x * x, axis=-1, keepdims=True) + EPS) * g


def _pair_specs(block, n_first, inner_last, make_index):
    def first(o, i):
        return make_index(jnp.minimum(o, n_first - 1), jnp.where(o < n_first, i, inner_last))

    def second(o, i):
        return make_index(jnp.maximum(o - n_first, 0), jnp.where(o >= n_first, i, 0))

    return pl.BlockSpec(block, first), pl.BlockSpec(block, second)


def _norm_in_kernel(xa_ref, xb_ref, g_ref, hn_ref, *, n_first):
    x = jnp.where(pl.program_id(0) < n_first, xa_ref[0], xb_ref[0])
    hn_ref[0] = _rms(x, g_ref[...]).astype(BF16)


def _norm_in(xs, g, *, L):
    D = xs[0].shape[-1]
    n_first = xs[0].shape[0]
    Bt = sum(x.shape[0] for x in xs)
    tm = _pick(L, (256, 128))
    nt = L // tm
    block = (1, tm, D)
    index = lambda b, i: (b, i, 0)
    if len(xs) == 2:
        spec_a, spec_b = _pair_specs(block, n_first, nt - 1, index)
    else:
        spec_a = spec_b = pl.BlockSpec(block, index)
        xs = list(xs) * 2
    hn = pl.pallas_call(
        functools.partial(_norm_in_kernel, n_first=n_first),
        grid=(Bt, nt),
        in_specs=[spec_a, spec_b, pl.BlockSpec((1, D), lambda b, i: (0, 0))],
        out_specs=pl.BlockSpec(block, index),
        out_shape=jax.ShapeDtypeStruct((Bt, L, D), BF16),
        compiler_params=_params(("parallel", "parallel")),
        name="norm_in",
    )(xs[0], xs[1], g.reshape(1, D))
    return hn.reshape(Bt * L, D)


def _mm_kernel(a_ref, b_ref, o_ref):
    o_ref[...] = jnp.dot(a_ref[...], b_ref[...], preferred_element_type=F32).astype(o_ref.dtype)


def _mm(a, b, *, tm, tn, name):
    M, K = a.shape
    _, N = b.shape
    return pl.pallas_call(
        _mm_kernel,
        grid=(N // tn, M // tm),
        in_specs=[pl.BlockSpec((tm, K), lambda j, i: (i, 0)), pl.BlockSpec((K, tn), lambda j, i: (0, j))],
        out_specs=pl.BlockSpec((tm, tn), lambda j, i: (i, j)),
        out_shape=jax.ShapeDtypeStruct((M, N), F32),
        compiler_params=_params(("parallel", "parallel")),
        name=name,
    )(a, b)


def _mm2_kernel(a1_ref, a2_ref, b_ref, o_ref):
    k1 = a1_ref.shape[1]
    acc = jnp.dot(a1_ref[...], b_ref[:k1, :], preferred_element_type=F32)
    o_ref[...] = acc + jnp.dot(a2_ref[...], b_ref[k1:, :], preferred_element_type=F32)


def _mm2(a1, a2, b, *, tm, tn, name):
    M, K1 = a1.shape
    K2 = a2.shape[1]
    N = b.shape[1]
    return pl.pallas_call(
        _mm2_kernel,
        grid=(N // tn, M // tm),
        in_specs=[pl.BlockSpec((tm, K1), lambda j, i: (i, 0)), pl.BlockSpec((tm, K2), lambda j, i: (i, 0)),
                  pl.BlockSpec((K1 + K2, tn), lambda j, i: (0, j))],
        out_specs=pl.BlockSpec((tm, tn), lambda j, i: (i, j)),
        out_shape=jax.ShapeDtypeStruct((M, N), F32),
        compiler_params=_params(("parallel", "parallel")),
        name=name,
    )(a1, a2, b)


def _cast_kernel(w_ref, o_ref):
    o_ref[...] = w_ref[...].astype(BF16)


def _cast_cols(w, *, col0, ncols):
    K = w.shape[0]
    tc = _pick(ncols, (512, 256, 128))
    assert col0 % tc == 0
    return pl.pallas_call(
        _cast_kernel,
        grid=(ncols // tc,),
        in_specs=[pl.BlockSpec((K, tc), lambda j: (0, col0 // tc + j))],
        out_specs=pl.BlockSpec((K, tc), lambda j: (0, j)),
        out_shape=jax.ShapeDtypeStruct((K, ncols), BF16),
        compiler_params=_params(("parallel",)),
        name="w_cast",
    )(w)


def _transpose_cast_kernel(w_ref, o_ref):
    o_ref[...] = w_ref[...].T.astype(BF16)


def _transpose_cast(w, *, col0, ncols):
    K = w.shape[0]
    tc = _pick(ncols, (256, 128))
    assert col0 % tc == 0
    return pl.pallas_call(
        _transpose_cast_kernel,
        grid=(ncols // tc,),
        in_specs=[pl.BlockSpec((K, tc), lambda j: (0, col0 // tc + j))],
        out_specs=pl.BlockSpec((tc, K), lambda j: (j, 0)),
        out_shape=jax.ShapeDtypeStruct((ncols, K), BF16),
        compiler_params=_params(("parallel",)),
        name="w_transpose",
    )(w)


def _scan_chunk(a, b, carry, reverse):
    R = a.shape[0]
    nv = R // SUBLANES
    a3 = a.reshape(nv, SUBLANES, LANES)
    b3 = b.reshape(nv, SUBLANES, LANES)
    row = lax.broadcasted_iota(jnp.int32, (nv, SUBLANES, LANES), 1)
    for d in (1, 2, 4):
        shift = SUBLANES - d if reverse else d
        valid = (row < SUBLANES - d) if reverse else (row >= d)
        sa = pltpu.roll(a3, shift, axis=1)
        sb = pltpu.roll(b3, shift, axis=1)
        b3 = b3 + jnp.where(valid, a3, 0.0) * sb
        a3 = a3 * jnp.where(valid, sa, 1.0)
    edge = 0 if reverse else SUBLANES - 1
    a_edge = jnp.broadcast_to(a3[:, edge:edge + 1, :], a3.shape)
    b_edge = jnp.broadcast_to(b3[:, edge:edge + 1, :], b3.shape)
    order = range(nv - 1, -1, -1) if reverse else range(nv)
    carries = [None] * nv
    for j in order:
        carries[j] = carry
        carry = b_edge[j] + a_edge[j] * carry
    h3 = b3 + a3 * jnp.stack(carries)
    return h3.reshape(R, LANES), carry


def _rglru_proj_kernel(x_ref, xprev_ref, xnext_ref, g_ref, cw_ref, cb_ref, w_ref, bias_ref, lam_ref, wa_ref, hb_ref,
                       y_ref, hy_ref, xc_ref, hf_ref, carry_ref, *, QR, R, nq, nmm):
    q = pl.program_id(2)
    pad = SUBLANES
    nsub = QR // R
    mrows = wa_ref.shape[0] // nmm
    per_mm = nsub // nmm
    cw = cw_ref[...]
    cb = cb_ref[...]
    decay = RG_C * jax.nn.softplus(-lam_ref[...])
    hw = 2 * LANES

    def gates(xc, d):
        z = jnp.dot(xc.astype(BF16), w_ref[0, :, d * hw:(d + 1) * hw], preferred_element_type=F32)
        z = z + bias_ref[0, :, d * hw:(d + 1) * hw]
        r = jax.nn.sigmoid(z[:, :LANES])
        i = jax.nn.sigmoid(z[:, LANES:])
        s = r * decay[d:d + 1, :]
        a = jnp.exp(-s)
        v = jnp.tanh(s) * (1.0 + a * a)
        b = (v * lax.rsqrt(jnp.maximum(v, TINY))) * (i * xc)
        return a, b

    def proj_rows(k):
        rows = slice(k * mrows, (k + 1) * mrows)
        hy_ref[rows, :] = lax.dot_general(wa_ref[rows, :], hb_ref[...], (((1,), (1,)), ((), ())),
                                          preferred_element_type=F32)

    @pl.when(q < nq)
    def _():
        carry = jnp.where(q == 0, 0.0, carry_ref[...])
        base = pl.multiple_of(q * QR, QR)
        for k in range(nsub):
            if k % per_mm == 0:
                proj_rows(k // per_mm)
            r0 = k * R
            top = jnp.where(q == 0, 0.0, xprev_ref[0]) if k == 0 else x_ref[0, r0 - pad:r0, :]
            bottom = jnp.where(q == nq - 1, 0.0, xnext_ref[0]) if k == nsub - 1 else x_ref[0, r0 + R:r0 + R + pad, :]
            big = jnp.concatenate([top, x_ref[0, r0:r0 + R, :], bottom], axis=0)
            xc = cb
            for t in range(cw.shape[0]):
                off = pad + t - RG_CONV_LEFT
                xc = xc + big[off:off + R, :] * cw[t:t + 1, :]
            xc_ref[pl.ds(base + r0, R), :] = xc
            a, b = gates(xc, 0)
            h, carry = _scan_chunk(a, b, carry, False)
            hf_ref[pl.ds(base + r0, R), :] = h
        carry_ref[...] = carry

    @pl.when(q >= nq)
    def _():
        carry = jnp.where(q == nq, 0.0, carry_ref[...])
        base = pl.multiple_of((2 * nq - 1 - q) * QR, QR)
        for k in range(nsub):
            if k % per_mm == 0:
                proj_rows(k // per_mm)
            r0 = (nsub - 1 - k) * R
            a, b = gates(xc_ref[pl.ds(base + r0, R), :], 1)
            h, carry = _scan_chunk(a, b, carry, True)
            gate = _gelu(g_ref[0, r0:r0 + R, :])
            y_ref[0, r0:r0 + R, :] = (hf_ref[pl.ds(base + r0, R), :] + h) * gate
        carry_ref[...] = carry


def _rglru_proj_tiles(n_steps, rows, cols, nsub):
    best = None
    for tm in range(nsub * BF16_ROWS, rows + 1, nsub * BF16_ROWS):
        if rows % tm or n_steps % (rows // tm):
            continue
        ncols = n_steps // (rows // tm)
        if cols % ncols or (cols // ncols) % LANES:
            continue
        tn = cols // ncols
        score = (min(tn, 2 * LANES), tm)
        if tm * tn <= 512 * 1024 and (best is None or score > best[0]):
            best = (score, (tm, tn))
    return None if best is None else best[1]


def _rglru_proj(proj, cw, cb, w_cat, b_cat, lam, w_hyt, hn, *, d_rg):
    Bt, L, _ = proj.shape
    nh = d_rg // LANES
    nq = 4
    QR = L // nq
    R = _pick(QR, (512, 256, 128, 64, 32, 16, 8))
    nsub = QR // R
    nmm = _pick(nsub, (4, 2, 1))
    n_steps = Bt * nh * 2 * nq
    rows, K = w_hyt.shape
    T = hn.shape[0]
    tiles = _rglru_proj_tiles(n_steps, rows, T, nmm)
    assert tiles is not None
    tmh, tnh = tiles
    ncols = T // tnh
    qb = QR // SUBLANES

    def step(b, h, q):
        return (b * nh + h) * (2 * nq) + q

    def fq(q):
        return jnp.minimum(q, nq - 1)

    def bq(q):
        return jnp.where(q < nq, nq - 1, 2 * nq - 1 - q)

    kern = functools.partial(_rglru_proj_kernel, QR=QR, R=R, nq=nq, nmm=nmm)
    return pl.pallas_call(
        kern,
        grid=(Bt, nh, 2 * nq),
        in_specs=[
            pl.BlockSpec((1, QR, LANES), lambda b, h, q: (b, fq(q), h)),
            pl.BlockSpec((1, SUBLANES, LANES), lambda b, h, q: (b, jnp.maximum(fq(q) * qb - 1, 0), h)),
            pl.BlockSpec((1, SUBLANES, LANES), lambda b, h, q: (b, jnp.minimum((fq(q) + 1) * qb, L // SUBLANES - 1), h)),
            pl.BlockSpec((1, QR, LANES), lambda b, h, q: (b, bq(q), nh + h)),
            pl.BlockSpec((cw.shape[0], LANES), lambda b, h, q: (0, h)),
            pl.BlockSpec((1, LANES), lambda b, h, q: (0, h)),
            pl.BlockSpec((1, LANES, 4 * LANES), lambda b, h, q: (h, 0, 0)),
            pl.BlockSpec((1, 1, 4 * LANES), lambda b, h, q: (h, 0, 0)),
            pl.BlockSpec((2, LANES), lambda b, h, q: (0, h)),
            pl.BlockSpec((tmh, K), lambda b, h, q: (step(b, h, q) // ncols, 0)),
            pl.BlockSpec((tnh, K), lambda b, h, q: (step(b, h, q) % ncols, 0)),
        ],
        out_specs=[
            pl.BlockSpec((1, QR, LANES), lambda b, h, q: (b, bq(q), h)),
            pl.BlockSpec((tmh, tnh), lambda b, h, q: (step(b, h, q) // ncols, step(b, h, q) % ncols)),
        ],
        out_shape=[jax.ShapeDtypeStruct((Bt, L, d_rg), F32), jax.ShapeDtypeStruct((rows, T), F32)],
        scratch_shapes=[pltpu.VMEM((L, LANES), F32), pltpu.VMEM((L, LANES), F32), pltpu.VMEM((SUBLANES, LANES), F32)],
        compiler_params=_params(("arbitrary", "arbitrary", "arbitrary")),
        name="rglru_proj_hy",
    )(proj, proj, proj, proj, cw, cb, w_cat, b_cat, lam, w_hyt, hn)


def _dft_tables(L):
    n = 2 * L
    A = L // LANES
    nb = np.arange(2 * A, dtype=np.float64)[None, :]
    kb = np.arange(A, dtype=np.float64)[:, None] + 0.5
    phi = 2.0 * np.pi * kb * nb / (2 * A)
    fr_full = np.concatenate([np.cos(phi), -np.sin(phi)], 0)
    na = np.arange(LANES, dtype=np.float64)[None, :]
    psi = 2.0 * np.pi * kb * na / n
    tw = np.stack([np.cos(psi), -np.sin(psi)])
    aa = 2.0 * np.pi * np.outer(np.arange(LANES), np.arange(LANES)) / LANES
    c, s = np.cos(aa), np.sin(aa)
    lf = np.block([[c, -s], [s, c]])
    lb = np.block([[c, s], [-s, c]])
    theta = phi[:, :A].T
    gi = np.concatenate([np.cos(theta), -np.sin(theta)], 1) * (2.0 / n)
    return dict(
        A=A,
        fr_full=jnp.asarray(fr_full, BF16), fr_half=jnp.asarray(fr_full[:, :A], BF16),
        tw=jnp.asarray(tw, F32), lf=jnp.asarray(lf, BF16), lb=jnp.asarray(lb, BF16), gi=jnp.asarray(gi, BF16),
    )


def _row_stage(fr_ref, tiles, tw_ref, dst_ref, c0):
    A = tw_ref.shape[1]
    y = jnp.dot(fr_ref[...], jnp.concatenate(tiles, axis=1), preferred_element_type=F32)
    twr = tw_ref[0]
    twi = tw_ref[1]
    for q in range(len(tiles)):
        yr = y[:A, q * LANES:(q + 1) * LANES]
        yi = y[A:, q * LANES:(q + 1) * LANES]
        dst_ref[c0 + q, :, :LANES] = (yr * twr - yi * twi).astype(BF16)
        dst_ref[c0 + q, :, LANES:] = (yr * twi + yi * twr).astype(BF16)


def _lane_stage(src_ref, m_ref, c0, gc):
    A = src_ref.shape[1]
    a2 = src_ref[pl.ds(c0, gc)].reshape(gc * A, 2 * LANES)
    return jnp.dot(a2, m_ref[...], preferred_element_type=F32).reshape(gc, A, 2 * LANES)


def _filter_time_kernel(zt_ref, aux_ref, w1_ref, w2_ref, w3_ref, col_ref, w4_ref, delta_ref, o_ref, h3_ref):
    @pl.when(pl.program_id(1) == 0)
    def _():
        fr = col_ref[:, 0:1]
        h = jnp.sin(fr * (jnp.dot(w1_ref[...], zt_ref[...].astype(BF16), preferred_element_type=F32) + col_ref[:, 1:2]))
        h = jnp.sin(fr * (jnp.dot(w2_ref[...], h.astype(BF16), preferred_element_type=F32) + col_ref[:, 2:3]))
        h = jnp.sin(fr * (jnp.dot(w3_ref[...], h.astype(BF16), preferred_element_type=F32) + col_ref[:, 3:4]))
        h3_ref[...] = h.astype(BF16)

    sign = aux_ref[0:1, :]
    t = aux_ref[1:2, :]
    k = jnp.dot(w4_ref[...], h3_ref[...], preferred_element_type=F32)
    o_ref[...] = k * jnp.exp(-t * jnp.abs(delta_ref[:, 0:1])) * sign


def _filter_time(zt, aux, w1t, w2t, w3t, cols, w4t, delta, *, L, d_hy, n_order):
    fw = w2t.shape[0]
    lt = _pick(L, (1024, 512, 256, 128))
    nt = 2 * L // lt
    tr = _pick(d_hy, (512, 256, 128))
    nr = d_hy // tr
    const = lambda j, r: (0, 0)

    def w4_map(j, r):
        direction = (j >= nt // 2).astype(jnp.int32)
        return (((r // nr) * 2 + direction) * nr + r % nr, 0)

    return pl.pallas_call(
        _filter_time_kernel,
        grid=(nt, n_order * nr),
        in_specs=[
            pl.BlockSpec((zt.shape[0], lt), lambda j, r: (0, j)),
            pl.BlockSpec((SUBLANES, lt), lambda j, r: (0, j)),
            pl.BlockSpec(w1t.shape, const),
            pl.BlockSpec(w2t.shape, const),
            pl.BlockSpec(w3t.shape, const),
            pl.BlockSpec(cols.shape, const),
            pl.BlockSpec((tr, fw), w4_map),
            pl.BlockSpec((tr, LANES), lambda j, r: (r % nr, 0)),
        ],
        out_specs=pl.BlockSpec((tr, lt), lambda j, r: (r, j)),
        out_shape=jax.ShapeDtypeStruct((n_order * d_hy, 2 * L), F32),
        scratch_shapes=[pltpu.VMEM((fw, lt), BF16)],
        compiler_params=_params(("arbitrary", "arbitrary")),
        name="hy_filter_time",
    )(zt, aux, w1t, w2t, w3t, cols, w4t, delta)


def _filter_spec_kernel(k_ref, fr_ref, tw_ref, lf_ref, o_ref, a_ref, *, cb, gc):
    rows = fr_ref.shape[1]
    for c0 in range(0, cb, gc):
        k = k_ref[pl.ds(c0, gc), :].reshape(gc, rows, LANES).astype(BF16)
        _row_stage(fr_ref, [k[q] for q in range(gc)], tw_ref, a_ref, c0)
    for c0 in range(0, cb, gc):
        o_ref[pl.ds(c0, gc)] = _lane_stage(a_ref, lf_ref, c0, gc)


def _filter_spec(kt, tabs):
    C = kt.shape[0]
    A = tabs["A"]
    cb = _pick(C, (32, 16, 8))
    gc = 8
    kern = functools.partial(_filter_spec_kernel, cb=cb, gc=gc)
    return pl.pallas_call(
        kern,
        grid=(C // cb,),
        in_specs=[
            pl.BlockSpec((cb, 2 * A * LANES), lambda i: (i, 0)),
            pl.BlockSpec((2 * A, 2 * A), lambda i: (0, 0)),
            pl.BlockSpec((2, A, LANES), lambda i: (0, 0, 0)),
            pl.BlockSpec((2 * LANES, 2 * LANES), lambda i: (0, 0)),
        ],
        out_specs=pl.BlockSpec((cb, A, 2 * LANES), lambda i: (i, 0, 0)),
        out_shape=jax.ShapeDtypeStruct((C, A, 2 * LANES), F32),
        scratch_shapes=[pltpu.VMEM((cb, A, 2 * LANES), BF16)],
        compiler_params=_params(("parallel",)),
        name="hy_filter_spec",
    )(kt, tabs["fr_full"], tabs["tw"], tabs["lf"])


def _shift_time(x, step):
    L = x.shape[1]
    lane = lax.broadcasted_iota(jnp.int32, (x.shape[0], LANES), 1)
    if step == 1:
        r = pltpu.roll(x, 1, axis=1)
        return jnp.concatenate([jnp.where(lane == 0, 0.0, r[:, :LANES]), r[:, LANES:]], axis=1)
    r = pltpu.roll(x, L - 1, axis=1)
    return jnp.concatenate([r[:, :L - LANES], jnp.where(lane == LANES - 1, 0.0, r[:, L - LANES:])], axis=1)


def _hyena_kernel(v_ref, x1_ref, x2_ref, par_ref, ks0_ref, ks1_ref, fr_ref, tw_ref, lf_ref, lb_ref, gi_ref, o_ref,
                  hv_ref, hx1_ref, hx2_ref, ub_ref, a_ref, b_ref, *, cb, gc, cc):
    A = tw_ref.shape[1]
    groups = [g * gc for g in range(cb // gc)]

    def short_conv(s, src, dst, c0):
        par = par_ref[pl.ds(c0, cc)]

        def coef(k):
            return jnp.tile(par[:, k, :], (1, A))

        x = src[pl.ds(c0, cc), :]
        y = (_shift_time(x, 1) * coef(3 * s) + x * coef(3 * s + 1) + _shift_time(x, -1) * coef(3 * s + 2)
             + coef(9 + s))
        y = y.reshape(cc, A, LANES)
        dst[pl.ds(c0, cc)] = y
        return y

    for c0 in range(0, cb, cc):
        ub_ref[pl.ds(c0, cc)] = short_conv(0, v_ref, hv_ref, c0).astype(BF16)

    twr = tw_ref[0]
    twi = tw_ref[1]

    def long_conv(ks_ref, finish, side_work):
        for c0 in groups:
            _row_stage(fr_ref, [ub_ref[c0 + q] for q in range(gc)], tw_ref, a_ref, c0)
        for c0 in groups:
            x = _lane_stage(a_ref, lf_ref, c0, gc)
            xr = x[:, :, :LANES]
            xi = x[:, :, LANES:]
            kr = ks_ref[pl.ds(c0, gc), :, :LANES]
            ki = ks_ref[pl.ds(c0, gc), :, LANES:]
            b_ref[pl.ds(c0, gc), :, :LANES] = (xr * kr - xi * ki).astype(BF16)
            b_ref[pl.ds(c0, gc), :, LANES:] = (xr * ki + xi * kr).astype(BF16)
        for work in side_work:
            work()
        for c0 in groups:
            qv = _lane_stage(b_ref, lb_ref, c0, gc)
            qr = qv[:, :, :LANES]
            qi = qv[:, :, LANES:]
            a_ref[pl.ds(c0, gc), :, :LANES] = (qr * twr + qi * twi).astype(BF16)
            a_ref[pl.ds(c0, gc), :, LANES:] = (qi * twr - qr * twi).astype(BF16)
        for c0 in groups:
            rhs = jnp.concatenate(
                [jnp.concatenate([a_ref[c0 + q, :, :LANES], a_ref[c0 + q, :, LANES:]], axis=0) for q in range(gc)], axis=1)
            y = jnp.dot(gi_ref[...], rhs, preferred_element_type=F32)
            for q in range(gc):
                finish(c0 + q, y[:, q * LANES:(q + 1) * LANES])

    def finish0(c, y):
        z = hx1_ref[c] * (y + hv_ref[c] * par_ref[c, 12:13, :])
        hx1_ref[c] = z
        ub_ref[c] = z.astype(BF16)

    long_conv(ks0_ref, finish0,
              [functools.partial(short_conv, 1, x1_ref, hx1_ref, c0) for c0 in range(0, cb, cc)])

    def finish1(c, y):
        o_ref[c] = hx2_ref[c] * (y + hx1_ref[c] * par_ref[c, 13:14, :])

    long_conv(ks1_ref, finish1,
              [functools.partial(short_conv, 2, x2_ref, hx2_ref, c0) for c0 in range(0, cb, cc)])


def _hyena(hyt, par, ks, tabs, *, Bt, L, d_hy):
    A = tabs["A"]
    cb = _pick(d_hy, (32, 16, 8))
    gc = 8
    ncb = d_hy // cb
    kern = functools.partial(_hyena_kernel, cb=cb, gc=gc, cc=SUBLANES)
    const2 = lambda j, b: (0, 0)
    tile = (cb, A, LANES)
    return pl.pallas_call(
        kern,
        grid=(ncb, Bt),
        in_specs=[
            pl.BlockSpec((cb, L), lambda j, b: (j, b)),
            pl.BlockSpec((cb, L), lambda j, b: (ncb + j, b)),
            pl.BlockSpec((cb, L), lambda j, b: (2 * ncb + j, b)),
            pl.BlockSpec((cb, 16, LANES), lambda j, b: (j, 0, 0)),
            pl.BlockSpec((cb, A, 2 * LANES), lambda j, b: (j, 0, 0)),
            pl.BlockSpec((cb, A, 2 * LANES), lambda j, b: (ncb + j, 0, 0)),
            pl.BlockSpec((2 * A, A), const2),
            pl.BlockSpec((2, A, LANES), lambda j, b: (0, 0, 0)),
            pl.BlockSpec((2 * LANES, 2 * LANES), const2),
            pl.BlockSpec((2 * LANES, 2 * LANES), const2),
            pl.BlockSpec((A, 2 * A), const2),
        ],
        out_specs=pl.BlockSpec(tile, lambda j, b: (j, b, 0)),
        out_shape=jax.ShapeDtypeStruct((d_hy, Bt * A, LANES), F32),
        scratch_shapes=[
            pltpu.VMEM(tile, F32),
            pltpu.VMEM(tile, F32),
            pltpu.VMEM(tile, F32),
            pltpu.VMEM(tile, BF16),
            pltpu.VMEM((cb, A, 2 * LANES), BF16),
            pltpu.VMEM((cb, A, 2 * LANES), BF16),
        ],
        compiler_params=_params(("parallel", "arbitrary")),
        name="hyena",
    )(hyt, hyt, hyt, par, ks, ks, tabs["fr_half"], tabs["tw"], tabs["lf"], tabs["lb"], tabs["gi"])


def _rg_norm_kernel(x_ref, g_ref, o_ref):
    o_ref[...] = _rms(x_ref[...], g_ref[...]).astype(BF16)


def _rg_norm(x, g):
    T, C = x.shape
    tm = _pick(T, (512, 256, 128))
    return pl.pallas_call(
        _rg_norm_kernel,
        grid=(T // tm,),
        in_specs=[pl.BlockSpec((tm, C), lambda i: (i, 0)), pl.BlockSpec((1, C), lambda i: (0, 0))],
        out_specs=pl.BlockSpec((tm, C), lambda i: (i, 0)),
        out_shape=jax.ShapeDtypeStruct((T, C), BF16),
        compiler_params=_params(("parallel",)),
        name="rg_norm",
    )(x, g.reshape(1, C))


def _hy_norm_kernel(hy_ref, g_ref, o_ref):
    d_hy, rows, _ = hy_ref.shape
    h = hy_ref[...].reshape(d_hy, rows * LANES)
    scale = lax.rsqrt(jnp.mean(h * h, axis=0, keepdims=True) + EPS)
    o_ref[...] = (h * scale * g_ref[:, 0:1]).T.astype(BF16)


def _hy_norm(hy3, g_col):
    d_hy, nrows, _ = hy3.shape
    rows = SUBLANES
    return pl.pallas_call(
        _hy_norm_kernel,
        grid=(nrows // rows,),
        in_specs=[pl.BlockSpec((d_hy, rows, LANES), lambda i: (0, i, 0)), pl.BlockSpec((d_hy, LANES), lambda i: (0, 0))],
        out_specs=pl.BlockSpec((rows * LANES, d_hy), lambda i: (i, 0)),
        out_shape=jax.ShapeDtypeStruct((nrows * LANES, d_hy), BF16),
        compiler_params=_params(("parallel",)),
        name="hy_norm",
    )(hy3, g_col)


def _post_mix_kernel(xa_ref, xb_ref, f_ref, g1_ref, g2_ref, x1_ref, hn_ref, *, n_first):
    x = jnp.where(pl.program_id(0) < n_first, xa_ref[0], xb_ref[0])
    x1 = x + _rms(f_ref[0], g1_ref[...])
    x1_ref[0] = x1
    hn_ref[0] = _rms(x1, g2_ref[...]).astype(BF16)


def _post_mix(xs, f, g1, g2, *, L):
    D = xs[0].shape[-1]
    n_first = xs[0].shape[0]
    Bt = sum(x.shape[0] for x in xs)
    tm = _pick(L, (256, 128))
    nt = L // tm
    block = (1, tm, D)
    index = lambda b, i: (b, i, 0)
    if len(xs) == 2:
        spec_a, spec_b = _pair_specs(block, n_first, nt - 1, index)
    else:
        spec_a = spec_b = pl.BlockSpec(block, index)
        xs = list(xs) * 2
    row = pl.BlockSpec(block, index)
    vec = pl.BlockSpec((1, D), lambda b, i: (0, 0))
    x1, hn = pl.pallas_call(
        functools.partial(_post_mix_kernel, n_first=n_first),
        grid=(Bt, nt),
        in_specs=[spec_a, spec_b, row, vec, vec],
        out_specs=[row, row],
        out_shape=[jax.ShapeDtypeStruct((Bt, L, D), F32), jax.ShapeDtypeStruct((Bt, L, D), BF16)],
        compiler_params=_params(("parallel", "parallel")),
        name="post_mix",
    )(xs[0], xs[1], f.reshape(Bt, L, D), g1.reshape(1, D), g2.reshape(1, D))
    return x1.reshape(Bt * L, D), hn.reshape(Bt * L, D)


def _post_ffn_pair_kernel(x_ref, f_ref, g_ref, ya_ref, yb_ref, *, n_first):
    y = x_ref[0] + _rms(f_ref[0], g_ref[...])

    @pl.when(pl.program_id(0) < n_first)
    def _():
        ya_ref[0] = y

    @pl.when(pl.program_id(0) >= n_first)
    def _():
        yb_ref[0] = y


def _post_ffn_kernel(x_ref, f_ref, g_ref, y_ref):
    y_ref[0] = x_ref[0] + _rms(f_ref[0], g_ref[...])


def _post_ffn(x, f, g, *, L, splits):
    T, D = x.shape
    Bt = T // L
    tm = _pick(L, (256, 128))
    nt = L // tm
    block = (1, tm, D)
    index = lambda b, i: (b, i, 0)
    row = pl.BlockSpec(block, index)
    args = (x.reshape(Bt, L, D), f.reshape(Bt, L, D), g.reshape(1, D))
    in_specs = [row, row, pl.BlockSpec((1, D), lambda b, i: (0, 0))]
    if len(splits) == 1:
        return (pl.pallas_call(
            _post_ffn_kernel,
            grid=(Bt, nt),
            in_specs=in_specs,
            out_specs=row,
            out_shape=jax.ShapeDtypeStruct((Bt, L, D), F32),
            compiler_params=_params(("parallel", "parallel")),
            name="post_ffn",
        )(*args),)
    n_first = splits[0]
    spec_a, spec_b = _pair_specs(block, n_first, nt - 1, index)
    return tuple(pl.pallas_call(
        functools.partial(_post_ffn_pair_kernel, n_first=n_first),
        grid=(Bt, nt),
        in_specs=in_specs,
        out_specs=[spec_a, spec_b],
        out_shape=[jax.ShapeDtypeStruct((n, L, D), F32) for n in splits],
        compiler_params=_params(("arbitrary", "arbitrary")),
        name="post_ffn",
    )(*args))


FFN_LAG = 2


def _ffn_in_kernel(a_ref, wg_ref, wu_ref, cw_ref, cb_ref, o_ref, g_ref, u_ref, *, tm, rc, n_tiles, tiles_per_seq):
    s = pl.program_id(0)
    ring = FFN_LAG + 1

    @pl.when(s == 0)
    def _():
        g_ref[...] = jnp.zeros_like(g_ref)
        u_ref[...] = jnp.zeros_like(u_ref)

    new = s % ring
    mid = (s + 1) % ring
    nxt = (s + 2) % ring

    e = jnp.maximum(s - FFN_LAG, 0) % n_tiles
    first = (e % tiles_per_seq) == 0
    last = (e % tiles_per_seq) == tiles_per_seq - 1
    cw = cw_ref[...]
    cb = cb_ref[...]
    pad = SUBLANES
    for r0 in range(0, tm, rc):
        a = a_ref[r0:r0 + rc, :]
        g_new = jnp.dot(a, wg_ref[...], preferred_element_type=F32)
        u_new = jnp.dot(a, wu_ref[...], preferred_element_type=F32)

        if r0 == 0:
            top = jnp.where(first, 0.0, g_ref[new, tm - pad:tm, :])
        else:
            top = g_ref[mid, r0 - pad:r0, :]
        if r0 + rc == tm:
            bottom = jnp.where(last, 0.0, g_ref[nxt, 0:pad, :])
        else:
            bottom = g_ref[mid, r0 + rc:r0 + rc + pad, :]
        big = jnp.concatenate([top, g_ref[mid, r0:r0 + rc, :], bottom], axis=0)
        y = (big[pad - 1:pad - 1 + rc] * cw[0:1, :] + big[pad:pad + rc] * cw[1:2, :]
             + big[pad + 1:pad + 1 + rc] * cw[2:3, :] + cb)
        o_ref[r0:r0 + rc, :] = (_gelu(y) * u_ref[mid, r0:r0 + rc, :]).astype(BF16)

        g_ref[new, r0:r0 + rc, :] = g_new
        u_ref[new, r0:r0 + rc, :] = u_new


def _ffn_in(a, w, cw, cb, *, L, d_ff):
    T, D = a.shape
    tm = _pick(L, (1024, 512, 256, 128))
    tn = _pick(d_ff, (256, 128))
    ncol = d_ff // tn
    n_tiles = T // tm
    steps = ncol * n_tiles

    def mm_tile(s):
        return jnp.minimum(s, steps - 1)

    def ew_tile(s):
        return jnp.maximum(s - FFN_LAG, 0)

    kern = functools.partial(_ffn_in_kernel, tm=tm, rc=_pick(tm, (256, 128)), n_tiles=n_tiles, tiles_per_seq=L // tm)
    return pl.pallas_call(
        kern,
        grid=(steps + FFN_LAG,),
        in_specs=[
            pl.BlockSpec((tm, D), lambda s: (mm_tile(s) % n_tiles, 0)),
            pl.BlockSpec((D, tn), lambda s: (0, mm_tile(s) // n_tiles)),
            pl.BlockSpec((D, tn), lambda s: (0, ncol + mm_tile(s) // n_tiles)),
            pl.BlockSpec((3, tn), lambda s: (0, ew_tile(s) // n_tiles)),
            pl.BlockSpec((1, tn), lambda s: (0, ew_tile(s) // n_tiles)),
        ],
        out_specs=pl.BlockSpec((tm, tn), lambda s: (ew_tile(s) % n_tiles, ew_tile(s) // n_tiles)),
        out_shape=jax.ShapeDtypeStruct((T, d_ff), BF16),
        scratch_shapes=[pltpu.VMEM((FFN_LAG + 1, tm, tn), F32), pltpu.VMEM((FFN_LAG + 1, tm, tn), F32)],
        compiler_params=_params(("arbitrary",)),
        name="ffn_in",
    )(a, w, w, cw, cb.reshape(1, d_ff))


def _lane_bcast(v):
    return jnp.broadcast_to(v[..., None], v.shape + (LANES,))


def _hyena_filter_spectrum(L, d_hy, n_order, w1, b1, w2, b2, w3, b3, w4, freq, tabs):
    emb = w1.shape[0]
    bands = (emb - 1) // 2
    n = jnp.arange(2 * L)
    m = jnp.where(n < L, n, 2 * L - n).astype(F32)
    sign = jnp.where(n < L, 1.0, jnp.where(n == L, 0.0, -1.0)).astype(F32)
    t = m / (L - 1)
    band = jnp.linspace(1e-4, bands - 1, bands, dtype=F32)
    ang = (2.0 * math.pi / L) * m[None, :] * band[:, None]
    zt = jnp.concatenate([t[None, :], jnp.cos(ang), -jnp.sin(ang)], axis=0)
    kpad = -(-emb // BF16_ROWS) * BF16_ROWS
    zt = jnp.pad(zt, ((0, kpad - emb), (0, 0)))
    aux = jnp.zeros((SUBLANES, 2 * L), F32).at[0].set(sign).at[1].set(t)
    w1t = jnp.pad(w1.T, ((0, 0), (0, kpad - emb))).astype(BF16)
    fw = w2.shape[0]
    cols = jnp.zeros((fw, LANES), F32)
    cols = cols.at[:, 0].set(freq).at[:, 1].set(b1).at[:, 2].set(b2).at[:, 3].set(b3)
    max_decay = math.log(HY_TARGET) / HY_FAST_DECAY
    min_decay = math.log(HY_TARGET) / HY_SLOW_DECAY
    delta = _lane_bcast(jnp.linspace(min_decay, max_decay, d_hy, dtype=F32))
    kt = _filter_time(zt, aux, w1t, w2.T.astype(BF16), w3.T.astype(BF16), cols, w4.T.astype(BF16), delta,
                      L=L, d_hy=d_hy, n_order=n_order)
    return _filter_spec(kt, tabs)


def _encoder_layer(xs, p, splits):
    (pre_mix_norm, w_in, rg_conv_w, rg_conv_b, rg_a_w, rg_a_b, rg_x_w, rg_x_b, rg_lambda, hy_conv_w, hy_conv_b,
     hy_w1, hy_b1, hy_w2, hy_b2, hy_w3, hy_b3, hy_w4, hy_sin_freq, hy_bias, rg_out_norm, hy_out_norm, w_out,
     post_mix_norm, pre_ffn_norm, w_ffn_in, ffn_conv_w, ffn_conv_b, w_ffn_out, post_ffn_norm) = p
    _, L, D = xs[0].shape
    Bt = sum(x.shape[0] for x in xs)
    T = Bt * L
    d_rg = rg_conv_w.shape[-1]
    n_order, d_hy = hy_bias.shape
    d_ff = ffn_conv_w.shape[-1]
    nh = rg_a_w.shape[1]
    assert d_rg == nh * LANES and n_order == 2 and L % (LANES * BF16_ROWS) == 0

    hn = _norm_in(xs, pre_mix_norm, L=L)
    w_hyt = _transpose_cast(w_in, col0=2 * d_rg, ncols=3 * d_hy)
    tm = _pick(T, (1024, 512, 256, 128))
    w_rg = _cast_cols(w_in, col0=0, ncols=2 * d_rg)
    proj_rg = _mm(hn, w_rg, tm=tm, tn=_pick(2 * d_rg, (1024, 512, 256, 128)), name="proj_rg")

    w_cat = jnp.concatenate([rg_a_w[0], rg_x_w[0], rg_a_w[1], rg_x_w[1]], axis=-1).astype(BF16)
    b_cat = jnp.concatenate([rg_a_b[0], rg_x_b[0], rg_a_b[1], rg_x_b[1]], axis=0)
    b_cat = b_cat.reshape(4, nh, LANES).transpose(1, 0, 2).reshape(nh, 1, 4 * LANES)
    rg_y, hyt = _rglru_proj(proj_rg.reshape(Bt, L, 2 * d_rg), rg_conv_w, rg_conv_b.reshape(1, d_rg), w_cat, b_cat,
                            rg_lambda, w_hyt, hn, d_rg=d_rg)

    tabs = _dft_tables(L)
    ks = _hyena_filter_spectrum(L, d_hy, n_order, hy_w1, hy_b1, hy_w2, hy_b2, hy_w3, hy_b3, hy_w4, hy_sin_freq, tabs)
    cw = hy_conv_w.reshape(3, 3, d_hy)
    par = jnp.concatenate([cw.transpose(1, 0, 2).reshape(9, d_hy), hy_conv_b.reshape(3, d_hy), hy_bias,
                           jnp.zeros((2, d_hy), F32)], axis=0)
    par = _lane_bcast(par.T)
    hy_y = _hyena(hyt, par, ks, tabs, Bt=Bt, L=L, d_hy=d_hy)

    mix_rg = _rg_norm(rg_y.reshape(T, d_rg), rg_out_norm)
    mix_hy = _hy_norm(hy_y, _lane_bcast(hy_out_norm))
    f = _mm2(mix_rg, mix_hy, w_out.astype(BF16), tm=tm, tn=_pick(D, (1024, 512, 256, 128)), name="out_proj")
    x1, hn2 = _post_mix(xs, f, post_mix_norm, pre_ffn_norm, L=L)

    fg = _ffn_in(hn2, w_ffn_in.astype(BF16), ffn_conv_w, ffn_conv_b, L=L, d_ff=d_ff)
    f2 = _mm(fg, w_ffn_out.astype(BF16), tm=_pick(T, (512, 256, 128)), tn=_pick(D, (512, 256, 128)), name="ffn_out")
    return _post_ffn(x1, f2, post_ffn_norm, L=L, splits=splits)


def _run_trunk(xs, params):
    splits = tuple(x.shape[0] for x in xs)
    depth = params[0].shape[0]
    for l in range(depth):
        last = l == depth - 1
        xs = _encoder_layer(xs, [q[l] for q in params], splits if last else (sum(splits),))
    return xs


def kernel(x_prompt, x_sample, pre_mix_norm, w_in, rg_conv_w, rg_conv_b, rg_a_w, rg_a_b, rg_x_w, rg_x_b, rg_lambda, hy_conv_w, hy_conv_b, hy_w1, hy_b1, hy_w2, hy_b2, hy_w3, hy_b3, hy_w4, hy_sin_freq, hy_bias, rg_out_norm, hy_out_norm, w_out, post_mix_norm, pre_ffn_norm, w_ffn_in, ffn_conv_w, ffn_conv_b, w_ffn_out, post_ffn_norm):
    params = (pre_mix_norm, w_in, rg_conv_w, rg_conv_b, rg_a_w, rg_a_b, rg_x_w, rg_x_b, rg_lambda, hy_conv_w,
              hy_conv_b, hy_w1, hy_b1, hy_w2, hy_b2, hy_w3, hy_b3, hy_w4, hy_sin_freq, hy_bias, rg_out_norm,
              hy_out_norm, w_out, post_mix_norm, pre_ffn_norm, w_ffn_in, ffn_conv_w, ffn_conv_b, w_ffn_out,
              post_ffn_norm)
    if x_prompt.shape[1:] == x_sample.shape[1:]:
        return _run_trunk((x_prompt, x_sample), params)
    return (_run_trunk((x_prompt,), params)[0], _run_trunk((x_sample,), params)[0])
```
